```python
import math
import jax
import jax.numpy as jnp
from jax import lax
import numpy as np

D_MODEL = 2048
BATCH = 16
SEQ = 2048
DEPTH = 4

CTX_LEN = 256
GRID_W = 64
EPS = 1e-6

D_MIX = D_MODEL
GROUP_W = D_MIX // 4

ATT_HEADS = 8
ATT_KV_HEADS = 2
ATT_REP = ATT_HEADS // ATT_KV_HEADS
HEAD_DIM = GROUP_W // ATT_HEADS
ATT_WINDOW = 128
ATT_BLOCK = 128
ROPE_THETA = 10000.0

CONV_CH = GROUP_W
CONV_K = 31

SSD_INNER = GROUP_W
SSD_HEAD_DIM = 64
SSD_HEADS = SSD_INNER // SSD_HEAD_DIM
SSD_GROUPS = 2
SSD_HEADS_PER_GROUP = SSD_HEADS // SSD_GROUPS
SSD_STATE = 128
SSD_CONV = 4
SSD_CONV_PAD = (2, 1)
SSD_CHUNK = 128
SSD_XBC = SSD_INNER + 2 * SSD_GROUPS * SSD_STATE

POOL_SIZES = (2, 4, 8, 16)
POOL_CH = GROUP_W // len(POOL_SIZES)

MOE_GROUPS = 4
MOE_PER_GROUP = 8
MOE_EXPERTS = MOE_GROUPS * MOE_PER_GROUP
MOE_TOPK = 2
D_EXPERT = 512

IN_SPLITS = (ATT_HEADS * HEAD_DIM, ATT_KV_HEADS * HEAD_DIM, ATT_KV_HEADS * HEAD_DIM,
             2 * CONV_CH, SSD_INNER, SSD_XBC, 2 * SSD_HEADS, GROUP_W)
D_IN = sum(IN_SPLITS)

kernel_name = 'hybrid_prefix_diffusion_trunk'


def rms_norm(x, g):
    xf = x.astype(jnp.float32)
    y = xf * lax.rsqrt(jnp.mean(xf * xf, axis=-1, keepdims=True) + EPS)
    return (y * g.astype(jnp.float32)).astype(x.dtype)


def layer_norm(x, g, b):
    xf = x.astype(jnp.float32)
    mu = jnp.mean(xf, axis=-1, keepdims=True)
    var = jnp.mean(jnp.square(xf - mu), axis=-1, keepdims=True)
    return ((xf - mu) * lax.rsqrt(var + EPS) * g.astype(jnp.float32) + b.astype(jnp.float32)).astype(x.dtype)


def depthwise_conv(x, w, b, pad):
    y = lax.conv_general_dilated(x, w[:, None, :].astype(x.dtype), (1,), [pad],
                                 dimension_numbers=('NWC', 'WIO', 'NWC'),
                                 feature_group_count=x.shape[-1])
    return y + b.astype(x.dtype)


def axial_rope_angles(n):
    rows = n // GRID_W
    row = jnp.repeat(jnp.arange(rows, dtype=jnp.float32), GRID_W)
    col = jnp.tile(jnp.arange(GRID_W, dtype=jnp.float32), rows)
    half = HEAD_DIM // 2
    inv_freq = 1.0 / (ROPE_THETA ** (jnp.arange(0, half, 2, dtype=jnp.float32) / half))
    return row[:, None] * inv_freq, col[:, None] * inv_freq


def rotate(x, ang):
    x1, x2 = jnp.split(x.astype(jnp.float32), 2, axis=-1)
    cos = jnp.cos(ang)[:, None, :]
    sin = jnp.sin(ang)[:, None, :]
    return jnp.concatenate([x1 * cos - x2 * sin, x2 * cos + x1 * sin], axis=-1).astype(x.dtype)


def axial_rope(x, ang_row, ang_col):
    half = HEAD_DIM // 2
    return jnp.concatenate([rotate(x[..., :half], ang_row), rotate(x[..., half:], ang_col)], axis=-1)


def sink_softmax(logits, sink):
    s = jnp.broadcast_to(sink.astype(jnp.float32)[:, :, None, None], logits.shape[:-1] + (1,))
    return jax.nn.softmax(jnp.concatenate([logits, s], axis=-1), axis=-1)[..., :-1]


def band_mask(n):
    nb = n // ATT_BLOCK
    i = jnp.arange(ATT_BLOCK)[:, None]
    j = jnp.arange(3 * ATT_BLOCK)[None, :]
    rel = j - ATT_BLOCK - i
    kpos = (jnp.arange(nb)[:, None, None] - 1) * ATT_BLOCK + j[None]
    return (jnp.abs(rel) <= ATT_WINDOW)[None] & (kpos >= 0) & (kpos < n)


def attention_mixer(q, k, v, qc, kc, vc, q_g, k_g, sink, ctx_out):
    b, n, _ = q.shape
    lc = kc.shape[1]
    scale = HEAD_DIM ** -0.5
    heads = lambda t, h: t.reshape(t.shape[0], t.shape[1], h, HEAD_DIM)
    ang_r, ang_c = axial_rope_angles(n)
    q = axial_rope(rms_norm(heads(q, ATT_HEADS), q_g), ang_r, ang_c)
    k = axial_rope(rms_norm(heads(k, ATT_KV_HEADS), k_g), ang_r, ang_c)
    v = heads(v, ATT_KV_HEADS)
    kc = rms_norm(heads(kc, ATT_KV_HEADS), k_g)
    vc = heads(vc, ATT_KV_HEADS)
    sink = sink.reshape(ATT_KV_HEADS, ATT_REP)
    nb = n // ATT_BLOCK
    qb = q.reshape(b, nb, ATT_BLOCK, ATT_KV_HEADS, ATT_REP, HEAD_DIM)

    def band(t):
        tp = jnp.pad(t, ((0, 0), (ATT_BLOCK, ATT_BLOCK), (0, 0), (0, 0)))
        tp = tp.reshape(b, nb + 2, ATT_BLOCK, ATT_KV_HEADS, HEAD_DIM)
        return jnp.concatenate([tp[:, :-2], tp[:, 1:-1], tp[:, 2:]], axis=2)

    kb, vb = band(k), band(v)
    s_lat = jnp.einsum('bnqhrd,bnshd->bnhrqs', qb, kb, preferred_element_type=jnp.float32) * scale
    s_lat = jnp.where(band_mask(n)[None, :, None, None], s_lat, -jnp.inf)
    s_ctx = jnp.einsum('bnqhrd,bshd->bnhrqs', qb, kc, preferred_element_type=jnp.float32) * scale
    p = sink_softmax(jnp.concatenate([s_lat, s_ctx], axis=-1), sink).astype(v.dtype)
    nk = 3 * ATT_BLOCK
    o = (jnp.einsum('bnhrqs,bnshd->bnqhrd', p[..., :nk], vb)
         + jnp.einsum('bnhrqs,bshd->bnqhrd', p[..., nk:], vc))
    y = o.reshape(b, n, ATT_HEADS * HEAD_DIM)
    if not ctx_out:
        return y, None
    qc = rms_norm(heads(qc, ATT_HEADS), q_g).reshape(b, lc, ATT_KV_HEADS, ATT_REP, HEAD_DIM)
    s_cc = jnp.einsum('bqhrd,bshd->bhrqs', qc, kc, preferred_element_type=jnp.float32) * scale
    pc = sink_softmax(s_cc, sink).astype(vc.dtype)
    yc = jnp.einsum('bhrqs,bshd->bqhrd', pc, vc).reshape(b, lc, ATT_HEADS * HEAD_DIM)
    return y, yc


def conformer_conv(u, dw_w, dw_b, ln_g, ln_b, pw_w, pw_b):
    a, gate = jnp.split(u, 2, axis=-1)
    h = a * jax.nn.sigmoid(gate)
    h = depthwise_conv(h, dw_w, dw_b, (CONV_K // 2, CONV_K // 2))
    h = jax.nn.silu(layer_norm(h, ln_g, ln_b))
    return h @ pw_w + pw_b


def ssd_scan(x, dt, A, Bm, Cm, h0, with_output):
    b, n = x.shape[:2]
    nc, Q = n // SSD_CHUNK, SSD_CHUNK
    G, R, P, N = SSD_GROUPS, SSD_HEADS_PER_GROUP, SSD_HEAD_DIM, SSD_STATE
    xc = x.astype(jnp.float32).reshape(b, nc, Q, G, R, P)
    dtc = dt.reshape(b, nc, Q, G, R)
    Bc = Bm.astype(jnp.float32).reshape(b, nc, Q, G, N)
    Cc = Cm.astype(jnp.float32).reshape(b, nc, Q, G, N)
    acum = jnp.cumsum(dtc * A.reshape(G, R), axis=2)
    xw = xc * (jnp.exp(acum[:, :, -1:] - acum) * dtc)[..., None]
    states = jnp.einsum('bcjgn,bcjgrp->bcgrpn', Bc, xw)
    chunk_decay = jnp.exp(acum[:, :, -1])

    def step(h, inp):
        s, d = inp
        return h * d[..., None, None] + s, h

    h_last, h_in = lax.scan(step, h0, (jnp.moveaxis(states, 1, 0), jnp.moveaxis(chunk_decay, 1, 0)))
    if not with_output:
        return None, h_last
    h_in = jnp.moveaxis(h_in, 0, 1)
    causal = jnp.tril(jnp.ones((Q, Q), bool))[:, :, None, None]
    seg = acum[:, :, :, None] - acum[:, :, None]
    w = (jnp.einsum('bcign,bcjgn->bcijg', Cc, Bc)[..., None]
         * jnp.exp(jnp.where(causal, seg, -jnp.inf)) * dtc[:, :, None])
    y = (jnp.einsum('bcijgr,bcjgrp->bcigrp', w, xc)
         + jnp.einsum('bcign,bcgrpn->bcigrp', Cc, h_in) * jnp.exp(acum)[..., None])
    return y.reshape(b, n, SSD_HEADS, P), h_last


def ssd_prepare(xbc, dt_raw, conv_w, conv_b, dt_bias):
    b, n, _ = xbc.shape
    xbc = jax.nn.silu(depthwise_conv(xbc, conv_w, conv_b, SSD_CONV_PAD))
    xs, Bm, Cm = jnp.split(xbc, [SSD_INNER, SSD_INNER + SSD_GROUPS * SSD_STATE], axis=-1)
    dt = jax.nn.softplus(dt_raw.astype(jnp.float32).reshape(b, n, 2, SSD_HEADS) + dt_bias.astype(jnp.float32))
    return (xs.reshape(b, n, SSD_HEADS, SSD_HEAD_DIM),
            Bm.reshape(b, n, SSD_GROUPS, SSD_STATE),
            Cm.reshape(b, n, SSD_GROUPS, SSD_STATE), dt)


def ssd_bidir(xs, Bm, Cm, dt, a_log, h0_f, h0_b, with_output):
    A = -jnp.exp(a_log.astype(jnp.float32))
    fl = lambda t: jnp.flip(t, axis=1)
    y_f, h_f = ssd_scan(xs, dt[:, :, 0], A[0], Bm, Cm, h0_f, with_output)
    y_b, h_b = ssd_scan(fl(xs), fl(dt[:, :, 1]), A[1], fl(Bm), fl(Cm), h0_b, with_output)
    y = y_f + fl(y_b) if with_output else None
    return y, h_f, h_b


def ssd_gate_out(y, xs, z, d_skip, norm_g):
    b, n = z.shape[:2]
    y = (y + xs.astype(jnp.float32) * d_skip.astype(jnp.float32)[:, None]).reshape(b, n, SSD_INNER)
    return rms_norm(y * jax.nn.silu(z.astype(jnp.float32)), norm_g).astype(z.dtype)


def pool_mixer(u, pool_w, pool_scale):
    b, n, _ = u.shape
    uf = u.astype(jnp.float32)
    cs = jnp.pad(jnp.cumsum(uf, axis=1), ((0, 0), (1, 0), (0, 0)))
    t = jnp.arange(n)
    outs = []
    for gi, w in enumerate(POOL_SIZES):
        lo = jnp.clip(t - w // 2, 0, n)
        hi = jnp.clip(t + w - w // 2, 0, n)
        sl = slice(gi * POOL_CH, (gi + 1) * POOL_CH)
        mean = (cs[:, hi, sl] - cs[:, lo, sl]) / (hi - lo).astype(jnp.float32)[:, None]
        outs.append(mean - uf[:, :, sl])
    p = jnp.stack(outs, axis=2)
    y = jnp.einsum('bngc,gcd->bngd', p, pool_w.astype(jnp.float32)).reshape(b, n, GROUP_W)
    return (y * pool_scale.astype(jnp.float32)).astype(u.dtype)


def token_mixers(h, hc, w_in, q_g, k_g, sink, cdw_w, cdw_b, cln_g, cln_b, cpw_w, cpw_b,
                 sconv_w, sconv_b, a_log, dt_bias, d_skip, snorm_g, pool_w, pool_scale, ctx_out):
    cuts = np.cumsum(IN_SPLITS)[:-1].tolist()
    q, k, v, glu, z, xbc, dtr, pin = jnp.split(h @ w_in, cuts, axis=-1)
    qc, kc, vc, gluc, zc, xbcc, dtrc, pinc = jnp.split(hc @ w_in, cuts, axis=-1)

    y_att, yc_att = attention_mixer(q, k, v, qc, kc, vc, q_g, k_g, sink, ctx_out)

    zero = jnp.zeros((h.shape[0], SSD_GROUPS, SSD_HEADS_PER_GROUP, SSD_HEAD_DIM, SSD_STATE), jnp.float32)
    xs_c, B_c, C_c, dt_c = ssd_prepare(xbcc, dtrc, sconv_w, sconv_b, dt_bias)
    y_sc, h_f, h_b = ssd_bidir(xs_c, B_c, C_c, dt_c, a_log, zero, zero, ctx_out)
    xs, B_l, C_l, dt_l = ssd_prepare(xbc, dtr, sconv_w, sconv_b, dt_bias)
    y_s, _, _ = ssd_bidir(xs, B_l, C_l, dt_l, a_log, h_f, h_b, True)

    y = jnp.concatenate([y_att,
                         conformer_conv(glu, cdw_w, cdw_b, cln_g, cln_b, cpw_w, cpw_b),
                         ssd_gate_out(y_s, xs, z, d_skip, snorm_g),
                         pool_mixer(pin, pool_w, pool_scale)], axis=-1)
    if not ctx_out:
        return y, None
    yc = jnp.concatenate([yc_att,
                          conformer_conv(gluc, cdw_w, cdw_b, cln_g, cln_b, cpw_w, cpw_b),
                          ssd_gate_out(y_sc, xs_c, zc, d_skip, snorm_g),
                          pool_mixer(pinc, pool_w, pool_scale)], axis=-1)
    return y, yc


def hier_moe(h, w_grp, w_exp, w_gate, w_up, w_down):
    shp = h.shape
    t = h.reshape(-1, shp[-1])
    g_logits = jnp.dot(t, w_grp, preferred_element_type=jnp.float32)
    g_idx = jnp.argmax(g_logits, axis=-1)
    g_w = jnp.take_along_axis(jax.nn.softmax(g_logits, axis=-1), g_idx[:, None], axis=-1)
    e_logits = jnp.dot(t, w_exp, preferred_element_type=jnp.float32).reshape(-1, MOE_GROUPS, MOE_PER_GROUP)
    e_logits = jnp.take_along_axis(e_logits, g_idx[:, None, None], axis=1)[:, 0]
    top_v, top_i = lax.top_k(jax.nn.softmax(e_logits, axis=-1), MOE_TOPK)
    top_v = top_v / jnp.sum(top_v, axis=-1, keepdims=True)
    eid = g_idx[:, None] * MOE_PER_GROUP + top_i
    gates = jnp.sum(jax.nn.one_hot(eid, MOE_EXPERTS, dtype=jnp.float32) * (g_w * top_v)[..., None], axis=1)
    gates = gates.astype(h.dtype)
    out = jnp.zeros_like(t)
    for e in range(MOE_EXPERTS):
        hid = jax.nn.silu(t @ w_gate[e]) * (t @ w_up[e])
        out = out + gates[:, e:e + 1] * (hid @ w_down[e])
    return out.reshape(shp)


def setup_inputs(seed: int = 0) -> dict:
    key = jax.random.key(seed)
    ks = iter(jax.random.split(key, 40))
    nrm = lambda shape, s: jax.random.normal(next(ks), shape, jnp.float32) * s
    L, D = DEPTH, D_MODEL
    dt = jnp.exp(jax.random.uniform(next(ks), (L, 2, SSD_HEADS), jnp.float32)
                 * (math.log(0.1) - math.log(0.001)) + math.log(0.001))
    return {
        'x': nrm((BATCH, SEQ, D), 1.0),
        'c': nrm((BATCH, D), 1.0),
        'ctx': nrm((BATCH, CTX_LEN, D), 1.0),
        'c_ctx': nrm((D,), 1.0),
        'norm1_g': 1.0 + nrm((L, D), 0.02),
        'norm2_g': 1.0 + nrm((L, D), 0.02),
        'w_mod': nrm((L, D, 6 * D), 0.5 * D ** -0.5),
        'b_mod': nrm((L, 6 * D), 0.02),
        'w_in': nrm((L, D, D_IN), D ** -0.5),
        'w_out': nrm((L, D_MIX, D), D_MIX ** -0.5),
        'q_norm_g': 1.0 + nrm((L, HEAD_DIM), 0.02),
        'k_norm_g': 1.0 + nrm((L, HEAD_DIM), 0.02),
        'attn_sink': nrm((L, ATT_HEADS), 0.5),
        'conv_dw_w': nrm((L, CONV_K, CONV_CH), CONV_K ** -0.5),
        'conv_dw_b': nrm((L, CONV_CH), 0.02),
        'conv_ln_g': 1.0 + nrm((L, CONV_CH), 0.02),
        'conv_ln_b': nrm((L, CONV_CH), 0.02),
        'conv_pw_w': nrm((L, CONV_CH, CONV_CH), CONV_CH ** -0.5),
        'conv_pw_b': nrm((L, CONV_CH), 0.02),
        'ssd_conv_w': nrm((L, SSD_CONV, SSD_XBC), SSD_CONV ** -0.5),
        'ssd_conv_b': nrm((L, SSD_XBC), 0.02),
        'ssd_a_log': jnp.log(jax.random.uniform(next(ks), (L, 2, SSD_HEADS), jnp.float32, 1.0, 16.0)),
        'ssd_dt_bias': dt + jnp.log(-jnp.expm1(-dt)),
        'ssd_d': 1.0 + nrm((L, SSD_HEADS), 0.02),
        'ssd_norm_g': 1.0 + nrm((L, SSD_INNER), 0.02),
        'pool_w': nrm((L, len(POOL_SIZES), POOL_CH, POOL_CH), POOL_CH ** -0.5),
        'pool_scale': 1.0 + nrm((L, GROUP_W), 0.1),
        'moe_group_router': nrm((L, D, MOE_GROUPS), D ** -0.5),
        'moe_expert_router': nrm((L, D, MOE_EXPERTS), D ** -0.5),
        'moe_w_gate': nrm((L, MOE_EXPERTS, D, D_EXPERT), D ** -0.5),
        'moe_w_up': nrm((L, MOE_EXPERTS, D, D_EXPERT), D ** -0.5),
        'moe_w_down': nrm((L, MOE_EXPERTS, D_EXPERT, D), D_EXPERT ** -0.5),
    }


def reference(x, c, ctx, c_ctx, norm1_g, norm2_g, w_mod, b_mod, w_in, w_out, q_norm_g, k_norm_g,
              attn_sink, conv_dw_w, conv_dw_b, conv_ln_g, conv_ln_b, conv_pw_w, conv_pw_b,
              ssd_conv_w, ssd_conv_b, ssd_a_log, ssd_dt_bias, ssd_d, ssd_norm_g, pool_w, pool_scale,
              moe_group_router, moe_expert_router, moe_w_gate, moe_w_up, moe_w_down):
    xc = ctx
    for l in range(DEPTH):
        last = l == DEPTH - 1
        mod = jax.nn.silu(c) @ w_mod[l] + b_mod[l]
        mod_c = jax.nn.silu(c_ctx) @ w_mod[l] + b_mod[l]
        sh1, sc1, g1, sh2, sc2, g2 = [m[:, None] for m in jnp.split(mod, 6, axis=-1)]
        sh1c, sc1c, g1c, sh2c, sc2c, g2c = jnp.split(mod_c, 6, axis=-1)
        h = rms_norm(x, norm1_g[l]) * (1.0 + sc1) + sh1
        hc = rms_norm(xc, norm1_g[l]) * (1.0 + sc1c) + sh1c
        y, yc = token_mixers(h, hc, w_in[l], q_norm_g[l], k_norm_g[l], attn_sink[l],
                             conv_dw_w[l], conv_dw_b[l], conv_ln_g[l], conv_ln_b[l], conv_pw_w[l], conv_pw_b[l],
                             ssd_conv_w[l], ssd_conv_b[l], ssd_a_log[l], ssd_dt_bias[l], ssd_d[l], ssd_norm_g[l],
                             pool_w[l], pool_scale[l], not last)
        x = x + g1 * (y @ w_out[l])
        h2 = rms_norm(x, norm2_g[l]) * (1.0 + sc2) + sh2
        x = x + g2 * hier_moe(h2, moe_group_router[l], moe_expert_router[l],
                              moe_w_gate[l], moe_w_up[l], moe_w_down[l])
        if not last:
            xc = xc + g1c * (yc @ w_out[l])
            h2c = rms_norm(xc, norm2_g[l]) * (1.0 + sc2c) + sh2c
            xc = xc + g2c * hier_moe(h2c, moe_group_router[l], moe_expert_router[l],
                                     moe_w_gate[l], moe_w_up[l], moe_w_down[l])
    return x
```

```python
import functools
import math

import jax
import jax.numpy as jnp
from jax import lax
from jax.experimental import pallas as pl
from jax.experimental.pallas import tpu as pltpu

F32 = jnp.float32
BF16 = jnp.bfloat16

EPS = 1e-6
GRID_W = 64
HEAD_DIM = 64
ATT_HEADS = 8
ATT_KV_HEADS = 2
ATT_REP = ATT_HEADS // ATT_KV_HEADS
ATT_BLOCK = 128
ATT_WINDOW = 128
ROPE_THETA = 10000.0
GROUP_W = 512
CONV_K = 31
CONV_PAD = 16
SSD_HEADS = 8
SSD_GROUPS = 2
SSD_STATE = 128
SSD_CHUNK = 128
SSD_CONV = 4
SSD_PAD = 8
SSD_INNER = 512
SSD_XBC = SSD_INNER + 2 * SSD_GROUPS * SSD_STATE
POOL_SIZES = (2, 4, 8, 16)
POOL_CH = 128
POOL_PAD = 8
MOE_GROUPS = 4
MOE_PER_GROUP = 8
MOE_EXPERTS = 32
D_EXPERT = 512

TM = 256
LANE = 128
NEG = -1e30
VMEM_LIMIT = 56 * 1024 * 1024

IN_COLS = (("q", 0, 512), ("k", 512, 128), ("v", 640, 128), ("glu", 768, 1024), ("z", 1792, 512),
           ("xbc", 2304, 1024), ("dt", 3328, 128), ("pin", 3456, 512))
D_IN_PAD = 3968
DT_SRC = 3328


def _cparams(sem):
    return pltpu.CompilerParams(dimension_semantics=sem, vmem_limit_bytes=VMEM_LIMIT)


def _resident(shape):
    nd = len(shape)
    return pl.BlockSpec(shape, lambda *_: (0,) * nd, pipeline_mode=pl.Buffered(1))


def _dot(a, b):
    return jnp.dot(a, b, preferred_element_type=F32)


def _dot_nt(a, b):
    return lax.dot_general(a, b, (((1,), (1,)), ((), ())), preferred_element_type=F32)


def _split2(x):
    hi = x.astype(BF16)
    lo = (x - hi.astype(F32)).astype(BF16)
    return hi, lo


def _split3(x):
    h1 = x.astype(BF16)
    r1 = x - h1.astype(F32)
    h2 = r1.astype(BF16)
    h3 = (r1 - h2.astype(F32)).astype(BF16)
    return h1, h2, h3


def _silu(x):
    return x * jax.nn.sigmoid(x)


def _mod_kernel(c_ref, w_ref, b_ref, o_ref):
    s = _silu(c_ref[...])
    o_ref[0] = _dot(s.astype(BF16), w_ref[0].astype(BF16)) + b_ref[0]


def _modulation(cc, w_mod, b_mod):
    L, D, D6 = w_mod.shape
    R = cc.shape[0]
    tn = 1024
    return pl.pallas_call(
        _mod_kernel,
        grid=(L, D6 // tn),
        in_specs=[pl.BlockSpec((R, D), lambda l, j: (0, 0)),
                  pl.BlockSpec((1, D, tn), lambda l, j: (l, 0, j)),
                  pl.BlockSpec((1, 1, tn), lambda l, j: (l, 0, j))],
        out_specs=pl.BlockSpec((1, R, tn), lambda l, j: (l, 0, j)),
        out_shape=jax.ShapeDtypeStruct((L, R, D6), F32),
        compiler_params=_cparams(("arbitrary", "arbitrary")),
        name="modulation",
    )(cc, w_mod, b_mod.reshape(L, 1, D6))


def _inproj_kernel(x_ref, mod_ref, g_ref, w_ref, *out_refs):
    x = x_ref[...]
    m = mod_ref[0]
    ms = jnp.mean(x * x, axis=-1, keepdims=True)
    h = x * lax.rsqrt(ms + EPS) * g_ref[...]
    h = h * (1.0 + m[1:2]) + m[0:1]
    hb = h.astype(BF16)
    for ref, (_, a, w) in zip(out_refs, IN_COLS):
        ref[...] = _dot(hb, w_ref[:, a:a + w]).astype(ref.dtype)


def _mod_row(i, tps, tiles_ctx):
    return (i // tps) * 2 + jnp.where((i % tps) >= tiles_ctx, 1, 0)


def _in_projection(X2, modl, g1, w_in_p, tps, tiles_ctx):
    T, D = X2.shape
    dts = {"dt": F32}
    out_shape = [jax.ShapeDtypeStruct((T, w), dts.get(n, BF16)) for n, _, w in IN_COLS]
    out_specs = [pl.BlockSpec((TM, w), lambda i: (i, 0)) for _, _, w in IN_COLS]
    return pl.pallas_call(
        _inproj_kernel,
        grid=(T // TM,),
        in_specs=[pl.BlockSpec((TM, D), lambda i: (i, 0)),
                  pl.BlockSpec((1, 6, D), lambda i: (_mod_row(i, tps, tiles_ctx), 0, 0)),
                  _resident((1, D)),
                  _resident((D, D_IN_PAD))],
        out_specs=out_specs,
        out_shape=out_shape,
        compiler_params=_cparams(("parallel",)),
        name="in_projection",
    )(X2, modl, g1, w_in_p)


def _norm_heads(x, g, bd):
    hi, lo = _split2(x * x)
    s = _dot(hi, bd) + _dot(lo, bd)
    return x * lax.rsqrt(s * (1.0 / HEAD_DIM) + EPS) * g


def _rope(x, cos, sin):
    lane = lax.broadcasted_iota(jnp.int32, x.shape, 1)
    sw = jnp.where((lane & 31) < 16, pltpu.roll(x, LANE - 16, 1), pltpu.roll(x, 16, 1))
    return x * cos + sw * sin


def _sink_softmax_pv(parts, sink_col):
    m = sink_col
    for s, _ in parts:
        m = jnp.maximum(m, jnp.max(s, axis=-1, keepdims=True))
    den = jnp.exp(sink_col - m)
    o = None
    for s, v in parts:
        p = jnp.exp(s - m)
        den = den + jnp.sum(p, axis=-1, keepdims=True)
        pv = _dot(p.astype(BF16), v)
        o = pv if o is None else o + pv
    return o / den


def _attn_kernel(sink_ref, q_ref, k_ref, v_ref, cos_ref, sin_ref, qg_ref, kg_ref, bd_ref, o_ref,
                 qh, kh, vh, *, LC, N):
    S = LC + N
    koff = LC + ATT_BLOCK
    bd = bd_ref[...]
    scale = HEAD_DIM ** -0.5
    zpad = jnp.zeros((ATT_BLOCK, HEAD_DIM), BF16)
    for h in range(ATT_KV_HEADS):
        for buf in (kh, vh):
            buf[h, LC:koff, :] = zpad
            buf[h, koff + N:koff + N + ATT_BLOCK, :] = zpad

    rc = TM
    for c0 in range(0, S, rc):
        lat = c0 >= LC
        dst = c0 + ATT_BLOCK if lat else c0
        if lat:
            cos = cos_ref[c0 - LC:c0 - LC + rc, :]
            sin = sin_ref[c0 - LC:c0 - LC + rc, :]
        kn = _norm_heads(k_ref[0, c0:c0 + rc, :].astype(F32), kg_ref[...], bd)
        if lat:
            kn = _rope(kn, cos, sin)
        knb = kn.astype(BF16)
        vv = v_ref[0, c0:c0 + rc, :]
        for h in range(ATT_KV_HEADS):
            kh[h, dst:dst + rc, :] = knb[:, h * HEAD_DIM:(h + 1) * HEAD_DIM]
            vh[h, dst:dst + rc, :] = vv[:, h * HEAD_DIM:(h + 1) * HEAD_DIM]
        for cb in range(ATT_HEADS // 2):
            qn = _norm_heads(q_ref[0, c0:c0 + rc, cb * LANE:(cb + 1) * LANE].astype(F32), qg_ref[...], bd)
            if lat:
                qn = _rope(qn, cos, sin)
            qnb = (qn * scale).astype(BF16)
            qh[2 * cb, c0:c0 + rc, :] = qnb[:, :HEAD_DIM]
            qh[2 * cb + 1, c0:c0 + rc, :] = qnb[:, HEAD_DIM:]

    def sink_column(h, rows):
        grp = lax.broadcasted_iota(jnp.int32, (ATT_REP * rows, 1), 0) // rows
        col = jnp.full((ATT_REP * rows, 1), sink_ref[ATT_REP * h], F32)
        for r in range(1, ATT_REP):
            col = jnp.where(grp == r, sink_ref[ATT_REP * h + r], col)
        return col

    def store_heads(o, h, row0, rows):
        for pr in range(ATT_REP // 2):
            pair = jnp.concatenate([o[(2 * pr) * rows:(2 * pr + 1) * rows],
                                    o[(2 * pr + 1) * rows:(2 * pr + 2) * rows]], axis=1)
            cb = (ATT_REP // 2) * h + pr
            o_ref[0, pl.ds(row0, rows), cb * LANE:(cb + 1) * LANE] = pair.astype(o_ref.dtype)

    for h in range(ATT_KV_HEADS):
        q4 = jnp.concatenate([qh[ATT_REP * h + r, 0:LC, :] for r in range(ATT_REP)], axis=0)
        kc = kh[h, 0:LC, :]
        vc = vh[h, 0:LC, :]
        o = _sink_softmax_pv([(_dot_nt(q4, kc), vc)], sink_column(h, LC))
        store_heads(o, h, 0, LC)

    nk = 3 * ATT_BLOCK
    qi = lax.broadcasted_iota(jnp.int32, (ATT_REP * ATT_BLOCK, nk), 0) & (ATT_BLOCK - 1)
    kj = lax.broadcasted_iota(jnp.int32, (ATT_REP * ATT_BLOCK, nk), 1)
    rel = kj - ATT_BLOCK - qi
    in_window = (rel <= ATT_WINDOW) & (rel >= -ATT_WINDOW)

    def block(j, carry):
        r0 = pl.multiple_of(j * ATT_BLOCK, ATT_BLOCK)
        kpos = kj + (j - 1) * ATT_BLOCK
        ok = in_window & (kpos >= 0) & (kpos < N)
        for h in range(ATT_KV_HEADS):
            q4 = jnp.concatenate([qh[ATT_REP * h + r, pl.ds(LC + r0, ATT_BLOCK), :] for r in range(ATT_REP)],
                                 axis=0)
            kb = kh[h, pl.ds(LC + r0, nk), :]
            vb = vh[h, pl.ds(LC + r0, nk), :]
            s_lat = jnp.where(ok, _dot_nt(q4, kb), NEG)
            s_ctx = _dot_nt(q4, kh[h, 0:LC, :])
            o = _sink_softmax_pv([(s_lat, vb), (s_ctx, vh[h, 0:LC, :])], sink_column(h, ATT_BLOCK))
            store_heads(o, h, LC + r0, ATT_BLOCK)
        return carry

    lax.fori_loop(0, N // ATT_BLOCK, block, 0)


def _attention(q, k, v, cos, sin, qg, kg, bd, sink, LC, N):
    B, S, _ = q.shape
    kern = functools.partial(_attn_kernel, LC=LC, N=N)
    grid_spec = pltpu.PrefetchScalarGridSpec(
        num_scalar_prefetch=1,
        grid=(B,),
        in_specs=[pl.BlockSpec((1, S, 512), lambda b, s: (b, 0, 0)),
                  pl.BlockSpec((1, S, 128), lambda b, s: (b, 0, 0)),
                  pl.BlockSpec((1, S, 128), lambda b, s: (b, 0, 0)),
                  pl.BlockSpec((N, LANE), lambda b, s: (0, 0)),
                  pl.BlockSpec((N, LANE), lambda b, s: (0, 0)),
                  pl.BlockSpec((1, LANE), lambda b, s: (0, 0)),
                  pl.BlockSpec((1, LANE), lambda b, s: (0, 0)),
                  pl.BlockSpec((LANE, LANE), lambda b, s: (0, 0))],
        out_specs=pl.BlockSpec((1, S, 512), lambda b, s: (b, 0, 0)),
        scratch_shapes=[pltpu.VMEM((ATT_HEADS, S, HEAD_DIM), BF16),
                        pltpu.VMEM((ATT_KV_HEADS, S + 2 * ATT_BLOCK, HEAD_DIM), BF16),
                        pltpu.VMEM((ATT_KV_HEADS, S + 2 * ATT_BLOCK, HEAD_DIM), BF16)])
    return pl.pallas_call(
        kern, grid_spec=grid_spec,
        out_shape=jax.ShapeDtypeStruct((B, S, 512), BF16),
        compiler_params=_cparams(("parallel",)),
        name="attention",
    )(sink, q, k, v, cos, sin, qg, kg, bd)


def _conv_kernel(glu_ref, dww_ref, dwb_ref, lng_ref, lnb_ref, pww_ref, pwb_ref, o_ref, hp, *, LC, N):
    C = GROUP_W
    half = CONV_K // 2
    rows = 128

    def stream(s0, ln):
        hp[0:CONV_PAD, :] = jnp.zeros((CONV_PAD, C), F32)
        hp[CONV_PAD + ln:2 * CONV_PAD + ln, :] = jnp.zeros((CONV_PAD, C), F32)

        def fill(i, carry):
            r0 = pl.multiple_of(i * TM, TM)
            g = glu_ref[0, pl.ds(s0 + r0, TM), :]
            a = g[:, :C].astype(F32)
            gate = g[:, C:].astype(F32)
            hp[pl.ds(CONV_PAD + r0, TM), :] = a * jax.nn.sigmoid(gate)
            return carry

        lax.fori_loop(0, ln // TM, fill, 0)

        def chunk(i, carry):
            r0 = pl.multiple_of(i * rows, rows)
            parts = []
            for cb in range(C // LANE):
                win = hp[pl.ds(r0, rows + 2 * CONV_PAD), cb * LANE:(cb + 1) * LANE]
                acc = jnp.zeros((rows, LANE), F32)
                for t in range(CONV_K):
                    off = CONV_PAD - half + t
                    w = dww_ref[t:t + 1, cb * LANE:(cb + 1) * LANE]
                    acc = acc + w * win[off:off + rows]
                parts.append(acc)
            y = jnp.concatenate(parts, axis=1) + dwb_ref[...]
            mu = jnp.mean(y, axis=-1, keepdims=True)
            yc = y - mu
            var = jnp.mean(yc * yc, axis=-1, keepdims=True)
            z = yc * lax.rsqrt(var + EPS) * lng_ref[...] + lnb_ref[...]
            z = _silu(z)
            out = _dot(z.astype(BF16), pww_ref[...]) + pwb_ref[...]
            o_ref[0, pl.ds(s0 + r0, rows), :] = out.astype(o_ref.dtype)
            return carry

        lax.fori_loop(0, ln // rows, chunk, 0)

    stream(0, LC)
    stream(LC, N)


def _conformer(glu, dww, dwb, lng, lnb, pww, pwb, LC, N):
    B, S, _ = glu.shape
    C = GROUP_W
    kern = functools.partial(_conv_kernel, LC=LC, N=N)
    vec = pl.BlockSpec((1, C), lambda b: (0, 0))
    return pl.pallas_call(
        kern, grid=(B,),
        in_specs=[pl.BlockSpec((1, S, 2 * C), lambda b: (b, 0, 0)),
                  pl.BlockSpec((32, C), lambda b: (0, 0)),
                  vec, vec, vec,
                  pl.BlockSpec((C, C), lambda b: (0, 0)),
                  vec],
        out_specs=pl.BlockSpec((1, S, C), lambda b: (b, 0, 0)),
        out_shape=jax.ShapeDtypeStruct((B, S, C), BF16),
        scratch_shapes=[pltpu.VMEM((max(LC, N) + 2 * CONV_PAD, C), F32)],
        compiler_params=_cparams(("parallel",)),
        name="conformer_conv",
    )(glu, dww, dwb, lng, lnb, pww, pwb)


def _ssd_kernel(xbc_ref, dt_ref, z_ref, cw_ref, cb_ref, arow_ref, dtb_ref, dsk_ref, ng_ref, o_ref,
                xp, xs, bm, cm, dts, yacc, state, *, LC, N):
    S = LC + N
    Q = SSD_CHUNK
    nc = S // Q
    nc_ctx = LC // Q
    ii = lax.broadcasted_iota(jnp.int32, (Q, Q), 0)
    jj = lax.broadcasted_iota(jnp.int32, (Q, Q), 1)
    lower = ii >= jj
    tri = (jnp.where(lower, 1.0, 0.0).astype(BF16), jnp.where(jj >= ii, 1.0, 0.0).astype(BF16))
    causal = (lower, jj >= ii)
    first_half = lax.broadcasted_iota(jnp.int32, (Q, LANE), 1) < SSD_STATE // 2

    def stream(s0, ln):
        xp[0:SSD_PAD, :] = jnp.zeros((SSD_PAD, SSD_XBC), F32)
        xp[SSD_PAD + ln:2 * SSD_PAD + ln, :] = jnp.zeros((SSD_PAD, SSD_XBC), F32)

        def fill(i, carry):
            r0 = pl.multiple_of(i * Q, Q)
            xp[pl.ds(SSD_PAD + r0, Q), :] = xbc_ref[0, pl.ds(s0 + r0, Q), :].astype(F32)
            return carry

        lax.fori_loop(0, ln // Q, fill, 0)

        def conv(i, carry):
            r0 = pl.multiple_of(i * Q, Q)
            for cb in range(SSD_XBC // LANE):
                cs = slice(cb * LANE, (cb + 1) * LANE)
                win = xp[pl.ds(r0, Q + 2 * SSD_PAD), cs]
                acc = jnp.zeros((Q, LANE), F32)
                for t in range(SSD_CONV):
                    acc = acc + cw_ref[t:t + 1, cs] * win[SSD_PAD - 2 + t:SSD_PAD - 2 + t + Q]
                y = _silu(acc + cb_ref[:, cs])
                if cb < SSD_INNER // LANE:
                    xs[pl.ds(s0 + r0, Q), cs] = y
                elif cb < (SSD_INNER + SSD_GROUPS * SSD_STATE) // LANE:
                    c2 = cb - SSD_INNER // LANE
                    bm[pl.ds(s0 + r0, Q), c2 * LANE:(c2 + 1) * LANE] = y
                else:
                    c2 = cb - (SSD_INNER + SSD_GROUPS * SSD_STATE) // LANE
                    cm[pl.ds(s0 + r0, Q), c2 * LANE:(c2 + 1) * LANE] = y
            return carry

        lax.fori_loop(0, ln // Q, conv, 0)

    stream(0, LC)
    stream(LC, N)

    def softplus_rows(i, carry):
        r0 = pl.multiple_of(i * Q, Q)
        v = dt_ref[0, pl.ds(r0, Q), :] + dtb_ref[...]
        dts[pl.ds(r0, Q), :] = jnp.maximum(v, 0.0) + jnp.log(1.0 + jnp.exp(-jnp.abs(v)))
        return carry

    lax.fori_loop(0, nc, softplus_rows, 0)

    def chunk_step(c, d):
        r0 = pl.multiple_of(c * Q, Q)
        dtc = dts[pl.ds(r0, Q), :]
        a1, a2, a3 = _split3(dtc * arow_ref[...])
        cum = _dot(tri[d], a1) + _dot(tri[d], a2) + _dot(tri[d], a3)
        cum_t = cum.T
        dt_t = dtc.T
        tot = cum[Q - 1:Q, :] if d == 0 else cum[0:1, :]
        e_cum = jnp.exp(cum)
        e_tot = jnp.exp(tot)
        w_state = jnp.exp(tot - cum) * dtc
        for g in range(SSD_GROUPS):
            bg = bm[pl.ds(r0, Q), g * SSD_STATE:(g + 1) * SSD_STATE]
            cg = cm[pl.ds(r0, Q), g * SSD_STATE:(g + 1) * SSD_STATE].astype(BF16)
            bg_t = bg.T.astype(BF16)
            cb_mat = _dot(cg, bg_t)
            for pr in range(SSD_HEADS // SSD_GROUPS // 2):
                pair = g * 2 + pr
                lanes = slice(pair * LANE, (pair + 1) * LANE)
                xpair = xs[pl.ds(r0, Q), lanes]
                y = None
                for sub in range(2):
                    hd = d * SSD_HEADS + pair * 2 + sub
                    seg = cum[:, hd:hd + 1] - cum_t[hd:hd + 1, :]
                    dec = jnp.where(causal[d], jnp.exp(jnp.minimum(seg, 0.0)), 0.0)
                    w = (cb_mat * dec * dt_t[hd:hd + 1, :]).astype(BF16)
                    xm = jnp.where(first_half if sub == 0 else ~first_half, xpair, 0.0).astype(BF16)
                    t = _dot(w, xm)
                    y = t if y is None else y + t
                h0 = d * SSD_HEADS + pair * 2
                sel = lambda m: jnp.where(first_half, m[:, h0:h0 + 1], m[:, h0 + 1:h0 + 2])
                st = state[d * (SSD_HEADS // 2) + pair]
                y = y + _dot(cg, st.astype(BF16)) * sel(e_cum)
                xw = (xpair * sel(w_state)).astype(BF16)
                state[d * (SSD_HEADS // 2) + pair] = st * sel(e_tot) + _dot(bg_t, xw)
                if d == 0:
                    yacc[pl.ds(r0, Q), lanes] = y
                else:
                    yacc[pl.ds(r0, Q), lanes] = yacc[pl.ds(r0, Q), lanes] + y

    state[...] = jnp.zeros(state.shape, F32)

    def fwd(c, carry):
        chunk_step(c, 0)
        return carry

    lax.fori_loop(0, nc, fwd, 0)

    def bwd_ctx(i, carry):
        chunk_step(nc_ctx - 1 - i, 1)
        return carry

    lax.fori_loop(0, nc_ctx, bwd_ctx, 0)

    def bwd_lat(i, carry):
        chunk_step(nc - 1 - i, 1)
        return carry

    lax.fori_loop(0, nc - nc_ctx, bwd_lat, 0)

    def gate_out(i, carry):
        r0 = pl.multiple_of(i * Q, Q)
        y = yacc[pl.ds(r0, Q), :] + xs[pl.ds(r0, Q), :] * dsk_ref[...]
        y = y * _silu(z_ref[0, pl.ds(r0, Q), :].astype(F32))
        ms = jnp.mean(y * y, axis=-1, keepdims=True)
        o_ref[0, pl.ds(r0, Q), :] = (y * lax.rsqrt(ms + EPS) * ng_ref[...]).astype(o_ref.dtype)
        return carry

    lax.fori_loop(0, nc, gate_out, 0)


def _ssd(xbc, dt, z, cw, cb, arow, dtb, dsk, ng, LC, N):
    B, S, _ = xbc.shape
    kern = functools.partial(_ssd_kernel, LC=LC, N=N)
    row = lambda w: pl.BlockSpec((1, w), lambda b: (0, 0))
    return pl.pallas_call(
        kern, grid=(B,),
        in_specs=[pl.BlockSpec((1, S, SSD_XBC), lambda b: (b, 0, 0)),
                  pl.BlockSpec((1, S, LANE), lambda b: (b, 0, 0)),
                  pl.BlockSpec((1, S, SSD_INNER), lambda b: (b, 0, 0)),
                  pl.BlockSpec((SSD_CONV, SSD_XBC), lambda b: (0, 0)),
                  row(SSD_XBC), row(LANE), row(LANE), row(SSD_INNER), row(SSD_INNER)],
        out_specs=pl.BlockSpec((1, S, SSD_INNER), lambda b: (b, 0, 0)),
        out_shape=jax.ShapeDtypeStruct((B, S, SSD_INNER), BF16),
        scratch_shapes=[pltpu.VMEM((max(LC, N) + 2 * SSD_PAD, SSD_XBC), F32),
                        pltpu.VMEM((S, SSD_INNER), F32),
                        pltpu.VMEM((S, SSD_GROUPS * SSD_STATE), F32),
                        pltpu.VMEM((S, SSD_GROUPS * SSD_STATE), F32),
                        pltpu.VMEM((S, LANE), F32),
                        pltpu.VMEM((S, SSD_INNER), F32),
                        pltpu.VMEM((SSD_HEADS, SSD_STATE, LANE), F32)],
        compiler_params=_cparams(("parallel",)),
        name="ssd",
    )(xbc, dt, z, cw, cb, arow, dtb, dsk, ng)


def _pool_kernel(pin_ref, pw_ref, ps_ref, o_ref, up, *, LC, N):
    C = GROUP_W
    rows = TM

    def stream(s0, ln):
        up[0:POOL_PAD, :] = jnp.zeros((POOL_PAD, C), F32)
        up[POOL_PAD + ln:2 * POOL_PAD + ln, :] = jnp.zeros((POOL_PAD, C), F32)

        def fill(i, carry):
            r0 = pl.multiple_of(i * rows, rows)
            up[pl.ds(POOL_PAD + r0, rows), :] = pin_ref[0, pl.ds(s0 + r0, rows), :].astype(F32)
            return carry

        lax.fori_loop(0, ln // rows, fill, 0)

        def chunk(i, carry):
            r0 = pl.multiple_of(i * rows, rows)
            t = r0 + lax.broadcasted_iota(jnp.int32, (rows, 1), 0)
            outs = []
            for gi, w in enumerate(POOL_SIZES):
                cs = slice(gi * POOL_CH, (gi + 1) * POOL_CH)
                win = up[pl.ds(r0, rows + 2 * POOL_PAD), cs]
                acc = jnp.zeros((rows, POOL_CH), F32)
                for d in range(-(w // 2), w - w // 2):
                    acc = acc + win[POOL_PAD + d:POOL_PAD + d + rows]
                cnt = jnp.minimum(t + (w - w // 2), ln) - jnp.maximum(t - w // 2, 0)
                p = acc / cnt.astype(F32) - win[POOL_PAD:POOL_PAD + rows]
                outs.append(_dot(p.astype(BF16), pw_ref[gi]))
            y = jnp.concatenate(outs, axis=1) * ps_ref[...]
            o_ref[0, pl.ds(s0 + r0, rows), :] = y.astype(o_ref.dtype)
            return carry

        lax.fori_loop(0, ln // rows, chunk, 0)

    stream(0, LC)
    stream(LC, N)


def _pool(pin, pw, ps, LC, N):
    B, S, C = pin.shape
    kern = functools.partial(_pool_kernel, LC=LC, N=N)
    return pl.pallas_call(
        kern, grid=(B,),
        in_specs=[pl.BlockSpec((1, S, C), lambda b: (b, 0, 0)),
                  pl.BlockSpec((len(POOL_SIZES), POOL_CH, POOL_CH), lambda b: (0, 0, 0)),
                  pl.BlockSpec((1, C), lambda b: (0, 0))],
        out_specs=pl.BlockSpec((1, S, C), lambda b: (b, 0, 0)),
        out_shape=jax.ShapeDtypeStruct((B, S, C), BF16),
        scratch_shapes=[pltpu.VMEM((max(LC, N) + 2 * POOL_PAD, C), F32)],
        compiler_params=_cparams(("parallel",)),
        name="pool_mixer",
    )(pin, pw, ps)


def _outproj_kernel(ya_ref, yb_ref, yc_ref, yd_ref, x_ref, mod_ref, g_ref, w_ref, wr1_ref, wr2_ref,
                    xo_ref, h2_ref, lg_ref):
    acc = None
    for i, ref in enumerate((ya_ref, yb_ref, yc_ref, yd_ref)):
        t = _dot(ref[...], w_ref[i * GROUP_W:(i + 1) * GROUP_W, :])
        acc = t if acc is None else acc + t
    m = mod_ref[0]
    x = x_ref[...] + m[2:3] * acc
    xo_ref[...] = x
    ms = jnp.mean(x * x, axis=-1, keepdims=True)
    h2 = x * lax.rsqrt(ms + EPS) * g_ref[...]
    h2 = h2 * (1.0 + m[4:5]) + m[3:4]
    h2_ref[...] = h2.astype(BF16)
    h_hi, h_lo = _split2(h2)
    lg_ref[...] = _dot(h_hi, wr1_ref[...]) + (_dot(h_hi, wr2_ref[...]) + _dot(h_lo, wr1_ref[...]))


def _out_projection(ys, X2, modl, g2, w_out, wr1, wr2, tps, tiles_ctx):
    T, D = X2.shape
    ytile = pl.BlockSpec((TM, GROUP_W), lambda i: (i, 0))
    return pl.pallas_call(
        _outproj_kernel,
        grid=(T // TM,),
        in_specs=[ytile, ytile, ytile, ytile,
                  pl.BlockSpec((TM, D), lambda i: (i, 0)),
                  pl.BlockSpec((1, 6, D), lambda i: (_mod_row(i, tps, tiles_ctx), 0, 0)),
                  _resident((1, D)),
                  _resident((4 * GROUP_W, D)),
                  _resident((D, LANE)),
                  _resident((D, LANE))],
        out_specs=[pl.BlockSpec((TM, D), lambda i: (i, 0)),
                   pl.BlockSpec((TM, D), lambda i: (i, 0)),
                   pl.BlockSpec((TM, LANE), lambda i: (i, 0))],
        out_shape=[jax.ShapeDtypeStruct((T, D), F32),
                   jax.ShapeDtypeStruct((T, D), BF16),
                   jax.ShapeDtypeStruct((T, LANE), F32)],
        input_output_aliases={4: 0},
        compiler_params=_cparams(("parallel",)),
        name="out_projection",
    )(*ys, X2, modl, g2, w_out, wr1, wr2)


def _route(logits, n_tiles):
    T = logits.shape[0]
    gl = logits[:, :MOE_GROUPS]
    el = logits[:, MOE_GROUPS:MOE_GROUPS + MOE_EXPERTS].reshape(T, MOE_GROUPS, MOE_PER_GROUP)
    g_idx = jnp.argmax(gl, axis=-1)
    g_w = jnp.take_along_axis(jax.nn.softmax(gl, axis=-1), g_idx[:, None], axis=-1)
    es = jnp.take_along_axis(el, g_idx[:, None, None], axis=1)[:, 0]
    top_v, top_i = lax.top_k(jax.nn.softmax(es, axis=-1), 2)
    top_v = top_v / jnp.sum(top_v, axis=-1, keepdims=True)
    eid = (g_idx[:, None] * MOE_PER_GROUP + top_i).astype(jnp.int32)
    gates = g_w * top_v

    e_flat = eid.reshape(-1)
    onehot = (e_flat[:, None] == jnp.arange(MOE_EXPERTS, dtype=jnp.int32)[None, :]).astype(jnp.int32)
    csum = jnp.cumsum(onehot, axis=0)
    rank = jnp.take_along_axis(csum, e_flat[:, None], axis=1)[:, 0] - 1
    counts = csum[-1]
    ntile = (counts + TM - 1) // TM
    tile_end = jnp.cumsum(ntile)
    tile_start = tile_end - ntile
    dest = (tile_start[e_flat] * TM + rank).astype(jnp.int32)
    total = tile_end[-1]
    tiles = jnp.arange(n_tiles, dtype=jnp.int32)
    active = tiles < total
    t_exp = jnp.searchsorted(tile_end, jnp.minimum(tiles, total - 1), side="right").astype(jnp.int32)
    t_first = (active & (tiles == tile_start[t_exp])).astype(jnp.int32)
    src = jnp.zeros((n_tiles * TM,), jnp.int32).at[dest].set(jnp.arange(2 * T, dtype=jnp.int32) // 2)
    return dest.reshape(T, 2), gates, src, t_exp, t_first, active.astype(jnp.int32)


def _moe_kernel(te_ref, tf_ref, tv_ref, x_ref, wg_ref, wu_ref, wd_ref, o_ref, wg_b, wu_b, wd_b):
    i = pl.program_id(0)

    @pl.when(tf_ref[i] == 1)
    def _():
        wg_b[...] = wg_ref[0].astype(BF16)
        wu_b[...] = wu_ref[0].astype(BF16)
        wd_b[...] = wd_ref[0].astype(BF16)

    @pl.when(tv_ref[i] == 1)
    def _():
        x = x_ref[...]
        g = _dot(x, wg_b[...])
        u = _dot(x, wu_b[...])
        hid = (_silu(g) * u).astype(BF16)
        o_ref[...] = _dot(hid, wd_b[...]).astype(o_ref.dtype)

    @pl.when(tv_ref[i] == 0)
    def _():
        o_ref[...] = jnp.zeros(o_ref.shape, o_ref.dtype)


def _moe_experts(xs, t_exp, t_first, t_active, w_gate, w_up, w_down):
    R, D = xs.shape
    n_tiles = R // TM
    wmap = lambda i, te, tf, tv: (te[i], 0, 0)
    grid_spec = pltpu.PrefetchScalarGridSpec(
        num_scalar_prefetch=3,
        grid=(n_tiles,),
        in_specs=[pl.BlockSpec((TM, D), lambda i, te, tf, tv: (i, 0)),
                  pl.BlockSpec((1, D, D_EXPERT), wmap),
                  pl.BlockSpec((1, D, D_EXPERT), wmap),
                  pl.BlockSpec((1, D_EXPERT, D), wmap)],
        out_specs=pl.BlockSpec((TM, D), lambda i, te, tf, tv: (i, 0)),
        scratch_shapes=[pltpu.VMEM((D, D_EXPERT), BF16),
                        pltpu.VMEM((D, D_EXPERT), BF16),
                        pltpu.VMEM((D_EXPERT, D), BF16)])
    return pl.pallas_call(
        _moe_kernel, grid_spec=grid_spec,
        out_shape=jax.ShapeDtypeStruct((R, D), BF16),
        compiler_params=_cparams(("arbitrary",)),
        name="moe_experts",
    )(t_exp, t_first, t_active, xs, w_gate, w_up, w_down)


def _combine_kernel(x_ref, y0_ref, y1_ref, gw_ref, mod_ref, o_ref):
    gw = gw_ref[...]
    moe = gw[:, 0:1] * y0_ref[...].astype(F32) + gw[:, 1:2] * y1_ref[...].astype(F32)
    o_ref[...] = x_ref[...] + mod_ref[0][5:6] * moe


def _moe_combine(X2, y0, y1, gates, modl, tps, tiles_ctx):
    T, D = X2.shape
    tile = pl.BlockSpec((TM, D), lambda i: (i, 0))
    return pl.pallas_call(
        _combine_kernel,
        grid=(T // TM,),
        in_specs=[tile, tile, tile,
                  pl.BlockSpec((TM, 2), lambda i: (i, 0)),
                  pl.BlockSpec((1, 6, D), lambda i: (_mod_row(i, tps, tiles_ctx), 0, 0))],
        out_specs=tile,
        out_shape=jax.ShapeDtypeStruct((T, D), F32),
        input_output_aliases={0: 0},
        compiler_params=_cparams(("parallel",)),
        name="moe_combine",
    )(X2, y0, y1, gates, modl)


def _rope_tables(n):
    rows = n // GRID_W
    row = jnp.repeat(jnp.arange(rows, dtype=F32), GRID_W)
    col = jnp.tile(jnp.arange(GRID_W, dtype=F32), rows)
    half = HEAD_DIM // 2
    inv_freq = 1.0 / (ROPE_THETA ** (jnp.arange(0, half, 2, dtype=F32) / half))
    ar = row[:, None] * inv_freq
    ac = col[:, None] * inv_freq
    cos = jnp.concatenate([jnp.cos(ar), jnp.cos(ar), jnp.cos(ac), jnp.cos(ac)], axis=-1)
    sin = jnp.concatenate([-jnp.sin(ar), jnp.sin(ar), -jnp.sin(ac), jnp.sin(ac)], axis=-1)
    return jnp.tile(cos, (1, 2)), jnp.tile(sin, (1, 2))


def kernel(x, c, ctx, c_ctx, norm1_g, norm2_g, w_mod, b_mod, w_in, w_out, q_norm_g, k_norm_g, attn_sink,
           conv_dw_w, conv_dw_b, conv_ln_g, conv_ln_b, conv_pw_w, conv_pw_b, ssd_conv_w, ssd_conv_b, ssd_a_log,
           ssd_dt_bias, ssd_d, ssd_norm_g, pool_w, pool_scale, moe_group_router, moe_expert_router, moe_w_gate,
           moe_w_up, moe_w_down):
    B, N, D = x.shape
    LC = ctx.shape[1]
    S = LC + N
    L = w_mod.shape[0]
    assert LC % TM == 0 and N % TM == 0 and N % GRID_W == 0
    tps = S // TM
    tiles_ctx = LC // TM
    T = B * S
    n_tiles = (2 * T) // TM + MOE_EXPERTS

    rows = -(-(B + 1) // 8) * 8
    cc = jnp.concatenate([c, c_ctx[None, :], jnp.zeros((rows - B - 1, D), F32)], axis=0)
    mod = _modulation(cc, w_mod, b_mod).reshape(L, rows, 6, D)

    w_in_p = jnp.concatenate([w_in[:, :, :DT_SRC + 16], jnp.zeros((L, D, LANE - 16), F32), w_in[:, :, DT_SRC + 16:]],
                             axis=-1).astype(BF16)
    w_out_b = w_out.astype(BF16)
    w_router = jnp.concatenate([moe_group_router, moe_expert_router,
                                jnp.zeros((L, D, LANE - MOE_GROUPS - MOE_EXPERTS), F32)], axis=-1)
    wr1 = w_router.astype(BF16)
    wr2 = (w_router - wr1.astype(F32)).astype(BF16)

    cos, sin = _rope_tables(N)
    bd = (jnp.arange(LANE)[:, None] // HEAD_DIM == jnp.arange(LANE)[None, :] // HEAD_DIM).astype(BF16)
    pad_lane = lambda v: jnp.concatenate([v.reshape(-1), jnp.zeros((LANE - v.size,), F32)]).reshape(1, LANE)

    X = jnp.concatenate([ctx, x], axis=1).reshape(T, D)
    for l in range(L):
        modl = jnp.stack([jnp.broadcast_to(mod[l, B], (B, 6, D)), mod[l, :B]], axis=1).reshape(2 * B, 6, D)
        q, k, v, glu, z, xbc, dt, pin = _in_projection(X, modl, norm1_g[l][None], w_in_p[l], tps, tiles_ctx)
        r3 = lambda a: a.reshape(B, S, a.shape[-1])
        y_att = _attention(r3(q), r3(k), r3(v), cos, sin, jnp.tile(q_norm_g[l], 2)[None], jnp.tile(k_norm_g[l], 2)[None],
                           bd, attn_sink[l], LC, N)
        dww = jnp.concatenate([conv_dw_w[l], jnp.zeros((32 - CONV_K, GROUP_W), F32)], axis=0)
        y_conv = _conformer(r3(glu), dww, conv_dw_b[l][None], conv_ln_g[l][None], conv_ln_b[l][None],
                            conv_pw_w[l].astype(BF16), conv_pw_b[l][None], LC, N)
        y_ssd = _ssd(r3(xbc), r3(dt), r3(z), ssd_conv_w[l], ssd_conv_b[l][None], pad_lane(-jnp.exp(ssd_a_log[l])),
                     pad_lane(ssd_dt_bias[l]), jnp.repeat(ssd_d[l], HEAD_DIM)[None], ssd_norm_g[l][None], LC, N)
        y_pool = _pool(r3(pin), pool_w[l].astype(BF16), pool_scale[l][None], LC, N)
        ys = [a.reshape(T, GROUP_W) for a in (y_att, y_conv, y_ssd, y_pool)]
        X, h2, logits = _out_projection(ys, X, modl, norm2_g[l][None], w_out_b[l], wr1[l], wr2[l], tps, tiles_ctx)

        dest, gates, src, t_exp, t_first, t_active = _route(logits, n_tiles)
        xs = jnp.take(h2, src, axis=0)
        ye = _moe_experts(xs, t_exp, t_first, t_active, moe_w_gate[l], moe_w_up[l], moe_w_down[l])
        y0 = jnp.take(ye, dest[:, 0], axis=0)
        y1 = jnp.take(ye, dest[:, 1], axis=0)
        X = _moe_combine(X, y0, y1, gates, modl, tps, tiles_ctx)
    return X.reshape(B, S, D)[:, LC:]
```

```python
import functools
import math

import jax
import jax.numpy as jnp
from jax import lax
from jax.experimental import pallas as pl
from jax.experimental.pallas import tpu as pltpu

F32 = jnp.float32
BF16 = jnp.bfloat16

EPS = 1e-6
GRID_W = 64
HEAD_DIM = 64
ATT_HEADS = 8
ATT_KV_HEADS = 2
ATT_REP = ATT_HEADS // ATT_KV_HEADS
ATT_BLOCK = 128
ATT_WINDOW = 128
ROPE_THETA = 10000.0
GROUP_W = 512
CONV_K = 31
CONV_PAD = 16
SSD_HEADS = 8
SSD_GROUPS = 2
SSD_STATE = 128
SSD_CHUNK = 128
SSD_CONV = 4
SSD_PAD = 8
SSD_INNER = 512
SSD_XBC = SSD_INNER + 2 * SSD_GROUPS * SSD_STATE
POOL_SIZES = (2, 4, 8, 16)
POOL_CH = 128
POOL_PAD = 8
MOE_GROUPS = 4
MOE_PER_GROUP = 8
MOE_EXPERTS = 32
D_EXPERT = 512

TM = 256
LANE = 128
NEG = -1e30
VMEM_LIMIT = 56 * 1024 * 1024

IN_COLS = (("q", 0, 512), ("k", 512, 128), ("v", 640, 128), ("glu", 768, 1024), ("z", 1792, 512),
           ("xbc", 2304, 1024), ("dt", 3328, 128), ("pin", 3456, 512))
D_IN_PAD = 3968
DT_SRC = 3328


def _cparams(sem):
    return pltpu.CompilerParams(dimension_semantics=sem, vmem_limit_bytes=VMEM_LIMIT)


def _resident(shape, layer=None):
    nd = len(shape)
    if layer is None:
        return pl.BlockSpec(shape, lambda *_: (0,) * nd, pipeline_mode=pl.Buffered(1))
    return pl.BlockSpec((1,) + tuple(shape), lambda *_: (layer,) + (0,) * nd, pipeline_mode=pl.Buffered(1))


def _dot(a, b):
    return jnp.dot(a, b, preferred_element_type=F32)


def _dot_nt(a, b):
    return lax.dot_general(a, b, (((1,), (1,)), ((), ())), preferred_element_type=F32)


def _split2(x):
    hi = x.astype(BF16)
    lo = (x - hi.astype(F32)).astype(BF16)
    return hi, lo


def _split3(x):
    h1 = x.astype(BF16)
    r1 = x - h1.astype(F32)
    h2 = r1.astype(BF16)
    h3 = (r1 - h2.astype(F32)).astype(BF16)
    return h1, h2, h3


def _silu(x):
    return x * jax.nn.sigmoid(x)


def _mod_kernel(c_ref, w_ref, b_ref, o_ref):
    s = _silu(c_ref[...])
    o_ref[0] = _dot(s.astype(BF16), w_ref[0].astype(BF16)) + b_ref[0]


def _modulation(cc, w_mod, b_mod):
    L, D, D6 = w_mod.shape
    R = cc.shape[0]
    tn = 1024
    return pl.pallas_call(
        _mod_kernel,
        grid=(L, D6 // tn),
        in_specs=[pl.BlockSpec((R, D), lambda l, j: (0, 0)),
                  pl.BlockSpec((1, D, tn), lambda l, j: (l, 0, j)),
                  pl.BlockSpec((1, 1, tn), lambda l, j: (l, 0, j))],
        out_specs=pl.BlockSpec((1, R, tn), lambda l, j: (l, 0, j)),
        out_shape=jax.ShapeDtypeStruct((L, R, D6), F32),
        compiler_params=_cparams(("arbitrary", "arbitrary")),
        name="modulation",
    )(cc, w_mod, b_mod.reshape(L, 1, D6))


def _inproj_kernel(x_ref, mod_ref, g_ref, w_ref, *out_refs):
    x = x_ref[...]
    m = mod_ref[0]
    ms = jnp.mean(x * x, axis=-1, keepdims=True)
    h = x * lax.rsqrt(ms + EPS) * g_ref[...]
    h = h * (1.0 + m[1:2]) + m[0:1]
    hb = h.astype(BF16)
    for ref, (_, a, w) in zip(out_refs, IN_COLS):
        ref[...] = _dot(hb, w_ref[0, :, a:a + w]).astype(ref.dtype)


def _mod_row(i, tps, tiles_ctx):
    return (i // tps) * 2 + jnp.where((i % tps) >= tiles_ctx, 1, 0)


def _in_projection(X2, modl, g1, w_in_p, layer, tps, tiles_ctx):
    T, D = X2.shape
    dts = {"dt": F32}
    out_shape = [jax.ShapeDtypeStruct((T, w), dts.get(n, BF16)) for n, _, w in IN_COLS]
    out_specs = [pl.BlockSpec((TM, w), lambda i: (i, 0)) for _, _, w in IN_COLS]
    return pl.pallas_call(
        _inproj_kernel,
        grid=(T // TM,),
        in_specs=[pl.BlockSpec((TM, D), lambda i: (i, 0)),
                  pl.BlockSpec((1, 6, D), lambda i: (_mod_row(i, tps, tiles_ctx), 0, 0)),
                  _resident((1, D)),
                  _resident((D, D_IN_PAD), layer)],
        out_specs=out_specs,
        out_shape=out_shape,
        compiler_params=_cparams(("parallel",)),
        name="in_projection",
    )(X2, modl, g1, w_in_p)


def _norm_heads(x, g, bd):
    hi, lo = _split2(x * x)
    s = _dot(hi, bd) + _dot(lo, bd)
    return x * lax.rsqrt(s * (1.0 / HEAD_DIM) + EPS) * g


def _rope(x, cos, sin):
    lane = lax.broadcasted_iota(jnp.int32, x.shape, 1)
    sw = jnp.where((lane & 31) < 16, pltpu.roll(x, LANE - 16, 1), pltpu.roll(x, 16, 1))
    return x * cos + sw * sin


def _sink_softmax_pv(parts, sink_col):
    m = sink_col
    for s, _ in parts:
        m = jnp.maximum(m, jnp.max(s, axis=-1, keepdims=True))
    den = jnp.exp(sink_col - m)
    o = None
    for s, v in parts:
        p = jnp.exp(s - m)
        den = den + jnp.sum(p, axis=-1, keepdims=True)
        pv = _dot(p.astype(BF16), v)
        o = pv if o is None else o + pv
    return o / den


def _attn_kernel(sink_ref, q_ref, k_ref, v_ref, cos_ref, sin_ref, qg_ref, kg_ref, bd_ref, o_ref,
                 qh, kh, vh, *, LC, N):
    S = LC + N
    koff = LC + ATT_BLOCK
    bd = bd_ref[...]
    scale = HEAD_DIM ** -0.5
    zpad = jnp.zeros((ATT_BLOCK, HEAD_DIM), BF16)
    for h in range(ATT_KV_HEADS):
        for buf in (kh, vh):
            buf[h, LC:koff, :] = zpad
            buf[h, koff + N:koff + N + ATT_BLOCK, :] = zpad

    rc = TM
    for c0 in range(0, S, rc):
        lat = c0 >= LC
        dst = c0 + ATT_BLOCK if lat else c0
        if lat:
            cos = cos_ref[c0 - LC:c0 - LC + rc, :]
            sin = sin_ref[c0 - LC:c0 - LC + rc, :]
        kn = _norm_heads(k_ref[0, c0:c0 + rc, :].astype(F32), kg_ref[...], bd)
        if lat:
            kn = _rope(kn, cos, sin)
        knb = kn.astype(BF16)
        vv = v_ref[0, c0:c0 + rc, :]
        for h in range(ATT_KV_HEADS):
            kh[h, dst:dst + rc, :] = knb[:, h * HEAD_DIM:(h + 1) * HEAD_DIM]
            vh[h, dst:dst + rc, :] = vv[:, h * HEAD_DIM:(h + 1) * HEAD_DIM]
        for cb in range(ATT_HEADS // 2):
            qn = _norm_heads(q_ref[0, c0:c0 + rc, cb * LANE:(cb + 1) * LANE].astype(F32), qg_ref[...], bd)
            if lat:
                qn = _rope(qn, cos, sin)
            qnb = (qn * scale).astype(BF16)
            qh[2 * cb, c0:c0 + rc, :] = qnb[:, :HEAD_DIM]
            qh[2 * cb + 1, c0:c0 + rc, :] = qnb[:, HEAD_DIM:]

    def sink_column(h, rows):
        grp = lax.broadcasted_iota(jnp.int32, (ATT_REP * rows, 1), 0) // rows
        col = jnp.full((ATT_REP * rows, 1), sink_ref[ATT_REP * h], F32)
        for r in range(1, ATT_REP):
            col = jnp.where(grp == r, sink_ref[ATT_REP * h + r], col)
        return col

    def store_heads(o, h, row0, rows):
        for pr in range(ATT_REP // 2):
            pair = jnp.concatenate([o[(2 * pr) * rows:(2 * pr + 1) * rows],
                                    o[(2 * pr + 1) * rows:(2 * pr + 2) * rows]], axis=1)
            cb = (ATT_REP // 2) * h + pr
            o_ref[0, pl.ds(row0, rows), cb * LANE:(cb + 1) * LANE] = pair.astype(o_ref.dtype)

    for h in range(ATT_KV_HEADS):
        q4 = jnp.concatenate([qh[ATT_REP * h + r, 0:LC, :] for r in range(ATT_REP)], axis=0)
        kc = kh[h, 0:LC, :]
        vc = vh[h, 0:LC, :]
        o = _sink_softmax_pv([(_dot_nt(q4, kc), vc)], sink_column(h, LC))
        store_heads(o, h, 0, LC)

    nk = 3 * ATT_BLOCK
    qi = lax.broadcasted_iota(jnp.int32, (ATT_REP * ATT_BLOCK, nk), 0) & (ATT_BLOCK - 1)
    kj = lax.broadcasted_iota(jnp.int32, (ATT_REP * ATT_BLOCK, nk), 1)
    rel = kj - ATT_BLOCK - qi
    in_window = (rel <= ATT_WINDOW) & (rel >= -ATT_WINDOW)

    def block(j, carry):
        r0 = pl.multiple_of(j * ATT_BLOCK, ATT_BLOCK)
        kpos = kj + (j - 1) * ATT_BLOCK
        ok = in_window & (kpos >= 0) & (kpos < N)
        for h in range(ATT_KV_HEADS):
            q4 = jnp.concatenate([qh[ATT_REP * h + r, pl.ds(LC + r0, ATT_BLOCK), :] for r in range(ATT_REP)],
                                 axis=0)
            kb = kh[h, pl.ds(LC + r0, nk), :]
            vb = vh[h, pl.ds(LC + r0, nk), :]
            s_lat = jnp.where(ok, _dot_nt(q4, kb), NEG)
            s_ctx = _dot_nt(q4, kh[h, 0:LC, :])
            o = _sink_softmax_pv([(s_lat, vb), (s_ctx, vh[h, 0:LC, :])], sink_column(h, ATT_BLOCK))
            store_heads(o, h, LC + r0, ATT_BLOCK)
        return carry

    lax.fori_loop(0, N // ATT_BLOCK, block, 0)


def _attention(q, k, v, cos, sin, qg, kg, bd, sink, LC, N):
    B, S, _ = q.shape
    kern = functools.partial(_attn_kernel, LC=LC, N=N)
    grid_spec = pltpu.PrefetchScalarGridSpec(
        num_scalar_prefetch=1,
        grid=(B,),
        in_specs=[pl.BlockSpec((1, S, 512), lambda b, s: (b, 0, 0)),
                  pl.BlockSpec((1, S, 128), lambda b, s: (b, 0, 0)),
                  pl.BlockSpec((1, S, 128), lambda b, s: (b, 0, 0)),
                  pl.BlockSpec((N, LANE), lambda b, s: (0, 0)),
                  pl.BlockSpec((N, LANE), lambda b, s: (0, 0)),
                  pl.BlockSpec((1, LANE), lambda b, s: (0, 0)),
                  pl.BlockSpec((1, LANE), lambda b, s: (0, 0)),
                  pl.BlockSpec((LANE, LANE), lambda b, s: (0, 0))],
        out_specs=pl.BlockSpec((1, S, 512), lambda b, s: (b, 0, 0)),
        scratch_shapes=[pltpu.VMEM((ATT_HEADS, S, HEAD_DIM), BF16),
                        pltpu.VMEM((ATT_KV_HEADS, S + 2 * ATT_BLOCK, HEAD_DIM), BF16),
                        pltpu.VMEM((ATT_KV_HEADS, S + 2 * ATT_BLOCK, HEAD_DIM), BF16)])
    return pl.pallas_call(
        kern, grid_spec=grid_spec,
        out_shape=jax.ShapeDtypeStruct((B, S, 512), BF16),
        compiler_params=_cparams(("parallel",)),
        name="attention",
    )(sink, q, k, v, cos, sin, qg, kg, bd)


def _conv_kernel(glu_ref, dww_ref, dwb_ref, lng_ref, lnb_ref, pww_ref, pwb_ref, o_ref, hp, *, LC, N):
    C = GROUP_W
    half = CONV_K // 2
    rows = 128

    def stream(s0, ln):
        hp[0:CONV_PAD, :] = jnp.zeros((CONV_PAD, C), F32)
        hp[CONV_PAD + ln:2 * CONV_PAD + ln, :] = jnp.zeros((CONV_PAD, C), F32)

        def fill(i, carry):
            r0 = pl.multiple_of(i * TM, TM)
            g = glu_ref[0, pl.ds(s0 + r0, TM), :]
            a = g[:, :C].astype(F32)
            gate = g[:, C:].astype(F32)
            hp[pl.ds(CONV_PAD + r0, TM), :] = a * jax.nn.sigmoid(gate)
            return carry

        lax.fori_loop(0, ln // TM, fill, 0)

        def chunk(i, carry):
            r0 = pl.multiple_of(i * rows, rows)
            parts = []
            for cb in range(C // LANE):
                win = hp[pl.ds(r0, rows + 2 * CONV_PAD), cb * LANE:(cb + 1) * LANE]
                acc = jnp.zeros((rows, LANE), F32)
                for t in range(CONV_K):
                    off = CONV_PAD - half + t
                    w = dww_ref[t:t + 1, cb * LANE:(cb + 1) * LANE]
                    acc = acc + w * win[off:off + rows]
                parts.append(acc)
            y = jnp.concatenate(parts, axis=1) + dwb_ref[...]
            mu = jnp.mean(y, axis=-1, keepdims=True)
            yc = y - mu
            var = jnp.mean(yc * yc, axis=-1, keepdims=True)
            z = yc * lax.rsqrt(var + EPS) * lng_ref[...] + lnb_ref[...]
            z = _silu(z)
            out = _dot(z.astype(BF16), pww_ref[...]) + pwb_ref[...]
            o_ref[0, pl.ds(s0 + r0, rows), :] = out.astype(o_ref.dtype)
            return carry

        lax.fori_loop(0, ln // rows, chunk, 0)

    stream(0, LC)
    stream(LC, N)


def _conformer(glu, dww, dwb, lng, lnb, pww, pwb, LC, N):
    B, S, _ = glu.shape
    C = GROUP_W
    kern = functools.partial(_conv_kernel, LC=LC, N=N)
    vec = pl.BlockSpec((1, C), lambda b: (0, 0))
    return pl.pallas_call(
        kern, grid=(B,),
        in_specs=[pl.BlockSpec((1, S, 2 * C), lambda b: (b, 0, 0)),
                  pl.BlockSpec((32, C), lambda b: (0, 0)),
                  vec, vec, vec,
                  pl.BlockSpec((C, C), lambda b: (0, 0)),
                  vec],
        out_specs=pl.BlockSpec((1, S, C), lambda b: (b, 0, 0)),
        out_shape=jax.ShapeDtypeStruct((B, S, C), BF16),
        scratch_shapes=[pltpu.VMEM((max(LC, N) + 2 * CONV_PAD, C), F32)],
        compiler_params=_cparams(("parallel",)),
        name="conformer_conv",
    )(glu, dww, dwb, lng, lnb, pww, pwb)


def _ssd_kernel(xbc_ref, dt_ref, z_ref, cw_ref, cb_ref, arow_ref, dtb_ref, dsk_ref, ng_ref, o_ref,
                xp, xs, bm, cm, dts, yacc, state, *, LC, N):
    S = LC + N
    Q = SSD_CHUNK
    nc = S // Q
    nc_ctx = LC // Q
    ii = lax.broadcasted_iota(jnp.int32, (Q, Q), 0)
    jj = lax.broadcasted_iota(jnp.int32, (Q, Q), 1)
    lower = ii >= jj
    tri = (jnp.where(lower, 1.0, 0.0).astype(BF16), jnp.where(jj >= ii, 1.0, 0.0).astype(BF16))
    causal = (lower, jj >= ii)
    first_half = lax.broadcasted_iota(jnp.int32, (Q, LANE), 1) < SSD_STATE // 2

    def stream(s0, ln):
        xp[0:SSD_PAD, :] = jnp.zeros((SSD_PAD, SSD_XBC), F32)
        xp[SSD_PAD + ln:2 * SSD_PAD + ln, :] = jnp.zeros((SSD_PAD, SSD_XBC), F32)

        def fill(i, carry):
            r0 = pl.multiple_of(i * Q, Q)
            xp[pl.ds(SSD_PAD + r0, Q), :] = xbc_ref[0, pl.ds(s0 + r0, Q), :].astype(F32)
            return carry

        lax.fori_loop(0, ln // Q, fill, 0)

        def conv(i, carry):
            r0 = pl.multiple_of(i * Q, Q)
            for cb in range(SSD_XBC // LANE):
                cs = slice(cb * LANE, (cb + 1) * LANE)
                win = xp[pl.ds(r0, Q + 2 * SSD_PAD), cs]
                acc = jnp.zeros((Q, LANE), F32)
                for t in range(SSD_CONV):
                    acc = acc + cw_ref[t:t + 1, cs] * win[SSD_PAD - 2 + t:SSD_PAD - 2 + t + Q]
                y = _silu(acc + cb_ref[:, cs])
                if cb < SSD_INNER // LANE:
                    xs[pl.ds(s0 + r0, Q), cs] = y
                elif cb < (SSD_INNER + SSD_GROUPS * SSD_STATE) // LANE:
                    c2 = cb - SSD_INNER // LANE
                    bm[pl.ds(s0 + r0, Q), c2 * LANE:(c2 + 1) * LANE] = y
                else:
                    c2 = cb - (SSD_INNER + SSD_GROUPS * SSD_STATE) // LANE
                    cm[pl.ds(s0 + r0, Q), c2 * LANE:(c2 + 1) * LANE] = y
            return carry

        lax.fori_loop(0, ln // Q, conv, 0)

    stream(0, LC)
    stream(LC, N)

    def softplus_rows(i, carry):
        r0 = pl.multiple_of(i * Q, Q)
        v = dt_ref[0, pl.ds(r0, Q), :] + dtb_ref[...]
        dts[pl.ds(r0, Q), :] = jnp.maximum(v, 0.0) + jnp.log(1.0 + jnp.exp(-jnp.abs(v)))
        return carry

    lax.fori_loop(0, nc, softplus_rows, 0)

    def chunk_step(c, d):
        r0 = pl.multiple_of(c * Q, Q)
        dtc = dts[pl.ds(r0, Q), :]
        a1, a2, a3 = _split3(dtc * arow_ref[...])
        cum = _dot(tri[d], a1) + _dot(tri[d], a2) + _dot(tri[d], a3)
        cum_t = cum.T
        dt_t = dtc.T
        tot = cum[Q - 1:Q, :] if d == 0 else cum[0:1, :]
        e_cum = jnp.exp(cum)
        e_tot = jnp.exp(tot)
        w_state = jnp.exp(tot - cum) * dtc
        for g in range(SSD_GROUPS):
            bg = bm[pl.ds(r0, Q), g * SSD_STATE:(g + 1) * SSD_STATE]
            cg = cm[pl.ds(r0, Q), g * SSD_STATE:(g + 1) * SSD_STATE].astype(BF16)
            bg_t = bg.T.astype(BF16)
            cb_mat = _dot(cg, bg_t)
            for pr in range(SSD_HEADS // SSD_GROUPS // 2):
                pair = g * 2 + pr
                lanes = slice(pair * LANE, (pair + 1) * LANE)
                xpair = xs[pl.ds(r0, Q), lanes]
                y = None
                for sub in range(2):
                    hd = d * SSD_HEADS + pair * 2 + sub
                    seg = cum[:, hd:hd + 1] - cum_t[hd:hd + 1, :]
                    dec = jnp.where(causal[d], jnp.exp(jnp.minimum(seg, 0.0)), 0.0)
                    w = (cb_mat * dec * dt_t[hd:hd + 1, :]).astype(BF16)
                    xm = jnp.where(first_half if sub == 0 else ~first_half, xpair, 0.0).astype(BF16)
                    t = _dot(w, xm)
                    y = t if y is None else y + t
                h0 = d * SSD_HEADS + pair * 2
                sel = lambda m: jnp.where(first_half, m[:, h0:h0 + 1], m[:, h0 + 1:h0 + 2])
                st = state[d * (SSD_HEADS // 2) + pair]
                y = y + _dot(cg, st.astype(BF16)) * sel(e_cum)
                xw = (xpair * sel(w_state)).astype(BF16)
                state[d * (SSD_HEADS // 2) + pair] = st * sel(e_tot) + _dot(bg_t, xw)
                if d == 0:
                    yacc[pl.ds(r0, Q), lanes] = y
                else:
                    yacc[pl.ds(r0, Q), lanes] = yacc[pl.ds(r0, Q), lanes] + y

    state[...] = jnp.zeros(state.shape, F32)

    def fwd(c, carry):
        chunk_step(c, 0)
        return carry

    lax.fori_loop(0, nc, fwd, 0)

    def bwd_ctx(i, carry):
        chunk_step(nc_ctx - 1 - i, 1)
        return carry

    lax.fori_loop(0, nc_ctx, bwd_ctx, 0)

    def bwd_lat(i, carry):
        chunk_step(nc - 1 - i, 1)
        return carry

    lax.fori_loop(0, nc - nc_ctx, bwd_lat, 0)

    def gate_out(i, carry):
        r0 = pl.multiple_of(i * Q, Q)
        y = yacc[pl.ds(r0, Q), :] + xs[pl.ds(r0, Q), :] * dsk_ref[...]
        y = y * _silu(z_ref[0, pl.ds(r0, Q), :].astype(F32))
        ms = jnp.mean(y * y, axis=-1, keepdims=True)
        o_ref[0, pl.ds(r0, Q), :] = (y * lax.rsqrt(ms + EPS) * ng_ref[...]).astype(o_ref.dtype)
        return carry

    lax.fori_loop(0, nc, gate_out, 0)


def _ssd(xbc, dt, z, cw, cb, arow, dtb, dsk, ng, LC, N):
    B, S, _ = xbc.shape
    kern = functools.partial(_ssd_kernel, LC=LC, N=N)
    row = lambda w: pl.BlockSpec((1, w), lambda b: (0, 0))
    return pl.pallas_call(
        kern, grid=(B,),
        in_specs=[pl.BlockSpec((1, S, SSD_XBC), lambda b: (b, 0, 0)),
                  pl.BlockSpec((1, S, LANE), lambda b: (b, 0, 0)),
                  pl.BlockSpec((1, S, SSD_INNER), lambda b: (b, 0, 0)),
                  pl.BlockSpec((SSD_CONV, SSD_XBC), lambda b: (0, 0)),
                  row(SSD_XBC), row(LANE), row(LANE), row(SSD_INNER), row(SSD_INNER)],
        out_specs=pl.BlockSpec((1, S, SSD_INNER), lambda b: (b, 0, 0)),
        out_shape=jax.ShapeDtypeStruct((B, S, SSD_INNER), BF16),
        scratch_shapes=[pltpu.VMEM((max(LC, N) + 2 * SSD_PAD, SSD_XBC), F32),
                        pltpu.VMEM((S, SSD_INNER), F32),
                        pltpu.VMEM((S, SSD_GROUPS * SSD_STATE), F32),
                        pltpu.VMEM((S, SSD_GROUPS * SSD_STATE), F32),
                        pltpu.VMEM((S, LANE), F32),
                        pltpu.VMEM((S, SSD_INNER), F32),
                        pltpu.VMEM((SSD_HEADS, SSD_STATE, LANE), F32)],
        compiler_params=_cparams(("parallel",)),
        name="ssd",
    )(xbc, dt, z, cw, cb, arow, dtb, dsk, ng)


def _pool_kernel(pin_ref, pw_ref, ps_ref, o_ref, up, *, LC, N):
    C = GROUP_W
    rows = TM

    def stream(s0, ln):
        up[0:POOL_PAD, :] = jnp.zeros((POOL_PAD, C), F32)
        up[POOL_PAD + ln:2 * POOL_PAD + ln, :] = jnp.zeros((POOL_PAD, C), F32)

        def fill(i, carry):
            r0 = pl.multiple_of(i * rows, rows)
            up[pl.ds(POOL_PAD + r0, rows), :] = pin_ref[0, pl.ds(s0 + r0, rows), :].astype(F32)
            return carry

        lax.fori_loop(0, ln // rows, fill, 0)

        def chunk(i, carry):
            r0 = pl.multiple_of(i * rows, rows)
            t = r0 + lax.broadcasted_iota(jnp.int32, (rows, 1), 0)
            outs = []
            for gi, w in enumerate(POOL_SIZES):
                cs = slice(gi * POOL_CH, (gi + 1) * POOL_CH)
                win = up[pl.ds(r0, rows + 2 * POOL_PAD), cs]
                acc = jnp.zeros((rows, POOL_CH), F32)
                for d in range(-(w // 2), w - w // 2):
                    acc = acc + win[POOL_PAD + d:POOL_PAD + d + rows]
                cnt = jnp.minimum(t + (w - w // 2), ln) - jnp.maximum(t - w // 2, 0)
                p = acc / cnt.astype(F32) - win[POOL_PAD:POOL_PAD + rows]
                outs.append(_dot(p.astype(BF16), pw_ref[gi]))
            y = jnp.concatenate(outs, axis=1) * ps_ref[...]
            o_ref[0, pl.ds(s0 + r0, rows), :] = y.astype(o_ref.dtype)
            return carry

        lax.fori_loop(0, ln // rows, chunk, 0)

    stream(0, LC)
    stream(LC, N)


def _pool(pin, pw, ps, LC, N):
    B, S, C = pin.shape
    kern = functools.partial(_pool_kernel, LC=LC, N=N)
    return pl.pallas_call(
        kern, grid=(B,),
        in_specs=[pl.BlockSpec((1, S, C), lambda b: (b, 0, 0)),
                  pl.BlockSpec((len(POOL_SIZES), POOL_CH, POOL_CH), lambda b: (0, 0, 0)),
                  pl.BlockSpec((1, C), lambda b: (0, 0))],
        out_specs=pl.BlockSpec((1, S, C), lambda b: (b, 0, 0)),
        out_shape=jax.ShapeDtypeStruct((B, S, C), BF16),
        scratch_shapes=[pltpu.VMEM((max(LC, N) + 2 * POOL_PAD, C), F32)],
        compiler_params=_cparams(("parallel",)),
        name="pool_mixer",
    )(pin, pw, ps)


def _pack_halves(x):
    w = x.shape[1] // 2
    u = lax.bitcast_convert_type(x.astype(BF16).astype(F32), jnp.uint32)
    return (u[:, :w] >> 16) | (u[:, w:] & jnp.uint32(0xFFFF0000))


def _unpack_halves(p):
    lo = lax.bitcast_convert_type(p << 16, F32)
    hi = lax.bitcast_convert_type(p & jnp.uint32(0xFFFF0000), F32)
    return lo, hi


def _outproj_kernel(ya_ref, yb_ref, yc_ref, yd_ref, x_ref, mod_ref, g_ref, w_ref, wr_ref,
                    xo_ref, h2_ref, lg_ref):
    acc = None
    for i, ref in enumerate((ya_ref, yb_ref, yc_ref, yd_ref)):
        t = _dot(ref[...], w_ref[0, i * GROUP_W:(i + 1) * GROUP_W, :])
        acc = t if acc is None else acc + t
    m = mod_ref[0]
    x = x_ref[...] + m[2:3] * acc
    xo_ref[...] = x
    ms = jnp.mean(x * x, axis=-1, keepdims=True)
    h2 = x * lax.rsqrt(ms + EPS) * g_ref[...]
    h2 = h2 * (1.0 + m[4:5]) + m[3:4]
    h2_ref[...] = _pack_halves(h2)
    h_hi, h_lo = _split2(h2)
    both = _dot(h_hi, wr_ref[0])
    lg_ref[...] = both[:, :LANE] + (both[:, LANE:] + _dot(h_lo, wr_ref[0, :, :LANE]))


def _out_projection(ys, X2, modl, g2, w_out, wr, layer, tps, tiles_ctx):
    T, D = X2.shape
    ytile = pl.BlockSpec((TM, GROUP_W), lambda i: (i, 0))
    return pl.pallas_call(
        _outproj_kernel,
        grid=(T // TM,),
        in_specs=[ytile, ytile, ytile, ytile,
                  pl.BlockSpec((TM, D), lambda i: (i, 0)),
                  pl.BlockSpec((1, 6, D), lambda i: (_mod_row(i, tps, tiles_ctx), 0, 0)),
                  _resident((1, D)),
                  _resident((4 * GROUP_W, D), layer),
                  _resident((D, 2 * LANE), layer)],
        out_specs=[pl.BlockSpec((TM, D), lambda i: (i, 0)),
                   pl.BlockSpec((TM, D // 2), lambda i: (i, 0)),
                   pl.BlockSpec((TM, LANE), lambda i: (i, 0))],
        out_shape=[jax.ShapeDtypeStruct((T, D), F32),
                   jax.ShapeDtypeStruct((T, D // 2), jnp.uint32),
                   jax.ShapeDtypeStruct((T, LANE), F32)],
        input_output_aliases={4: 0},
        compiler_params=_cparams(("parallel",)),
        name="out_projection",
    )(*ys, X2, modl, g2, w_out, wr)


META_E = 0
META_R = 2
META_W = 4


def _route_kernel(lg_ref, meta_ref, cnt_ref, carry):
    @pl.when(pl.program_id(0) == 0)
    def _():
        carry[...] = jnp.zeros(carry.shape, F32)

    lg = lg_ref[...]
    lane = lax.broadcasted_iota(jnp.int32, lg.shape, 1).astype(F32)
    big = 1e9
    rmax = lambda m: jnp.max(jnp.where(m, lg, NEG), axis=-1, keepdims=True)
    first = lambda m: jnp.min(jnp.where(m, lane, big), axis=-1, keepdims=True)

    gm = lane < MOE_GROUPS
    gmax = rmax(gm)
    gidx = first(gm & (lg == gmax))
    g_w = 1.0 / jnp.sum(jnp.where(gm, jnp.exp(lg - gmax), 0.0), axis=-1, keepdims=True)
    lo = MOE_GROUPS + MOE_PER_GROUP * gidx
    em = (lane >= lo) & (lane < lo + MOE_PER_GROUP)
    v1 = rmax(em)
    i1 = first(em & (lg == v1))
    em2 = em & (lane != i1)
    v2 = rmax(em2)
    i2 = first(em2 & (lg == v2))
    t = jnp.exp(v2 - v1)
    w1 = g_w / (1.0 + t)
    w2 = g_w * t / (1.0 + t)

    oh1 = jnp.where(lane == i1, 1.0, 0.0)
    oh2 = jnp.where(lane == i2, 1.0, 0.0)
    oh = oh1 + oh2
    rows = lg.shape[0]
    ri = lax.broadcasted_iota(jnp.int32, (rows, rows), 0)
    ci = lax.broadcasted_iota(jnp.int32, (rows, rows), 1)
    before = jnp.where(ci < ri, 1.0, 0.0).astype(BF16)
    base = _dot(before, oh.astype(BF16)) + carry[0:1, :]
    r1 = jnp.sum(oh1 * base, axis=-1, keepdims=True)
    r2 = jnp.sum(oh2 * base, axis=-1, keepdims=True)
    carry[0:1, :] = carry[0:1, :] + jnp.sum(oh, axis=0, keepdims=True)

    meta = jnp.zeros(lg.shape, F32)
    for k, val in enumerate((i1 - MOE_GROUPS, i2 - MOE_GROUPS, r1, r2, w1, w2)):
        meta = jnp.where(lane == k, val, meta)
    meta_ref[...] = meta
    cnt_ref[...] = jnp.broadcast_to(carry[0:1, :], cnt_ref.shape)


def _route(logits, n_tiles):
    T = logits.shape[0]
    meta, cnt = pl.pallas_call(
        _route_kernel,
        grid=(T // TM,),
        in_specs=[pl.BlockSpec((TM, LANE), lambda i: (i, 0))],
        out_specs=[pl.BlockSpec((TM, LANE), lambda i: (i, 0)),
                   pl.BlockSpec((8, LANE), lambda i: (0, 0))],
        out_shape=[jax.ShapeDtypeStruct((T, LANE), F32), jax.ShapeDtypeStruct((8, LANE), F32)],
        scratch_shapes=[pltpu.VMEM((8, LANE), F32)],
        compiler_params=_cparams(("arbitrary",)),
        name="moe_route",
    )(logits)
    idx = meta[:, :4].astype(jnp.int32).reshape(T // TM, TM, 4).transpose(0, 2, 1)
    counts = cnt[0, MOE_GROUPS:MOE_GROUPS + MOE_EXPERTS].astype(jnp.int32)
    ntile = (counts + TM - 1) // TM
    tile_end = jnp.cumsum(ntile)
    tile_start = (tile_end - ntile).astype(jnp.int32)
    total = tile_end[-1]
    tiles = jnp.arange(n_tiles, dtype=jnp.int32)
    active = tiles < total
    last = jnp.minimum(tiles, total - 1)
    t_exp = jnp.sum((last[:, None] >= tile_end[None, :]).astype(jnp.int32), axis=1)
    t_first = (active & (tiles == jnp.sum(jnp.where(t_exp[:, None] == jnp.arange(MOE_EXPERTS)[None, :],
                                                     tile_start[None, :], 0), axis=1))).astype(jnp.int32)
    return meta, idx, tile_start, t_exp, t_first, active.astype(jnp.int32)


def _row_dest(ts_ref, idx_ref, r, slot):
    return ts_ref[idx_ref[0, META_E + slot, r]] * TM + idx_ref[0, META_R + slot, r]


def _dispatch_kernel(ts_ref, idx_ref, h_ref, xs_in_ref, xs_ref, sem):
    del xs_in_ref

    def row_copy(r, dst):
        return pltpu.make_async_copy(h_ref.at[pl.ds(r, 1)], xs_ref.at[pl.ds(dst, 1)], sem)

    def start(r, carry):
        for slot in range(2):
            row_copy(r, _row_dest(ts_ref, idx_ref, r, slot)).start()
        return carry

    lax.fori_loop(0, TM, start, 0, unroll=8)
    for _ in range(2 * TM):
        row_copy(0, 0).wait()


def _moe_dispatch(h2p, idx, tile_start, n_tiles):
    T, W = h2p.shape
    grid_spec = pltpu.PrefetchScalarGridSpec(
        num_scalar_prefetch=1,
        grid=(T // TM,),
        in_specs=[pl.BlockSpec((1, 4, TM), lambda i, ts: (i, 0, 0), memory_space=pltpu.SMEM),
                  pl.BlockSpec((TM, W), lambda i, ts: (i, 0)),
                  pl.BlockSpec(memory_space=pl.ANY)],
        out_specs=pl.BlockSpec(memory_space=pl.ANY),
        scratch_shapes=[pltpu.SemaphoreType.DMA(())])
    return pl.pallas_call(
        _dispatch_kernel, grid_spec=grid_spec,
        out_shape=jax.ShapeDtypeStruct((n_tiles * TM, W), jnp.uint32),
        input_output_aliases={3: 0},
        compiler_params=_cparams(("arbitrary",)),
        name="moe_dispatch",
    )(tile_start, idx, h2p, jnp.zeros((n_tiles * TM, W), jnp.uint32))


def _moe_kernel(te_ref, tf_ref, tv_ref, x_ref, wg_ref, wu_ref, wd_ref, o_ref, wg_b, wu_b, wd_b):
    i = pl.program_id(0)

    @pl.when(tf_ref[i] == 1)
    def _():
        wg_b[...] = wg_ref[0, 0].astype(BF16)
        wu_b[...] = wu_ref[0, 0].astype(BF16)
        wd_b[...] = wd_ref[0, 0].astype(BF16)

    @pl.when(tv_ref[i] == 1)
    def _():
        lo, hi = _unpack_halves(x_ref[...])
        lo = lo.astype(BF16)
        hi = hi.astype(BF16)
        half = lo.shape[1]
        g = _dot(lo, wg_b[:half, :]) + _dot(hi, wg_b[half:, :])
        u = _dot(lo, wu_b[:half, :]) + _dot(hi, wu_b[half:, :])
        hid = (_silu(g) * u).astype(BF16)
        o_ref[...] = _pack_halves(_dot(hid, wd_b[...]))

    @pl.when(tv_ref[i] == 0)
    def _():
        o_ref[...] = jnp.zeros(o_ref.shape, o_ref.dtype)


def _moe_experts(xs, t_exp, t_first, t_active, w_gate, w_up, w_down, layer):
    R, W = xs.shape
    D = 2 * W
    n_tiles = R // TM
    wmap = lambda i, te, tf, tv: (layer, te[i], 0, 0)
    grid_spec = pltpu.PrefetchScalarGridSpec(
        num_scalar_prefetch=3,
        grid=(n_tiles,),
        in_specs=[pl.BlockSpec((TM, W), lambda i, te, tf, tv: (i, 0)),
                  pl.BlockSpec((1, 1, D, D_EXPERT), wmap),
                  pl.BlockSpec((1, 1, D, D_EXPERT), wmap),
                  pl.BlockSpec((1, 1, D_EXPERT, D), wmap)],
        out_specs=pl.BlockSpec((TM, W), lambda i, te, tf, tv: (i, 0)),
        scratch_shapes=[pltpu.VMEM((D, D_EXPERT), BF16),
                        pltpu.VMEM((D, D_EXPERT), BF16),
                        pltpu.VMEM((D_EXPERT, D), BF16)])
    return pl.pallas_call(
        _moe_kernel, grid_spec=grid_spec,
        out_shape=jax.ShapeDtypeStruct((R, W), jnp.uint32),
        compiler_params=_cparams(("arbitrary",)),
        name="moe_experts",
    )(t_exp, t_first, t_active, xs, w_gate, w_up, w_down)


def _combine_kernel(ts_ref, idx_ref, idx_next_ref, x_ref, meta_ref, mod_ref, ye_ref, o_ref, ybuf, sems):
    i = pl.program_id(0)
    n = pl.num_programs(0)
    cur = i % 2

    def row_copy(src, buf, slot, r):
        return pltpu.make_async_copy(ye_ref.at[pl.ds(src, 1)], ybuf.at[buf, slot, pl.ds(r, 1)], sems.at[buf])

    def gather(iref, buf):
        def start(r, carry):
            for slot in range(2):
                row_copy(_row_dest(ts_ref, iref, r, slot), buf, slot, r).start()
            return carry

        lax.fori_loop(0, TM, start, 0, unroll=8)

    @pl.when(i == 0)
    def _():
        gather(idx_ref, 0)

    @pl.when(i + 1 < n)
    def _():
        gather(idx_next_ref, 1 - cur)

    for _ in range(2 * TM):
        row_copy(0, cur, 0, 0).wait()

    half = x_ref.shape[1] // 2
    meta = meta_ref[...]
    w1 = meta[:, META_W:META_W + 1]
    w2 = meta[:, META_W + 1:META_W + 2]
    lo1, hi1 = _unpack_halves(ybuf[cur, 0])
    lo2, hi2 = _unpack_halves(ybuf[cur, 1])
    g2 = mod_ref[0][5:6]
    o_ref[:, :half] = x_ref[:, :half] + g2[:, :half] * (w1 * lo1 + w2 * lo2)
    o_ref[:, half:] = x_ref[:, half:] + g2[:, half:] * (w1 * hi1 + w2 * hi2)


def _moe_combine(X2, ye, meta, idx, tile_start, modl, tps, tiles_ctx):
    T, D = X2.shape
    steps = T // TM
    tile = pl.BlockSpec((TM, D), lambda i, ts: (i, 0))
    grid_spec = pltpu.PrefetchScalarGridSpec(
        num_scalar_prefetch=1,
        grid=(steps,),
        in_specs=[pl.BlockSpec((1, 4, TM), lambda i, ts: (i, 0, 0), memory_space=pltpu.SMEM),
                  pl.BlockSpec((1, 4, TM), lambda i, ts: (jnp.minimum(i + 1, steps - 1), 0, 0),
                               memory_space=pltpu.SMEM),
                  tile,
                  pl.BlockSpec((TM, LANE), lambda i, ts: (i, 0)),
                  pl.BlockSpec((1, 6, D), lambda i, ts: (_mod_row(i, tps, tiles_ctx), 0, 0)),
                  pl.BlockSpec(memory_space=pl.ANY)],
        out_specs=tile,
        scratch_shapes=[pltpu.VMEM((2, 2, TM, D // 2), jnp.uint32),
                        pltpu.SemaphoreType.DMA((2,))])
    return pl.pallas_call(
        _combine_kernel, grid_spec=grid_spec,
        out_shape=jax.ShapeDtypeStruct((T, D), F32),
        input_output_aliases={3: 0},
        compiler_params=_cparams(("arbitrary",)),
        name="moe_combine",
    )(tile_start, idx, idx, X2, meta, modl, ye)


def _rope_tables(n):
    rows = n // GRID_W
    row = jnp.repeat(jnp.arange(rows, dtype=F32), GRID_W)
    col = jnp.tile(jnp.arange(GRID_W, dtype=F32), rows)
    half = HEAD_DIM // 2
    inv_freq = 1.0 / (ROPE_THETA ** (jnp.arange(0, half, 2, dtype=F32) / half))
    ar = row[:, None] * inv_freq
    ac = col[:, None] * inv_freq
    cos = jnp.concatenate([jnp.cos(ar), jnp.cos(ar), jnp.cos(ac), jnp.cos(ac)], axis=-1)
    sin = jnp.concatenate([-jnp.sin(ar), jnp.sin(ar), -jnp.sin(ac), jnp.sin(ac)], axis=-1)
    return jnp.tile(cos, (1, 2)), jnp.tile(sin, (1, 2))


def kernel(x, c, ctx, c_ctx, norm1_g, norm2_g, w_mod, b_mod, w_in, w_out, q_norm_g, k_norm_g, attn_sink,
           conv_dw_w, conv_dw_b, conv_ln_g, conv_ln_b, conv_pw_w, conv_pw_b, ssd_conv_w, ssd_conv_b, ssd_a_log,
           ssd_dt_bias, ssd_d, ssd_norm_g, pool_w, pool_scale, moe_group_router, moe_expert_router, moe_w_gate,
           moe_w_up, moe_w_down):
    B, N, D = x.shape
    LC = ctx.shape[1]
    S = LC + N
    L = w_mod.shape[0]
    assert LC % TM == 0 and N % TM == 0 and N % GRID_W == 0
    tps = S // TM
    tiles_ctx = LC // TM
    T = B * S
    n_tiles = (2 * T) // TM + MOE_EXPERTS

    rows = -(-(B + 1) // 8) * 8
    cc = jnp.concatenate([c, c_ctx[None, :], jnp.zeros((rows - B - 1, D), F32)], axis=0)
    mod = _modulation(cc, w_mod, b_mod).reshape(L, rows, 6, D)

    w_in_p = jnp.concatenate([w_in[:, :, :DT_SRC + 16], jnp.zeros((L, D, LANE - 16), F32), w_in[:, :, DT_SRC + 16:]],
                             axis=-1).astype(BF16)
    w_out_b = w_out.astype(BF16)
    w_router = jnp.concatenate([moe_group_router, moe_expert_router,
                                jnp.zeros((L, D, LANE - MOE_GROUPS - MOE_EXPERTS), F32)], axis=-1)
    wr_hi = w_router.astype(BF16)
    wr = jnp.concatenate([wr_hi, (w_router - wr_hi.astype(F32)).astype(BF16)], axis=-1)

    cos, sin = _rope_tables(N)
    bd = (jnp.arange(LANE)[:, None] // HEAD_DIM == jnp.arange(LANE)[None, :] // HEAD_DIM).astype(BF16)
    pad_lane = lambda v: jnp.concatenate([v.reshape(-1), jnp.zeros((LANE - v.size,), F32)]).reshape(1, LANE)

    X = jnp.concatenate([ctx, x], axis=1).reshape(T, D)
    for l in range(L):
        modl = jnp.stack([jnp.broadcast_to(mod[l, B], (B, 6, D)), mod[l, :B]], axis=1).reshape(2 * B, 6, D)
        q, k, v, glu, z, xbc, dt, pin = _in_projection(X, modl, norm1_g[l][None], w_in_p, l, tps, tiles_ctx)
        r3 = lambda a: a.reshape(B, S, a.shape[-1])
        y_att = _attention(r3(q), r3(k), r3(v), cos, sin, jnp.tile(q_norm_g[l], 2)[None], jnp.tile(k_norm_g[l], 2)[None],
                           bd, attn_sink[l], LC, N)
        dww = jnp.concatenate([conv_dw_w[l], jnp.zeros((32 - CONV_K, GROUP_W), F32)], axis=0)
        y_conv = _conformer(r3(glu), dww, conv_dw_b[l][None], conv_ln_g[l][None], conv_ln_b[l][None],
                            conv_pw_w[l].astype(BF16), conv_pw_b[l][None], LC, N)
        y_ssd = _ssd(r3(xbc), r3(dt), r3(z), ssd_conv_w[l], ssd_conv_b[l][None], pad_lane(-jnp.exp(ssd_a_log[l])),
                     pad_lane(ssd_dt_bias[l]), jnp.repeat(ssd_d[l], HEAD_DIM)[None], ssd_norm_g[l][None], LC, N)
        y_pool = _pool(r3(pin), pool_w[l].astype(BF16), pool_scale[l][None], LC, N)
        ys = [a.reshape(T, GROUP_W) for a in (y_att, y_conv, y_ssd, y_pool)]
        X, h2p, logits = _out_projection(ys, X, modl, norm2_g[l][None], w_out_b, wr, l, tps, tiles_ctx)

        meta, idx, tile_start, t_exp, t_first, t_active = _route(logits, n_tiles)
        xs = _moe_dispatch(h2p, idx, tile_start, n_tiles)
        ye = _moe_experts(xs, t_exp, t_first, t_active, moe_w_gate, moe_w_up, moe_w_down, l)
        X = _moe_combine(X, ye, meta, idx, tile_start, modl, tps, tiles_ctx)
    return X.reshape(B, S, D)[:, LC:]
```

```python
import functools
import math

import jax
import jax.numpy as jnp
from jax import lax
from jax.experimental import pallas as pl
from jax.experimental.pallas import tpu as pltpu

F32 = jnp.float32
BF16 = jnp.bfloat16

EPS = 1e-6
GRID_W = 64
HEAD_DIM = 64
ATT_HEADS = 8
ATT_KV_HEADS = 2
ATT_REP = ATT_HEADS // ATT_KV_HEADS
ATT_BLOCK = 128
ATT_WINDOW = 128
ROPE_THETA = 10000.0
GROUP_W = 512
CONV_K = 31
CONV_PAD = 16
CONV_ROWS = 128
SSD_HEADS = 8
SSD_GROUPS = 2
SSD_STATE = 128
SSD_CHUNK = 128
SSD_CONV = 4
SSD_PAD = 8
SSD_INNER = 512
SSD_XBC = SSD_INNER + 2 * SSD_GROUPS * SSD_STATE
POOL_SIZES = (2, 4, 8, 16)
POOL_CH = 128
POOL_PAD = 8
MOE_GROUPS = 4
MOE_PER_GROUP = 8
MOE_EXPERTS = 32
D_EXPERT = 512

TM = 256
LANE = 128
NEG = -1e30
VMEM_LIMIT = 56 * 1024 * 1024

IN_COLS = (("q", 0, 512), ("k", 512, 128), ("v", 640, 128), ("glu", 768, 1024), ("z", 1792, 512),
           ("xbc", 2304, 1024), ("dt", 3328, 128), ("pin", 3456, 512))
D_IN_PAD = 3968
DT_SRC = 3328


def _cparams(sem):
    return pltpu.CompilerParams(dimension_semantics=sem, vmem_limit_bytes=VMEM_LIMIT)


def _resident(shape, layer=None):
    nd = len(shape)
    if layer is None:
        return pl.BlockSpec(shape, lambda *_: (0,) * nd, pipeline_mode=pl.Buffered(1))
    return pl.BlockSpec((1,) + tuple(shape), lambda *_: (layer,) + (0,) * nd, pipeline_mode=pl.Buffered(1))


def _dot(a, b):
    return jnp.dot(a, b, preferred_element_type=F32)


def _dot_nt(a, b):
    return lax.dot_general(a, b, (((1,), (1,)), ((), ())), preferred_element_type=F32)


def _split2(x):
    hi = x.astype(BF16)
    lo = (x - hi.astype(F32)).astype(BF16)
    return hi, lo


def _split3(x):
    h1 = x.astype(BF16)
    r1 = x - h1.astype(F32)
    h2 = r1.astype(BF16)
    h3 = (r1 - h2.astype(F32)).astype(BF16)
    return h1, h2, h3


def _silu(x):
    return x * jax.nn.sigmoid(x)


def _mod_kernel(c_ref, w_ref, b_ref, o_ref):
    s = _silu(c_ref[...])
    o_ref[0] = _dot(s.astype(BF16), w_ref[0].astype(BF16)) + b_ref[0]


def _modulation(cc, w_mod, b_mod):
    L, D, D6 = w_mod.shape
    R = cc.shape[0]
    tn = 1024
    return pl.pallas_call(
        _mod_kernel,
        grid=(L, D6 // tn),
        in_specs=[pl.BlockSpec((R, D), lambda l, j: (0, 0)),
                  pl.BlockSpec((1, D, tn), lambda l, j: (l, 0, j)),
                  pl.BlockSpec((1, 1, tn), lambda l, j: (l, 0, j))],
        out_specs=pl.BlockSpec((1, R, tn), lambda l, j: (l, 0, j)),
        out_shape=jax.ShapeDtypeStruct((L, R, D6), F32),
        compiler_params=_cparams(("arbitrary", "arbitrary")),
        name="modulation",
    )(cc, w_mod, b_mod.reshape(L, 1, D6))


def _inproj_kernel(x_ref, mod_ref, g_ref, w_ref, *out_refs):
    x = x_ref[...]
    m = mod_ref[0]
    ms = jnp.mean(x * x, axis=-1, keepdims=True)
    h = x * lax.rsqrt(ms + EPS) * g_ref[...]
    h = h * (1.0 + m[1:2]) + m[0:1]
    hb = h.astype(BF16)
    for ref, (_, a, w) in zip(out_refs, IN_COLS):
        ref[...] = _dot(hb, w_ref[0, :, a:a + w]).astype(ref.dtype)


def _mod_row(i, tps, tiles_ctx):
    return (i // tps) * 2 + jnp.where((i % tps) >= tiles_ctx, 1, 0)


def _in_projection(X2, modl, g1, w_in_p, layer, tps, tiles_ctx):
    T, D = X2.shape
    dts = {"dt": F32}
    out_shape = [jax.ShapeDtypeStruct((T, w), dts.get(n, BF16)) for n, _, w in IN_COLS]
    out_specs = [pl.BlockSpec((TM, w), lambda i: (i, 0)) for _, _, w in IN_COLS]
    return pl.pallas_call(
        _inproj_kernel,
        grid=(T // TM,),
        in_specs=[pl.BlockSpec((TM, D), lambda i: (i, 0)),
                  pl.BlockSpec((1, 6, D), lambda i: (_mod_row(i, tps, tiles_ctx), 0, 0)),
                  _resident((1, D)),
                  _resident((D, D_IN_PAD), layer)],
        out_specs=out_specs,
        out_shape=out_shape,
        compiler_params=_cparams(("parallel",)),
        name="in_projection",
    )(X2, modl, g1, w_in_p)


def _norm_heads(x, g, bd):
    hi, lo = _split2(x * x)
    s = _dot(hi, bd) + _dot(lo, bd)
    return x * lax.rsqrt(s * (1.0 / HEAD_DIM) + EPS) * g


def _rope(x, cos, sin):
    lane = lax.broadcasted_iota(jnp.int32, x.shape, 1)
    sw = jnp.where((lane & 31) < 16, pltpu.roll(x, LANE - 16, 1), pltpu.roll(x, 16, 1))
    return x * cos + sw * sin


def _sink_softmax_pv(parts, sink_col):
    m = sink_col
    for s, _ in parts:
        m = jnp.maximum(m, jnp.max(s, axis=-1, keepdims=True))
    den = jnp.exp(sink_col - m)
    o = None
    for s, v in parts:
        p = jnp.exp(s - m)
        den = den + jnp.sum(p, axis=-1, keepdims=True)
        pv = _dot(p.astype(BF16), v)
        o = pv if o is None else o + pv
    return o / den


def _attn_kernel(sink_ref, q_ref, k_ref, v_ref, cos_ref, sin_ref, qg_ref, kg_ref, bd_ref, o_ref,
                 qh, kh, vh, *, LC, N):
    S = LC + N
    koff = LC + ATT_BLOCK
    bd = bd_ref[...]
    scale = HEAD_DIM ** -0.5
    zpad = jnp.zeros((ATT_BLOCK, HEAD_DIM), BF16)
    for h in range(ATT_KV_HEADS):
        for buf in (kh, vh):
            buf[h, LC:koff, :] = zpad
            buf[h, koff + N:koff + N + ATT_BLOCK, :] = zpad

    rc = TM
    for c0 in range(0, S, rc):
        lat = c0 >= LC
        dst = c0 + ATT_BLOCK if lat else c0
        if lat:
            cos = cos_ref[c0 - LC:c0 - LC + rc, :]
            sin = sin_ref[c0 - LC:c0 - LC + rc, :]
        kn = _norm_heads(k_ref[0, c0:c0 + rc, :].astype(F32), kg_ref[...], bd)
        if lat:
            kn = _rope(kn, cos, sin)
        knb = kn.astype(BF16)
        vv = v_ref[0, c0:c0 + rc, :]
        for h in range(ATT_KV_HEADS):
            kh[h, dst:dst + rc, :] = knb[:, h * HEAD_DIM:(h + 1) * HEAD_DIM]
            vh[h, dst:dst + rc, :] = vv[:, h * HEAD_DIM:(h + 1) * HEAD_DIM]
        for cb in range(ATT_HEADS // 2):
            qn = _norm_heads(q_ref[0, c0:c0 + rc, cb * LANE:(cb + 1) * LANE].astype(F32), qg_ref[...], bd)
            if lat:
                qn = _rope(qn, cos, sin)
            qnb = (qn * scale).astype(BF16)
            qh[2 * cb, c0:c0 + rc, :] = qnb[:, :HEAD_DIM]
            qh[2 * cb + 1, c0:c0 + rc, :] = qnb[:, HEAD_DIM:]

    def sink_column(h, rows):
        grp = lax.broadcasted_iota(jnp.int32, (ATT_REP * rows, 1), 0) // rows
        col = jnp.full((ATT_REP * rows, 1), sink_ref[ATT_REP * h], F32)
        for r in range(1, ATT_REP):
            col = jnp.where(grp == r, sink_ref[ATT_REP * h + r], col)
        return col

    def store_heads(o, h, row0, rows):
        for pr in range(ATT_REP // 2):
            pair = jnp.concatenate([o[(2 * pr) * rows:(2 * pr + 1) * rows],
                                    o[(2 * pr + 1) * rows:(2 * pr + 2) * rows]], axis=1)
            cb = (ATT_REP // 2) * h + pr
            o_ref[0, pl.ds(row0, rows), cb * LANE:(cb + 1) * LANE] = pair.astype(o_ref.dtype)

    for h in range(ATT_KV_HEADS):
        q4 = jnp.concatenate([qh[ATT_REP * h + r, 0:LC, :] for r in range(ATT_REP)], axis=0)
        kc = kh[h, 0:LC, :]
        vc = vh[h, 0:LC, :]
        o = _sink_softmax_pv([(_dot_nt(q4, kc), vc)], sink_column(h, LC))
        store_heads(o, h, 0, LC)

    nk = 3 * ATT_BLOCK
    qi = lax.broadcasted_iota(jnp.int32, (ATT_REP * ATT_BLOCK, nk), 0) & (ATT_BLOCK - 1)
    kj = lax.broadcasted_iota(jnp.int32, (ATT_REP * ATT_BLOCK, nk), 1)
    rel = kj - ATT_BLOCK - qi
    in_window = (rel <= ATT_WINDOW) & (rel >= -ATT_WINDOW)

    def block(j, carry):
        r0 = pl.multiple_of(j * ATT_BLOCK, ATT_BLOCK)
        kpos = kj + (j - 1) * ATT_BLOCK
        ok = in_window & (kpos >= 0) & (kpos < N)
        for h in range(ATT_KV_HEADS):
            q4 = jnp.concatenate([qh[ATT_REP * h + r, pl.ds(LC + r0, ATT_BLOCK), :] for r in range(ATT_REP)],
                                 axis=0)
            kb = kh[h, pl.ds(LC + r0, nk), :]
            vb = vh[h, pl.ds(LC + r0, nk), :]
            s_lat = jnp.where(ok, _dot_nt(q4, kb), NEG)
            s_ctx = _dot_nt(q4, kh[h, 0:LC, :])
            o = _sink_softmax_pv([(s_lat, vb), (s_ctx, vh[h, 0:LC, :])], sink_column(h, ATT_BLOCK))
            store_heads(o, h, LC + r0, ATT_BLOCK)
        return carry

    lax.fori_loop(0, N // ATT_BLOCK, block, 0)


def _attention(q, k, v, cos, sin, qg, kg, bd, sink, LC, N):
    B, S, _ = q.shape
    kern = functools.partial(_attn_kernel, LC=LC, N=N)
    grid_spec = pltpu.PrefetchScalarGridSpec(
        num_scalar_prefetch=1,
        grid=(B,),
        in_specs=[pl.BlockSpec((1, S, 512), lambda b, s: (b, 0, 0)),
                  pl.BlockSpec((1, S, 128), lambda b, s: (b, 0, 0)),
                  pl.BlockSpec((1, S, 128), lambda b, s: (b, 0, 0)),
                  pl.BlockSpec((N, LANE), lambda b, s: (0, 0)),
                  pl.BlockSpec((N, LANE), lambda b, s: (0, 0)),
                  pl.BlockSpec((1, LANE), lambda b, s: (0, 0)),
                  pl.BlockSpec((1, LANE), lambda b, s: (0, 0)),
                  pl.BlockSpec((LANE, LANE), lambda b, s: (0, 0))],
        out_specs=pl.BlockSpec((1, S, 512), lambda b, s: (b, 0, 0)),
        scratch_shapes=[pltpu.VMEM((ATT_HEADS, S, HEAD_DIM), BF16),
                        pltpu.VMEM((ATT_KV_HEADS, S + 2 * ATT_BLOCK, HEAD_DIM), BF16),
                        pltpu.VMEM((ATT_KV_HEADS, S + 2 * ATT_BLOCK, HEAD_DIM), BF16)])
    return pl.pallas_call(
        kern, grid_spec=grid_spec,
        out_shape=jax.ShapeDtypeStruct((B, S, 512), BF16),
        compiler_params=_cparams(("parallel",)),
        name="attention",
    )(sink, q, k, v, cos, sin, qg, kg, bd)


def _conv_kernel(glu_ref, dww_ref, dwb_ref, lng_ref, lnb_ref, pww_ref, pwb_ref, o_ref, hp, rot, *, LC, N):
    C = GROUP_W
    half = CONV_K // 2
    rows = CONV_ROWS

    def stream(s0, ln):
        hp[0:CONV_PAD, :] = jnp.zeros((CONV_PAD, C), F32)
        hp[CONV_PAD + ln:2 * CONV_PAD + ln, :] = jnp.zeros((CONV_PAD, C), F32)

        def fill(i, carry):
            r0 = pl.multiple_of(i * TM, TM)
            g = glu_ref[0, pl.ds(s0 + r0, TM), :]
            a = g[:, :C].astype(F32)
            gate = g[:, C:].astype(F32)
            hp[pl.ds(CONV_PAD + r0, TM), :] = a * jax.nn.sigmoid(gate)
            return carry

        lax.fori_loop(0, ln // TM, fill, 0)

        def chunk(i, carry):
            r0 = pl.multiple_of(i * rows, rows)
            parts = []
            for cb in range(C // LANE):
                win = hp[pl.ds(r0, rows + 2 * CONV_PAD), cb * LANE:(cb + 1) * LANE]
                for ph in range(8):
                    rot[cb, ph] = win[ph:ph + rot.shape[2]]
                acc = jnp.zeros((rows, LANE), F32)
                for t in range(CONV_K):
                    off = CONV_PAD - half + t
                    w = dww_ref[t:t + 1, cb * LANE:(cb + 1) * LANE]
                    acc = acc + w * rot[cb, off % 8, (off // 8) * 8:(off // 8) * 8 + rows, :]
                parts.append(acc)
            y = jnp.concatenate(parts, axis=1) + dwb_ref[...]
            mu = jnp.mean(y, axis=-1, keepdims=True)
            yc = y - mu
            var = jnp.mean(yc * yc, axis=-1, keepdims=True)
            z = yc * lax.rsqrt(var + EPS) * lng_ref[...] + lnb_ref[...]
            z = _silu(z)
            out = _dot(z.astype(BF16), pww_ref[...]) + pwb_ref[...]
            o_ref[0, pl.ds(s0 + r0, rows), :] = out.astype(o_ref.dtype)
            return carry

        lax.fori_loop(0, ln // rows, chunk, 0)

    stream(0, LC)
    stream(LC, N)


def _conformer(glu, dww, dwb, lng, lnb, pww, pwb, LC, N):
    B, S, _ = glu.shape
    C = GROUP_W
    kern = functools.partial(_conv_kernel, LC=LC, N=N)
    vec = pl.BlockSpec((1, C), lambda b: (0, 0))
    return pl.pallas_call(
        kern, grid=(B,),
        in_specs=[pl.BlockSpec((1, S, 2 * C), lambda b: (b, 0, 0)),
                  pl.BlockSpec((32, C), lambda b: (0, 0)),
                  vec, vec, vec,
                  pl.BlockSpec((C, C), lambda b: (0, 0)),
                  vec],
        out_specs=pl.BlockSpec((1, S, C), lambda b: (b, 0, 0)),
        out_shape=jax.ShapeDtypeStruct((B, S, C), BF16),
        scratch_shapes=[pltpu.VMEM((max(LC, N) + 2 * CONV_PAD, C), F32),
                        pltpu.VMEM((C // LANE, 8, CONV_ROWS + 2 * CONV_PAD - 8, LANE), F32)],
        compiler_params=_cparams(("parallel",)),
        name="conformer_conv",
    )(glu, dww, dwb, lng, lnb, pww, pwb)


def _ssd_kernel(xbc_ref, dt_ref, z_ref, cw_ref, cb_ref, arow_ref, dtb_ref, dsk_ref, ng_ref, o_ref,
                xp, xs, bm, cm, dts, yacc, state, *, LC, N):
    S = LC + N
    Q = SSD_CHUNK
    nc = S // Q
    nc_ctx = LC // Q
    ii = lax.broadcasted_iota(jnp.int32, (Q, Q), 0)
    jj = lax.broadcasted_iota(jnp.int32, (Q, Q), 1)
    lower = ii >= jj
    tri = (jnp.where(lower, 1.0, 0.0).astype(BF16), jnp.where(jj >= ii, 1.0, 0.0).astype(BF16))
    causal = (lower, jj >= ii)
    first_half = lax.broadcasted_iota(jnp.int32, (Q, LANE), 1) < SSD_STATE // 2

    def stream(s0, ln):
        xp[0:SSD_PAD, :] = jnp.zeros((SSD_PAD, SSD_XBC), F32)
        xp[SSD_PAD + ln:2 * SSD_PAD + ln, :] = jnp.zeros((SSD_PAD, SSD_XBC), F32)

        def fill(i, carry):
            r0 = pl.multiple_of(i * Q, Q)
            xp[pl.ds(SSD_PAD + r0, Q), :] = xbc_ref[0, pl.ds(s0 + r0, Q), :].astype(F32)
            return carry

        lax.fori_loop(0, ln // Q, fill, 0)

        def conv(i, carry):
            r0 = pl.multiple_of(i * Q, Q)
            for cb in range(SSD_XBC // LANE):
                cs = slice(cb * LANE, (cb + 1) * LANE)
                win = xp[pl.ds(r0, Q + 2 * SSD_PAD), cs]
                acc = jnp.zeros((Q, LANE), F32)
                for t in range(SSD_CONV):
                    acc = acc + cw_ref[t:t + 1, cs] * win[SSD_PAD - 2 + t:SSD_PAD - 2 + t + Q]
                y = _silu(acc + cb_ref[:, cs])
                if cb < SSD_INNER // LANE:
                    xs[pl.ds(s0 + r0, Q), cs] = y
                elif cb < (SSD_INNER + SSD_GROUPS * SSD_STATE) // LANE:
                    c2 = cb - SSD_INNER // LANE
                    bm[pl.ds(s0 + r0, Q), c2 * LANE:(c2 + 1) * LANE] = y
                else:
                    c2 = cb - (SSD_INNER + SSD_GROUPS * SSD_STATE) // LANE
                    cm[pl.ds(s0 + r0, Q), c2 * LANE:(c2 + 1) * LANE] = y
            return carry

        lax.fori_loop(0, ln // Q, conv, 0)

    stream(0, LC)
    stream(LC, N)

    def softplus_rows(i, carry):
        r0 = pl.multiple_of(i * Q, Q)
        v = dt_ref[0, pl.ds(r0, Q), :] + dtb_ref[...]
        dts[pl.ds(r0, Q), :] = jnp.maximum(v, 0.0) + jnp.log(1.0 + jnp.exp(-jnp.abs(v)))
        return carry

    lax.fori_loop(0, nc, softplus_rows, 0)

    def chunk_step(c, d):
        r0 = pl.multiple_of(c * Q, Q)
        dtc = dts[pl.ds(r0, Q), :]
        a1, a2, a3 = _split3(dtc * arow_ref[...])
        cum = _dot(tri[d], a1) + _dot(tri[d], a2) + _dot(tri[d], a3)
        cum_t = cum.T
        dt_t = dtc.T
        tot = cum[Q - 1:Q, :] if d == 0 else cum[0:1, :]
        e_cum = jnp.exp(cum)
        e_tot = jnp.exp(tot)
        w_state = jnp.exp(tot - cum) * dtc
        for g in range(SSD_GROUPS):
            bg = bm[pl.ds(r0, Q), g * SSD_STATE:(g + 1) * SSD_STATE]
            cg = cm[pl.ds(r0, Q), g * SSD_STATE:(g + 1) * SSD_STATE].astype(BF16)
            bg_t = bg.T.astype(BF16)
            cb_mat = _dot(cg, bg_t)
            for pr in range(SSD_HEADS // SSD_GROUPS // 2):
                pair = g * 2 + pr
                lanes = slice(pair * LANE, (pair + 1) * LANE)
                xpair = xs[pl.ds(r0, Q), lanes]
                y = None
                for sub in range(2):
                    hd = d * SSD_HEADS + pair * 2 + sub
                    seg = cum[:, hd:hd + 1] - cum_t[hd:hd + 1, :]
                    dec = jnp.where(causal[d], jnp.exp(jnp.minimum(seg, 0.0)), 0.0)
                    w = (cb_mat * dec * dt_t[hd:hd + 1, :]).astype(BF16)
                    xm = jnp.where(first_half if sub == 0 else ~first_half, xpair, 0.0).astype(BF16)
                    t = _dot(w, xm)
                    y = t if y is None else y + t
                h0 = d * SSD_HEADS + pair * 2
                sel = lambda m: jnp.where(first_half, m[:, h0:h0 + 1], m[:, h0 + 1:h0 + 2])
                st = state[d * (SSD_HEADS // 2) + pair]
                y = y + _dot(cg, st.astype(BF16)) * sel(e_cum)
                xw = (xpair * sel(w_state)).astype(BF16)
                state[d * (SSD_HEADS // 2) + pair] = st * sel(e_tot) + _dot(bg_t, xw)
                if d == 0:
                    yacc[pl.ds(r0, Q), lanes] = y
                else:
                    yacc[pl.ds(r0, Q), lanes] = yacc[pl.ds(r0, Q), lanes] + y

    state[...] = jnp.zeros(state.shape, F32)

    def fwd(c, carry):
        chunk_step(c, 0)
        return carry

    lax.fori_loop(0, nc, fwd, 0)

    def bwd_ctx(i, carry):
        chunk_step(nc_ctx - 1 - i, 1)
        return carry

    lax.fori_loop(0, nc_ctx, bwd_ctx, 0)

    def bwd_lat(i, carry):
        chunk_step(nc - 1 - i, 1)
        return carry

    lax.fori_loop(0, nc - nc_ctx, bwd_lat, 0)

    def gate_out(i, carry):
        r0 = pl.multiple_of(i * Q, Q)
        y = yacc[pl.ds(r0, Q), :] + xs[pl.ds(r0, Q), :] * dsk_ref[...]
        y = y * _silu(z_ref[0, pl.ds(r0, Q), :].astype(F32))
        ms = jnp.mean(y * y, axis=-1, keepdims=True)
        o_ref[0, pl.ds(r0, Q), :] = (y * lax.rsqrt(ms + EPS) * ng_ref[...]).astype(o_ref.dtype)
        return carry

    lax.fori_loop(0, nc, gate_out, 0)


def _ssd(xbc, dt, z, cw, cb, arow, dtb, dsk, ng, LC, N):
    B, S, _ = xbc.shape
    kern = functools.partial(_ssd_kernel, LC=LC, N=N)
    row = lambda w: pl.BlockSpec((1, w), lambda b: (0, 0))
    return pl.pallas_call(
        kern, grid=(B,),
        in_specs=[pl.BlockSpec((1, S, SSD_XBC), lambda b: (b, 0, 0)),
                  pl.BlockSpec((1, S, LANE), lambda b: (b, 0, 0)),
                  pl.BlockSpec((1, S, SSD_INNER), lambda b: (b, 0, 0)),
                  pl.BlockSpec((SSD_CONV, SSD_XBC), lambda b: (0, 0)),
                  row(SSD_XBC), row(LANE), row(LANE), row(SSD_INNER), row(SSD_INNER)],
        out_specs=pl.BlockSpec((1, S, SSD_INNER), lambda b: (b, 0, 0)),
        out_shape=jax.ShapeDtypeStruct((B, S, SSD_INNER), BF16),
        scratch_shapes=[pltpu.VMEM((max(LC, N) + 2 * SSD_PAD, SSD_XBC), F32),
                        pltpu.VMEM((S, SSD_INNER), F32),
                        pltpu.VMEM((S, SSD_GROUPS * SSD_STATE), F32),
                        pltpu.VMEM((S, SSD_GROUPS * SSD_STATE), F32),
                        pltpu.VMEM((S, LANE), F32),
                        pltpu.VMEM((S, SSD_INNER), F32),
                        pltpu.VMEM((SSD_HEADS, SSD_STATE, LANE), F32)],
        compiler_params=_cparams(("parallel",)),
        name="ssd",
    )(xbc, dt, z, cw, cb, arow, dtb, dsk, ng)


def _pool_kernel(pin_ref, pw_ref, ps_ref, o_ref, up, *, LC, N):
    C = GROUP_W
    rows = TM

    def stream(s0, ln):
        up[0:POOL_PAD, :] = jnp.zeros((POOL_PAD, C), F32)
        up[POOL_PAD + ln:2 * POOL_PAD + ln, :] = jnp.zeros((POOL_PAD, C), F32)

        def fill(i, carry):
            r0 = pl.multiple_of(i * rows, rows)
            up[pl.ds(POOL_PAD + r0, rows), :] = pin_ref[0, pl.ds(s0 + r0, rows), :].astype(F32)
            return carry

        lax.fori_loop(0, ln // rows, fill, 0)

        def chunk(i, carry):
            r0 = pl.multiple_of(i * rows, rows)
            t = r0 + lax.broadcasted_iota(jnp.int32, (rows, 1), 0)
            outs = []
            for gi, w in enumerate(POOL_SIZES):
                cs = slice(gi * POOL_CH, (gi + 1) * POOL_CH)
                win = up[pl.ds(r0, rows + 2 * POOL_PAD), cs]
                acc = jnp.zeros((rows, POOL_CH), F32)
                for d in range(-(w // 2), w - w // 2):
                    acc = acc + win[POOL_PAD + d:POOL_PAD + d + rows]
                cnt = jnp.minimum(t + (w - w // 2), ln) - jnp.maximum(t - w // 2, 0)
                p = acc / cnt.astype(F32) - win[POOL_PAD:POOL_PAD + rows]
                outs.append(_dot(p.astype(BF16), pw_ref[gi]))
            y = jnp.concatenate(outs, axis=1) * ps_ref[...]
            o_ref[0, pl.ds(s0 + r0, rows), :] = y.astype(o_ref.dtype)
            return carry

        lax.fori_loop(0, ln // rows, chunk, 0)

    stream(0, LC)
    stream(LC, N)


def _pool(pin, pw, ps, LC, N):
    B, S, C = pin.shape
    kern = functools.partial(_pool_kernel, LC=LC, N=N)
    return pl.pallas_call(
        kern, grid=(B,),
        in_specs=[pl.BlockSpec((1, S, C), lambda b: (b, 0, 0)),
                  pl.BlockSpec((len(POOL_SIZES), POOL_CH, POOL_CH), lambda b: (0, 0, 0)),
                  pl.BlockSpec((1, C), lambda b: (0, 0))],
        out_specs=pl.BlockSpec((1, S, C), lambda b: (b, 0, 0)),
        out_shape=jax.ShapeDtypeStruct((B, S, C), BF16),
        scratch_shapes=[pltpu.VMEM((max(LC, N) + 2 * POOL_PAD, C), F32)],
        compiler_params=_cparams(("parallel",)),
        name="pool_mixer",
    )(pin, pw, ps)


def _pack_halves(x):
    w = x.shape[1] // 2
    u = lax.bitcast_convert_type(x.astype(BF16).astype(F32), jnp.uint32)
    return (u[:, :w] >> 16) | (u[:, w:] & jnp.uint32(0xFFFF0000))


def _unpack_halves(p):
    lo = lax.bitcast_convert_type(p << 16, F32)
    hi = lax.bitcast_convert_type(p & jnp.uint32(0xFFFF0000), F32)
    return lo, hi


def _outproj_kernel(ya_ref, yb_ref, yc_ref, yd_ref, x_ref, mod_ref, g_ref, w_ref, wr_ref,
                    xo_ref, h2_ref, lg_ref):
    acc = None
    for i, ref in enumerate((ya_ref, yb_ref, yc_ref, yd_ref)):
        t = _dot(ref[...], w_ref[0, i * GROUP_W:(i + 1) * GROUP_W, :])
        acc = t if acc is None else acc + t
    m = mod_ref[0]
    x = x_ref[...] + m[2:3] * acc
    xo_ref[...] = x
    ms = jnp.mean(x * x, axis=-1, keepdims=True)
    h2 = x * lax.rsqrt(ms + EPS) * g_ref[...]
    h2 = h2 * (1.0 + m[4:5]) + m[3:4]
    h2_ref[...] = _pack_halves(h2)
    h_hi, h_lo = _split2(h2)
    both = _dot(h_hi, wr_ref[0])
    lg_ref[...] = both[:, :LANE] + (both[:, LANE:] + _dot(h_lo, wr_ref[0, :, :LANE]))


def _out_projection(ys, X2, modl, g2, w_out, wr, layer, tps, tiles_ctx, latent_only=False):
    T, D = X2.shape
    if latent_only:
        tl = tps - tiles_ctx
        steps = (T // TM) // tps * tl
        src = lambda i: ((i // tl) * tps + tiles_ctx + i % tl, 0)
        mrow = lambda i: ((i // tl) * 2 + 1, 0, 0)
        aliases = {}
    else:
        steps = T // TM
        src = lambda i: (i, 0)
        mrow = lambda i: (_mod_row(i, tps, tiles_ctx), 0, 0)
        aliases = {4: 0}
    rows = steps * TM
    ytile = pl.BlockSpec((TM, GROUP_W), src)
    return pl.pallas_call(
        _outproj_kernel,
        grid=(steps,),
        in_specs=[ytile, ytile, ytile, ytile,
                  pl.BlockSpec((TM, D), src),
                  pl.BlockSpec((1, 6, D), mrow),
                  _resident((1, D)),
                  _resident((4 * GROUP_W, D), layer),
                  _resident((D, 2 * LANE), layer)],
        out_specs=[pl.BlockSpec((TM, D), lambda i: (i, 0)),
                   pl.BlockSpec((TM, D // 2), lambda i: (i, 0)),
                   pl.BlockSpec((TM, LANE), lambda i: (i, 0))],
        out_shape=[jax.ShapeDtypeStruct((rows, D), F32),
                   jax.ShapeDtypeStruct((rows, D // 2), jnp.uint32),
                   jax.ShapeDtypeStruct((rows, LANE), F32)],
        input_output_aliases=aliases,
        compiler_params=_cparams(("parallel",)),
        name="out_projection",
    )(*ys, X2, modl, g2, w_out, wr)


META_E = 0
META_R = 2
META_W = 4


def _route_kernel(lg_ref, meta_ref, cnt_ref, carry):
    @pl.when(pl.program_id(0) == 0)
    def _():
        carry[...] = jnp.zeros(carry.shape, F32)

    lg = lg_ref[...]
    lane = lax.broadcasted_iota(jnp.int32, lg.shape, 1).astype(F32)
    big = 1e9
    rmax = lambda m: jnp.max(jnp.where(m, lg, NEG), axis=-1, keepdims=True)
    first = lambda m: jnp.min(jnp.where(m, lane, big), axis=-1, keepdims=True)

    gm = lane < MOE_GROUPS
    gmax = rmax(gm)
    gidx = first(gm & (lg == gmax))
    g_w = 1.0 / jnp.sum(jnp.where(gm, jnp.exp(lg - gmax), 0.0), axis=-1, keepdims=True)
    lo = MOE_GROUPS + MOE_PER_GROUP * gidx
    em = (lane >= lo) & (lane < lo + MOE_PER_GROUP)
    v1 = rmax(em)
    i1 = first(em & (lg == v1))
    em2 = em & (lane != i1)
    v2 = rmax(em2)
    i2 = first(em2 & (lg == v2))
    t = jnp.exp(v2 - v1)
    w1 = g_w / (1.0 + t)
    w2 = g_w * t / (1.0 + t)

    oh1 = jnp.where(lane == i1, 1.0, 0.0)
    oh2 = jnp.where(lane == i2, 1.0, 0.0)
    oh = oh1 + oh2
    rows = lg.shape[0]
    ri = lax.broadcasted_iota(jnp.int32, (rows, rows), 0)
    ci = lax.broadcasted_iota(jnp.int32, (rows, rows), 1)
    before = jnp.where(ci < ri, 1.0, 0.0).astype(BF16)
    base = _dot(before, oh.astype(BF16)) + carry[0:1, :]
    r1 = jnp.sum(oh1 * base, axis=-1, keepdims=True)
    r2 = jnp.sum(oh2 * base, axis=-1, keepdims=True)
    carry[0:1, :] = carry[0:1, :] + jnp.sum(oh, axis=0, keepdims=True)

    meta = jnp.zeros(lg.shape, F32)
    for k, val in enumerate((i1 - MOE_GROUPS, i2 - MOE_GROUPS, r1, r2, w1, w2)):
        meta = jnp.where(lane == k, val, meta)
    meta_ref[...] = meta
    cnt_ref[...] = jnp.broadcast_to(carry[0:1, :], cnt_ref.shape)


def _route(logits, n_tiles):
    T = logits.shape[0]
    meta, cnt = pl.pallas_call(
        _route_kernel,
        grid=(T // TM,),
        in_specs=[pl.BlockSpec((TM, LANE), lambda i: (i, 0))],
        out_specs=[pl.BlockSpec((TM, LANE), lambda i: (i, 0)),
                   pl.BlockSpec((8, LANE), lambda i: (0, 0))],
        out_shape=[jax.ShapeDtypeStruct((T, LANE), F32), jax.ShapeDtypeStruct((8, LANE), F32)],
        scratch_shapes=[pltpu.VMEM((8, LANE), F32)],
        compiler_params=_cparams(("arbitrary",)),
        name="moe_route",
    )(logits)
    idx = meta[:, :4].astype(jnp.int32).reshape(T // TM, TM, 4).transpose(0, 2, 1)
    counts = cnt[0, MOE_GROUPS:MOE_GROUPS + MOE_EXPERTS].astype(jnp.int32)
    ntile = (counts + TM - 1) // TM
    tile_end = jnp.cumsum(ntile)
    tile_start = (tile_end - ntile).astype(jnp.int32)
    total = tile_end[-1]
    tiles = jnp.arange(n_tiles, dtype=jnp.int32)
    active = tiles < total
    last = jnp.minimum(tiles, total - 1)
    t_exp = jnp.sum((last[:, None] >= tile_end[None, :]).astype(jnp.int32), axis=1)
    t_first = (active & (tiles == jnp.sum(jnp.where(t_exp[:, None] == jnp.arange(MOE_EXPERTS)[None, :],
                                                     tile_start[None, :], 0), axis=1))).astype(jnp.int32)
    return meta, idx, tile_start, t_exp, t_first, active.astype(jnp.int32)


def _row_dest(ts_ref, idx_ref, r, slot):
    return ts_ref[idx_ref[0, META_E + slot, r]] * TM + idx_ref[0, META_R + slot, r]


def _dispatch_kernel(ts_ref, idx_ref, h_ref, xs_in_ref, xs_ref, sem):
    del xs_in_ref

    def row_copy(r, dst):
        return pltpu.make_async_copy(h_ref.at[pl.ds(r, 1)], xs_ref.at[pl.ds(dst, 1)], sem)

    def start(r, carry):
        for slot in range(2):
            row_copy(r, _row_dest(ts_ref, idx_ref, r, slot)).start(priority=slot)
        return carry

    lax.fori_loop(0, TM, start, 0, unroll=8)
    for _ in range(2 * TM):
        row_copy(0, 0).wait()


def _moe_dispatch(h2p, idx, tile_start, n_tiles):
    T, W = h2p.shape
    grid_spec = pltpu.PrefetchScalarGridSpec(
        num_scalar_prefetch=1,
        grid=(T // TM,),
        in_specs=[pl.BlockSpec((1, 4, TM), lambda i, ts: (i, 0, 0), memory_space=pltpu.SMEM),
                  pl.BlockSpec((TM, W), lambda i, ts: (i, 0)),
                  pl.BlockSpec(memory_space=pl.ANY)],
        out_specs=pl.BlockSpec(memory_space=pl.ANY),
        scratch_shapes=[pltpu.SemaphoreType.DMA(())])
    return pl.pallas_call(
        _dispatch_kernel, grid_spec=grid_spec,
        out_shape=jax.ShapeDtypeStruct((n_tiles * TM, W), jnp.uint32),
        input_output_aliases={3: 0},
        compiler_params=_cparams(("arbitrary",)),
        name="moe_dispatch",
    )(tile_start, idx, h2p, jnp.zeros((n_tiles * TM, W), jnp.uint32))


def _moe_kernel(te_ref, tf_ref, tv_ref, x_ref, wg_ref, wu_ref, wd_ref, o_ref, wg_b, wu_b, wd_b):
    i = pl.program_id(0)

    @pl.when(tf_ref[i] == 1)
    def _():
        wg_b[...] = wg_ref[0, 0].astype(BF16)
        wu_b[...] = wu_ref[0, 0].astype(BF16)
        wd_b[...] = wd_ref[0, 0].astype(BF16)

    @pl.when(tv_ref[i] == 1)
    def _():
        lo, hi = _unpack_halves(x_ref[...])
        lo = lo.astype(BF16)
        hi = hi.astype(BF16)
        half = lo.shape[1]
        g = _dot(lo, wg_b[:half, :]) + _dot(hi, wg_b[half:, :])
        u = _dot(lo, wu_b[:half, :]) + _dot(hi, wu_b[half:, :])
        hid = (_silu(g) * u).astype(BF16)
        o_ref[...] = _pack_halves(_dot(hid, wd_b[...]))

    @pl.when(tv_ref[i] == 0)
    def _():
        o_ref[...] = jnp.zeros(o_ref.shape, o_ref.dtype)


def _moe_experts(xs, t_exp, t_first, t_active, w_gate, w_up, w_down, layer):
    R, W = xs.shape
    D = 2 * W
    n_tiles = R // TM
    wmap = lambda i, te, tf, tv: (layer, te[i], 0, 0)
    grid_spec = pltpu.PrefetchScalarGridSpec(
        num_scalar_prefetch=3,
        grid=(n_tiles,),
        in_specs=[pl.BlockSpec((TM, W), lambda i, te, tf, tv: (i, 0)),
                  pl.BlockSpec((1, 1, D, D_EXPERT), wmap),
                  pl.BlockSpec((1, 1, D, D_EXPERT), wmap),
                  pl.BlockSpec((1, 1, D_EXPERT, D), wmap)],
        out_specs=pl.BlockSpec((TM, W), lambda i, te, tf, tv: (i, 0)),
        scratch_shapes=[pltpu.VMEM((D, D_EXPERT), BF16),
                        pltpu.VMEM((D, D_EXPERT), BF16),
                        pltpu.VMEM((D_EXPERT, D), BF16)])
    return pl.pallas_call(
        _moe_kernel, grid_spec=grid_spec,
        out_shape=jax.ShapeDtypeStruct((R, W), jnp.uint32),
        compiler_params=_cparams(("arbitrary",)),
        name="moe_experts",
    )(t_exp, t_first, t_active, xs, w_gate, w_up, w_down)


def _combine_kernel(ts_ref, idx_ref, idx_next_ref, x_ref, meta_ref, mod_ref, ye_ref, o_ref, ybuf, sems):
    i = pl.program_id(0)
    n = pl.num_programs(0)
    cur = i % 2

    def row_copy(src, buf, slot, r):
        return pltpu.make_async_copy(ye_ref.at[pl.ds(src, 1)], ybuf.at[buf, slot, pl.ds(r, 1)], sems.at[buf])

    def gather(iref, buf):
        def start(r, carry):
            for slot in range(2):
                row_copy(_row_dest(ts_ref, iref, r, slot), buf, slot, r).start(priority=slot)
            return carry

        lax.fori_loop(0, TM, start, 0, unroll=8)

    @pl.when(i == 0)
    def _():
        gather(idx_ref, 0)

    @pl.when(i + 1 < n)
    def _():
        gather(idx_next_ref, 1 - cur)

    for _ in range(2 * TM):
        row_copy(0, cur, 0, 0).wait()

    half = x_ref.shape[1] // 2
    meta = meta_ref[...]
    w1 = meta[:, META_W:META_W + 1]
    w2 = meta[:, META_W + 1:META_W + 2]
    lo1, hi1 = _unpack_halves(ybuf[cur, 0])
    lo2, hi2 = _unpack_halves(ybuf[cur, 1])
    g2 = mod_ref[0][5:6]
    o_ref[:, :half] = x_ref[:, :half] + g2[:, :half] * (w1 * lo1 + w2 * lo2)
    o_ref[:, half:] = x_ref[:, half:] + g2[:, half:] * (w1 * hi1 + w2 * hi2)


def _moe_combine(X2, ye, meta, idx, tile_start, modl, tps, tiles_ctx):
    T, D = X2.shape
    steps = T // TM
    tile = pl.BlockSpec((TM, D), lambda i, ts: (i, 0))
    grid_spec = pltpu.PrefetchScalarGridSpec(
        num_scalar_prefetch=1,
        grid=(steps,),
        in_specs=[pl.BlockSpec((1, 4, TM), lambda i, ts: (i, 0, 0), memory_space=pltpu.SMEM),
                  pl.BlockSpec((1, 4, TM), lambda i, ts: (jnp.minimum(i + 1, steps - 1), 0, 0),
                               memory_space=pltpu.SMEM),
                  tile,
                  pl.BlockSpec((TM, LANE), lambda i, ts: (i, 0)),
                  pl.BlockSpec((1, 6, D), lambda i, ts: (_mod_row(i, tps, tiles_ctx), 0, 0)),
                  pl.BlockSpec(memory_space=pl.ANY)],
        out_specs=tile,
        scratch_shapes=[pltpu.VMEM((2, 2, TM, D // 2), jnp.uint32),
                        pltpu.SemaphoreType.DMA((2,))])
    return pl.pallas_call(
        _combine_kernel, grid_spec=grid_spec,
        out_shape=jax.ShapeDtypeStruct((T, D), F32),
        input_output_aliases={3: 0},
        compiler_params=_cparams(("arbitrary",)),
        name="moe_combine",
    )(tile_start, idx, idx, X2, meta, modl, ye)


def _rope_tables(n):
    rows = n // GRID_W
    row = jnp.repeat(jnp.arange(rows, dtype=F32), GRID_W)
    col = jnp.tile(jnp.arange(GRID_W, dtype=F32), rows)
    half = HEAD_DIM // 2
    inv_freq = 1.0 / (ROPE_THETA ** (jnp.arange(0, half, 2, dtype=F32) / half))
    ar = row[:, None] * inv_freq
    ac = col[:, None] * inv_freq
    cos = jnp.concatenate([jnp.cos(ar), jnp.cos(ar), jnp.cos(ac), jnp.cos(ac)], axis=-1)
    sin = jnp.concatenate([-jnp.sin(ar), jnp.sin(ar), -jnp.sin(ac), jnp.sin(ac)], axis=-1)
    return jnp.tile(cos, (1, 2)), jnp.tile(sin, (1, 2))


def kernel(x, c, ctx, c_ctx, norm1_g, norm2_g, w_mod, b_mod, w_in, w_out, q_norm_g, k_norm_g, attn_sink,
           conv_dw_w, conv_dw_b, conv_ln_g, conv_ln_b, conv_pw_w, conv_pw_b, ssd_conv_w, ssd_conv_b, ssd_a_log,
           ssd_dt_bias, ssd_d, ssd_norm_g, pool_w, pool_scale, moe_group_router, moe_expert_router, moe_w_gate,
           moe_w_up, moe_w_down):
    B, N, D = x.shape
    LC = ctx.shape[1]
    S = LC + N
    L = w_mod.shape[0]
    assert LC % TM == 0 and N % TM == 0 and N % GRID_W == 0
    tps = S // TM
    tiles_ctx = LC // TM
    T = B * S

    rows = -(-(B + 1) // 8) * 8
    cc = jnp.concatenate([c, c_ctx[None, :], jnp.zeros((rows - B - 1, D), F32)], axis=0)
    mod = _modulation(cc, w_mod, b_mod).reshape(L, rows, 6, D)

    w_in_p = jnp.concatenate([w_in[:, :, :DT_SRC + 16], jnp.zeros((L, D, LANE - 16), F32), w_in[:, :, DT_SRC + 16:]],
                             axis=-1).astype(BF16)
    w_out_b = w_out.astype(BF16)
    w_router = jnp.concatenate([moe_group_router, moe_expert_router,
                                jnp.zeros((L, D, LANE - MOE_GROUPS - MOE_EXPERTS), F32)], axis=-1)
    wr_hi = w_router.astype(BF16)
    wr = jnp.concatenate([wr_hi, (w_router - wr_hi.astype(F32)).astype(BF16)], axis=-1)

    cos, sin = _rope_tables(N)
    bd = (jnp.arange(LANE)[:, None] // HEAD_DIM == jnp.arange(LANE)[None, :] // HEAD_DIM).astype(BF16)
    pad_lane = lambda v: jnp.concatenate([v.reshape(-1), jnp.zeros((LANE - v.size,), F32)]).reshape(1, LANE)

    X = jnp.concatenate([ctx, x], axis=1).reshape(T, D)
    for l in range(L):
        modl = jnp.stack([jnp.broadcast_to(mod[l, B], (B, 6, D)), mod[l, :B]], axis=1).reshape(2 * B, 6, D)
        q, k, v, glu, z, xbc, dt, pin = _in_projection(X, modl, norm1_g[l][None], w_in_p, l, tps, tiles_ctx)
        r3 = lambda a: a.reshape(B, S, a.shape[-1])
        y_att = _attention(r3(q), r3(k), r3(v), cos, sin, jnp.tile(q_norm_g[l], 2)[None], jnp.tile(k_norm_g[l], 2)[None],
                           bd, attn_sink[l], LC, N)
        dww = jnp.concatenate([conv_dw_w[l], jnp.zeros((32 - CONV_K, GROUP_W), F32)], axis=0)
        y_conv = _conformer(r3(glu), dww, conv_dw_b[l][None], conv_ln_g[l][None], conv_ln_b[l][None],
                            conv_pw_w[l].astype(BF16), conv_pw_b[l][None], LC, N)
        y_ssd = _ssd(r3(xbc), r3(dt), r3(z), ssd_conv_w[l], ssd_conv_b[l][None], pad_lane(-jnp.exp(ssd_a_log[l])),
                     pad_lane(ssd_dt_bias[l]), jnp.repeat(ssd_d[l], HEAD_DIM)[None], ssd_norm_g[l][None], LC, N)
        y_pool = _pool(r3(pin), pool_w[l].astype(BF16), pool_scale[l][None], LC, N)
        ys = [a.reshape(T, GROUP_W) for a in (y_att, y_conv, y_ssd, y_pool)]
        last = l == L - 1
        X, h2p, logits = _out_projection(ys, X, modl, norm2_g[l][None], w_out_b, wr, l, tps, tiles_ctx,
                                         latent_only=last)
        n_tiles = (2 * X.shape[0]) // TM + MOE_EXPERTS
        meta, idx, tile_start, t_exp, t_first, t_active = _route(logits, n_tiles)
        xs = _moe_dispatch(h2p, idx, tile_start, n_tiles)
        ye = _moe_experts(xs, t_exp, t_first, t_active, moe_w_gate, moe_w_up, moe_w_down, l)
        if last:
            X = _moe_combine(X, ye, meta, idx, tile_start, modl, tps - tiles_ctx, 0)
        else:
            X = _moe_combine(X, ye, meta, idx, tile_start, modl, tps, tiles_ctx)
    return X.reshape(B, N, D)
```

```python
import functools
import math

import jax
import jax.numpy as jnp
from jax import lax
from jax.experimental import pallas as pl
from jax.experimental.pallas import tpu as pltpu

F32 = jnp.float32
BF16 = jnp.bfloat16

EPS = 1e-6
GRID_W = 64
HEAD_DIM = 64
ATT_HEADS = 8
ATT_KV_HEADS = 2
ATT_REP = ATT_HEADS // ATT_KV_HEADS
ATT_BLOCK = 128
ATT_WINDOW = 128
ROPE_THETA = 10000.0
GROUP_W = 512
CONV_K = 31
CONV_PAD = 16
CONV_ROWS = 128
SSD_HEADS = 8
SSD_GROUPS = 2
SSD_STATE = 128
SSD_CHUNK = 128
SSD_CONV = 4
SSD_PAD = 8
SSD_INNER = 512
SSD_XBC = SSD_INNER + 2 * SSD_GROUPS * SSD_STATE
POOL_SIZES = (2, 4, 8, 16)
POOL_CH = 128
POOL_PAD = 8
MOE_GROUPS = 4
MOE_PER_GROUP = 8
MOE_EXPERTS = 32
D_EXPERT = 512

TM = 256
LANE = 128
NEG = -1e30
VMEM_LIMIT = 56 * 1024 * 1024

IN_COLS = (("q", 0, 512), ("k", 512, 128), ("v", 640, 128), ("glu", 768, 1024), ("z", 1792, 512),
           ("xbc", 2304, 1024), ("dt", 3328, 128), ("pin", 3456, 512))
D_IN_PAD = 3968
DT_SRC = 3328


def _cparams(sem):
    return pltpu.CompilerParams(dimension_semantics=sem, vmem_limit_bytes=VMEM_LIMIT)


def _resident(shape, layer=None):
    nd = len(shape)
    if layer is None:
        return pl.BlockSpec(shape, lambda *_: (0,) * nd, pipeline_mode=pl.Buffered(1))
    return pl.BlockSpec((1,) + tuple(shape), lambda *_: (layer,) + (0,) * nd, pipeline_mode=pl.Buffered(1))


def _dot(a, b):
    return jnp.dot(a, b, preferred_element_type=F32)


def _dot_nt(a, b):
    return lax.dot_general(a, b, (((1,), (1,)), ((), ())), preferred_element_type=F32)


def _split2(x):
    hi = x.astype(BF16)
    lo = (x - hi.astype(F32)).astype(BF16)
    return hi, lo


def _split3(x):
    h1 = x.astype(BF16)
    r1 = x - h1.astype(F32)
    h2 = r1.astype(BF16)
    h3 = (r1 - h2.astype(F32)).astype(BF16)
    return h1, h2, h3


def _silu(x):
    return x * jax.nn.sigmoid(x)


def _mod_kernel(c_ref, w_ref, b_ref, o_ref):
    s = _silu(c_ref[...])
    o_ref[0] = _dot(s.astype(BF16), w_ref[0].astype(BF16)) + b_ref[0]


def _modulation(cc, w_mod, b_mod):
    L, D, D6 = w_mod.shape
    R = cc.shape[0]
    tn = 1024
    return pl.pallas_call(
        _mod_kernel,
        grid=(L, D6 // tn),
        in_specs=[pl.BlockSpec((R, D), lambda l, j: (0, 0)),
                  pl.BlockSpec((1, D, tn), lambda l, j: (l, 0, j)),
                  pl.BlockSpec((1, 1, tn), lambda l, j: (l, 0, j))],
        out_specs=pl.BlockSpec((1, R, tn), lambda l, j: (l, 0, j)),
        out_shape=jax.ShapeDtypeStruct((L, R, D6), F32),
        compiler_params=_cparams(("arbitrary", "arbitrary")),
        name="modulation",
    )(cc, w_mod, b_mod.reshape(L, 1, D6))


def _inproj_kernel(x_ref, mod_ref, g_ref, w_ref, *out_refs):
    x = x_ref[...]
    m = mod_ref[0]
    ms = jnp.mean(x * x, axis=-1, keepdims=True)
    h = x * lax.rsqrt(ms + EPS) * g_ref[...]
    h = h * (1.0 + m[1:2]) + m[0:1]
    hb = h.astype(BF16)
    for ref, (_, a, w) in zip(out_refs, IN_COLS):
        ref[...] = _dot(hb, w_ref[0, :, a:a + w]).astype(ref.dtype)


def _mod_row(i, tps, tiles_ctx):
    return (i // tps) * 2 + jnp.where((i % tps) >= tiles_ctx, 1, 0)


def _in_projection(X2, modl, g1, w_in_p, layer, tps, tiles_ctx):
    T, D = X2.shape
    dts = {"dt": F32}
    out_shape = [jax.ShapeDtypeStruct((T, w), dts.get(n, BF16)) for n, _, w in IN_COLS]
    out_specs = [pl.BlockSpec((TM, w), lambda i: (i, 0)) for _, _, w in IN_COLS]
    return pl.pallas_call(
        _inproj_kernel,
        grid=(T // TM,),
        in_specs=[pl.BlockSpec((TM, D), lambda i: (i, 0)),
                  pl.BlockSpec((1, 6, D), lambda i: (_mod_row(i, tps, tiles_ctx), 0, 0)),
                  _resident((1, D)),
                  _resident((D, D_IN_PAD), layer)],
        out_specs=out_specs,
        out_shape=out_shape,
        compiler_params=_cparams(("parallel",)),
        name="in_projection",
    )(X2, modl, g1, w_in_p)


def _norm_heads(x, g, bd):
    hi, lo = _split2(x * x)
    s = _dot(hi, bd) + _dot(lo, bd)
    return x * lax.rsqrt(s * (1.0 / HEAD_DIM) + EPS) * g


def _rope(x, cos, sin):
    lane = lax.broadcasted_iota(jnp.int32, x.shape, 1)
    sw = jnp.where((lane & 31) < 16, pltpu.roll(x, LANE - 16, 1), pltpu.roll(x, 16, 1))
    return x * cos + sw * sin


def _sink_softmax_pv(parts, sink_col):
    m = sink_col
    for s, _ in parts:
        m = jnp.maximum(m, jnp.max(s, axis=-1, keepdims=True))
    den = jnp.exp(sink_col - m)
    o = None
    for s, v in parts:
        p = jnp.exp(s - m)
        den = den + jnp.sum(p, axis=-1, keepdims=True)
        pv = _dot(p.astype(BF16), v)
        o = pv if o is None else o + pv
    return o / den


def _attn_kernel(sink_ref, q_ref, k_ref, v_ref, cos_ref, sin_ref, qg_ref, kg_ref, bd_ref, o_ref,
                 qh, kh, vh, *, LC, N):
    S = LC + N
    koff = LC + ATT_BLOCK
    bd = bd_ref[...]
    scale = HEAD_DIM ** -0.5
    zpad = jnp.zeros((ATT_BLOCK, HEAD_DIM), BF16)
    for h in range(ATT_KV_HEADS):
        for buf in (kh, vh):
            buf[h, LC:koff, :] = zpad
            buf[h, koff + N:koff + N + ATT_BLOCK, :] = zpad

    rc = TM
    for c0 in range(0, S, rc):
        lat = c0 >= LC
        dst = c0 + ATT_BLOCK if lat else c0
        if lat:
            cos = cos_ref[c0 - LC:c0 - LC + rc, :]
            sin = sin_ref[c0 - LC:c0 - LC + rc, :]
        kn = _norm_heads(k_ref[0, c0:c0 + rc, :].astype(F32), kg_ref[...], bd)
        if lat:
            kn = _rope(kn, cos, sin)
        knb = kn.astype(BF16)
        vv = v_ref[0, c0:c0 + rc, :]
        for h in range(ATT_KV_HEADS):
            kh[h, dst:dst + rc, :] = knb[:, h * HEAD_DIM:(h + 1) * HEAD_DIM]
            vh[h, dst:dst + rc, :] = vv[:, h * HEAD_DIM:(h + 1) * HEAD_DIM]
        for cb in range(ATT_HEADS // 2):
            qn = _norm_heads(q_ref[0, c0:c0 + rc, cb * LANE:(cb + 1) * LANE].astype(F32), qg_ref[...], bd)
            if lat:
                qn = _rope(qn, cos, sin)
            qnb = (qn * scale).astype(BF16)
            qh[2 * cb, c0:c0 + rc, :] = qnb[:, :HEAD_DIM]
            qh[2 * cb + 1, c0:c0 + rc, :] = qnb[:, HEAD_DIM:]

    def sink_column(h, rows):
        grp = lax.broadcasted_iota(jnp.int32, (ATT_REP * rows, 1), 0) // rows
        col = jnp.full((ATT_REP * rows, 1), sink_ref[ATT_REP * h], F32)
        for r in range(1, ATT_REP):
            col = jnp.where(grp == r, sink_ref[ATT_REP * h + r], col)
        return col

    def store_heads(o, h, row0, rows):
        for pr in range(ATT_REP // 2):
            pair = jnp.concatenate([o[(2 * pr) * rows:(2 * pr + 1) * rows],
                                    o[(2 * pr + 1) * rows:(2 * pr + 2) * rows]], axis=1)
            cb = (ATT_REP // 2) * h + pr
            o_ref[0, pl.ds(row0, rows), cb * LANE:(cb + 1) * LANE] = pair.astype(o_ref.dtype)

    for h in range(ATT_KV_HEADS):
        q4 = jnp.concatenate([qh[ATT_REP * h + r, 0:LC, :] for r in range(ATT_REP)], axis=0)
        kc = kh[h, 0:LC, :]
        vc = vh[h, 0:LC, :]
        o = _sink_softmax_pv([(_dot_nt(q4, kc), vc)], sink_column(h, LC))
        store_heads(o, h, 0, LC)

    nk = 3 * ATT_BLOCK
    qi = lax.broadcasted_iota(jnp.int32, (ATT_REP * ATT_BLOCK, nk), 0) & (ATT_BLOCK - 1)
    kj = lax.broadcasted_iota(jnp.int32, (ATT_REP * ATT_BLOCK, nk), 1)
    rel = kj - ATT_BLOCK - qi
    in_window = (rel <= ATT_WINDOW) & (rel >= -ATT_WINDOW)

    def block(j, carry):
        r0 = pl.multiple_of(j * ATT_BLOCK, ATT_BLOCK)
        kpos = kj + (j - 1) * ATT_BLOCK
        ok = in_window & (kpos >= 0) & (kpos < N)
        for h in range(ATT_KV_HEADS):
            q4 = jnp.concatenate([qh[ATT_REP * h + r, pl.ds(LC + r0, ATT_BLOCK), :] for r in range(ATT_REP)],
                                 axis=0)
            kb = kh[h, pl.ds(LC + r0, nk), :]
            vb = vh[h, pl.ds(LC + r0, nk), :]
            s_lat = jnp.where(ok, _dot_nt(q4, kb), NEG)
            s_ctx = _dot_nt(q4, kh[h, 0:LC, :])
            o = _sink_softmax_pv([(s_lat, vb), (s_ctx, vh[h, 0:LC, :])], sink_column(h, ATT_BLOCK))
            store_heads(o, h, LC + r0, ATT_BLOCK)
        return carry

    lax.fori_loop(0, N // ATT_BLOCK, block, 0)


def _attention(q, k, v, cos, sin, qg, kg, bd, sink, LC, N):
    B, S, _ = q.shape
    kern = functools.partial(_attn_kernel, LC=LC, N=N)
    grid_spec = pltpu.PrefetchScalarGridSpec(
        num_scalar_prefetch=1,
        grid=(B,),
        in_specs=[pl.BlockSpec((1, S, 512), lambda b, s: (b, 0, 0)),
                  pl.BlockSpec((1, S, 128), lambda b, s: (b, 0, 0)),
                  pl.BlockSpec((1, S, 128), lambda b, s: (b, 0, 0)),
                  pl.BlockSpec((N, LANE), lambda b, s: (0, 0)),
                  pl.BlockSpec((N, LANE), lambda b, s: (0, 0)),
                  pl.BlockSpec((1, LANE), lambda b, s: (0, 0)),
                  pl.BlockSpec((1, LANE), lambda b, s: (0, 0)),
                  pl.BlockSpec((LANE, LANE), lambda b, s: (0, 0))],
        out_specs=pl.BlockSpec((1, S, 512), lambda b, s: (b, 0, 0)),
        scratch_shapes=[pltpu.VMEM((ATT_HEADS, S, HEAD_DIM), BF16),
                        pltpu.VMEM((ATT_KV_HEADS, S + 2 * ATT_BLOCK, HEAD_DIM), BF16),
                        pltpu.VMEM((ATT_KV_HEADS, S + 2 * ATT_BLOCK, HEAD_DIM), BF16)])
    return pl.pallas_call(
        kern, grid_spec=grid_spec,
        out_shape=jax.ShapeDtypeStruct((B, S, 512), BF16),
        compiler_params=_cparams(("parallel",)),
        name="attention",
    )(sink, q, k, v, cos, sin, qg, kg, bd)


def _conv_kernel(glu_ref, dww_ref, dwb_ref, lng_ref, lnb_ref, pww_ref, pwb_ref, o_ref, hp, rot, *, LC, N):
    C = GROUP_W
    half = CONV_K // 2
    rows = CONV_ROWS

    def stream(s0, ln):
        hp[0:CONV_PAD, :] = jnp.zeros((CONV_PAD, C), F32)
        hp[CONV_PAD + ln:2 * CONV_PAD + ln, :] = jnp.zeros((CONV_PAD, C), F32)

        def fill(i, carry):
            r0 = pl.multiple_of(i * TM, TM)
            g = glu_ref[0, pl.ds(s0 + r0, TM), :]
            a = g[:, :C].astype(F32)
            gate = g[:, C:].astype(F32)
            hp[pl.ds(CONV_PAD + r0, TM), :] = a * jax.nn.sigmoid(gate)
            return carry

        lax.fori_loop(0, ln // TM, fill, 0)

        def chunk(i, carry):
            r0 = pl.multiple_of(i * rows, rows)
            parts = []
            for cb in range(C // LANE):
                win = hp[pl.ds(r0, rows + 2 * CONV_PAD), cb * LANE:(cb + 1) * LANE]
                for ph in range(8):
                    rot[cb, ph] = win[ph:ph + rot.shape[2]]
                acc = jnp.zeros((rows, LANE), F32)
                for t in range(CONV_K):
                    off = CONV_PAD - half + t
                    w = dww_ref[t:t + 1, cb * LANE:(cb + 1) * LANE]
                    acc = acc + w * rot[cb, off % 8, (off // 8) * 8:(off // 8) * 8 + rows, :]
                parts.append(acc)
            y = jnp.concatenate(parts, axis=1) + dwb_ref[...]
            mu = jnp.mean(y, axis=-1, keepdims=True)
            yc = y - mu
            var = jnp.mean(yc * yc, axis=-1, keepdims=True)
            z = yc * lax.rsqrt(var + EPS) * lng_ref[...] + lnb_ref[...]
            z = _silu(z)
            out = _dot(z.astype(BF16), pww_ref[...]) + pwb_ref[...]
            o_ref[0, pl.ds(s0 + r0, rows), :] = out.astype(o_ref.dtype)
            return carry

        lax.fori_loop(0, ln // rows, chunk, 0)

    stream(0, LC)
    stream(LC, N)


def _conformer(glu, dww, dwb, lng, lnb, pww, pwb, LC, N):
    B, S, _ = glu.shape
    C = GROUP_W
    kern = functools.partial(_conv_kernel, LC=LC, N=N)
    vec = pl.BlockSpec((1, C), lambda b: (0, 0))
    return pl.pallas_call(
        kern, grid=(B,),
        in_specs=[pl.BlockSpec((1, S, 2 * C), lambda b: (b, 0, 0)),
                  pl.BlockSpec((32, C), lambda b: (0, 0)),
                  vec, vec, vec,
                  pl.BlockSpec((C, C), lambda b: (0, 0)),
                  vec],
        out_specs=pl.BlockSpec((1, S, C), lambda b: (b, 0, 0)),
        out_shape=jax.ShapeDtypeStruct((B, S, C), BF16),
        scratch_shapes=[pltpu.VMEM((max(LC, N) + 2 * CONV_PAD, C), F32),
                        pltpu.VMEM((C // LANE, 8, CONV_ROWS + 2 * CONV_PAD - 8, LANE), F32)],
        compiler_params=_cparams(("parallel",)),
        name="conformer_conv",
    )(glu, dww, dwb, lng, lnb, pww, pwb)


def _ssd_kernel(xbc_ref, dt_ref, z_ref, cw_ref, cb_ref, arow_ref, dtb_ref, dsk_ref, ng_ref, o_ref,
                xp, xs, bm, cm, dts, yacc, state, *, LC, N):
    S = LC + N
    Q = SSD_CHUNK
    nc = S // Q
    nc_ctx = LC // Q
    ii = lax.broadcasted_iota(jnp.int32, (Q, Q), 0)
    jj = lax.broadcasted_iota(jnp.int32, (Q, Q), 1)
    lower = ii >= jj
    tri = (jnp.where(lower, 1.0, 0.0).astype(BF16), jnp.where(jj >= ii, 1.0, 0.0).astype(BF16))
    causal = (lower, jj >= ii)
    first_half = lax.broadcasted_iota(jnp.int32, (Q, LANE), 1) < SSD_STATE // 2

    def stream(s0, ln):
        xp[0:SSD_PAD, :] = jnp.zeros((SSD_PAD, SSD_XBC), F32)
        xp[SSD_PAD + ln:2 * SSD_PAD + ln, :] = jnp.zeros((SSD_PAD, SSD_XBC), F32)

        def fill(i, carry):
            r0 = pl.multiple_of(i * Q, Q)
            xp[pl.ds(SSD_PAD + r0, Q), :] = xbc_ref[0, pl.ds(s0 + r0, Q), :].astype(F32)
            return carry

        lax.fori_loop(0, ln // Q, fill, 0)

        def conv(i, carry):
            r0 = pl.multiple_of(i * Q, Q)
            for cb in range(SSD_XBC // LANE):
                cs = slice(cb * LANE, (cb + 1) * LANE)
                win = xp[pl.ds(r0, Q + 2 * SSD_PAD), cs]
                acc = jnp.zeros((Q, LANE), F32)
                for t in range(SSD_CONV):
                    acc = acc + cw_ref[t:t + 1, cs] * win[SSD_PAD - 2 + t:SSD_PAD - 2 + t + Q]
                y = _silu(acc + cb_ref[:, cs])
                if cb < SSD_INNER // LANE:
                    xs[pl.ds(s0 + r0, Q), cs] = y
                elif cb < (SSD_INNER + SSD_GROUPS * SSD_STATE) // LANE:
                    c2 = cb - SSD_INNER // LANE
                    bm[pl.ds(s0 + r0, Q), c2 * LANE:(c2 + 1) * LANE] = y
                else:
                    c2 = cb - (SSD_INNER + SSD_GROUPS * SSD_STATE) // LANE
                    cm[pl.ds(s0 + r0, Q), c2 * LANE:(c2 + 1) * LANE] = y
            return carry

        lax.fori_loop(0, ln // Q, conv, 0)

    stream(0, LC)
    stream(LC, N)

    def softplus_rows(i, carry):
        r0 = pl.multiple_of(i * Q, Q)
        v = dt_ref[0, pl.ds(r0, Q), :] + dtb_ref[...]
        dts[pl.ds(r0, Q), :] = jnp.maximum(v, 0.0) + jnp.log(1.0 + jnp.exp(-jnp.abs(v)))
        return carry

    lax.fori_loop(0, nc, softplus_rows, 0)

    def chunk_step(c, d):
        r0 = pl.multiple_of(c * Q, Q)
        dtc = dts[pl.ds(r0, Q), :]
        a1, a2, a3 = _split3(dtc * arow_ref[...])
        cum = _dot(tri[d], a1) + _dot(tri[d], a2) + _dot(tri[d], a3)
        cum_t = cum.T
        dt_t = dtc.T
        tot = cum[Q - 1:Q, :] if d == 0 else cum[0:1, :]
        e_cum = jnp.exp(cum)
        e_tot = jnp.exp(tot)
        w_state = jnp.exp(tot - cum) * dtc
        for g in range(SSD_GROUPS):
            bg = bm[pl.ds(r0, Q), g * SSD_STATE:(g + 1) * SSD_STATE]
            cg = cm[pl.ds(r0, Q), g * SSD_STATE:(g + 1) * SSD_STATE].astype(BF16)
            bg_t = bg.T.astype(BF16)
            cb_mat = _dot(cg, bg_t)
            for pr in range(SSD_HEADS // SSD_GROUPS // 2):
                pair = g * 2 + pr
                lanes = slice(pair * LANE, (pair + 1) * LANE)
                xpair = xs[pl.ds(r0, Q), lanes]
                y = None
                for sub in range(2):
                    hd = d * SSD_HEADS + pair * 2 + sub
                    seg = cum[:, hd:hd + 1] - cum_t[hd:hd + 1, :]
                    dec = jnp.where(causal[d], jnp.exp(jnp.minimum(seg, 0.0)), 0.0)
                    w = (cb_mat * dec * dt_t[hd:hd + 1, :]).astype(BF16)
                    xm = jnp.where(first_half if sub == 0 else ~first_half, xpair, 0.0).astype(BF16)
                    t = _dot(w, xm)
                    y = t if y is None else y + t
                h0 = d * SSD_HEADS + pair * 2
                sel = lambda m: jnp.where(first_half, m[:, h0:h0 + 1], m[:, h0 + 1:h0 + 2])
                st = state[d * (SSD_HEADS // 2) + pair]
                y = y + _dot(cg, st.astype(BF16)) * sel(e_cum)
                xw = (xpair * sel(w_state)).astype(BF16)
                state[d * (SSD_HEADS // 2) + pair] = st * sel(e_tot) + _dot(bg_t, xw)
                if d == 0:
                    yacc[pl.ds(r0, Q), lanes] = y
                else:
                    yacc[pl.ds(r0, Q), lanes] = yacc[pl.ds(r0, Q), lanes] + y

    state[...] = jnp.zeros(state.shape, F32)

    def fwd(c, carry):
        chunk_step(c, 0)
        return carry

    lax.fori_loop(0, nc, fwd, 0)

    def bwd_ctx(i, carry):
        chunk_step(nc_ctx - 1 - i, 1)
        return carry

    lax.fori_loop(0, nc_ctx, bwd_ctx, 0)

    def bwd_lat(i, carry):
        chunk_step(nc - 1 - i, 1)
        return carry

    lax.fori_loop(0, nc - nc_ctx, bwd_lat, 0)

    def gate_out(i, carry):
        r0 = pl.multiple_of(i * Q, Q)
        y = yacc[pl.ds(r0, Q), :] + xs[pl.ds(r0, Q), :] * dsk_ref[...]
        y = y * _silu(z_ref[0, pl.ds(r0, Q), :].astype(F32))
        ms = jnp.mean(y * y, axis=-1, keepdims=True)
        o_ref[0, pl.ds(r0, Q), :] = (y * lax.rsqrt(ms + EPS) * ng_ref[...]).astype(o_ref.dtype)
        return carry

    lax.fori_loop(0, nc, gate_out, 0)


def _ssd(xbc, dt, z, cw, cb, arow, dtb, dsk, ng, LC, N):
    B, S, _ = xbc.shape
    kern = functools.partial(_ssd_kernel, LC=LC, N=N)
    row = lambda w: pl.BlockSpec((1, w), lambda b: (0, 0))
    return pl.pallas_call(
        kern, grid=(B,),
        in_specs=[pl.BlockSpec((1, S, SSD_XBC), lambda b: (b, 0, 0)),
                  pl.BlockSpec((1, S, LANE), lambda b: (b, 0, 0)),
                  pl.BlockSpec((1, S, SSD_INNER), lambda b: (b, 0, 0)),
                  pl.BlockSpec((SSD_CONV, SSD_XBC), lambda b: (0, 0)),
                  row(SSD_XBC), row(LANE), row(LANE), row(SSD_INNER), row(SSD_INNER)],
        out_specs=pl.BlockSpec((1, S, SSD_INNER), lambda b: (b, 0, 0)),
        out_shape=jax.ShapeDtypeStruct((B, S, SSD_INNER), BF16),
        scratch_shapes=[pltpu.VMEM((max(LC, N) + 2 * SSD_PAD, SSD_XBC), F32),
                        pltpu.VMEM((S, SSD_INNER), F32),
                        pltpu.VMEM((S, SSD_GROUPS * SSD_STATE), F32),
                        pltpu.VMEM((S, SSD_GROUPS * SSD_STATE), F32),
                        pltpu.VMEM((S, LANE), F32),
                        pltpu.VMEM((S, SSD_INNER), F32),
                        pltpu.VMEM((SSD_HEADS, SSD_STATE, LANE), F32)],
        compiler_params=_cparams(("parallel",)),
        name="ssd",
    )(xbc, dt, z, cw, cb, arow, dtb, dsk, ng)


def _pool_kernel(pin_ref, pw_ref, ps_ref, o_ref, up, *, LC, N):
    C = GROUP_W
    rows = TM

    def stream(s0, ln):
        up[0:POOL_PAD, :] = jnp.zeros((POOL_PAD, C), F32)
        up[POOL_PAD + ln:2 * POOL_PAD + ln, :] = jnp.zeros((POOL_PAD, C), F32)

        def fill(i, carry):
            r0 = pl.multiple_of(i * rows, rows)
            up[pl.ds(POOL_PAD + r0, rows), :] = pin_ref[0, pl.ds(s0 + r0, rows), :].astype(F32)
            return carry

        lax.fori_loop(0, ln // rows, fill, 0)

        def chunk(i, carry):
            r0 = pl.multiple_of(i * rows, rows)
            t = r0 + lax.broadcasted_iota(jnp.int32, (rows, 1), 0)
            outs = []
            for gi, w in enumerate(POOL_SIZES):
                cs = slice(gi * POOL_CH, (gi + 1) * POOL_CH)
                win = up[pl.ds(r0, rows + 2 * POOL_PAD), cs]
                acc = jnp.zeros((rows, POOL_CH), F32)
                for d in range(-(w // 2), w - w // 2):
                    acc = acc + win[POOL_PAD + d:POOL_PAD + d + rows]
                cnt = jnp.minimum(t + (w - w // 2), ln) - jnp.maximum(t - w // 2, 0)
                p = acc / cnt.astype(F32) - win[POOL_PAD:POOL_PAD + rows]
                outs.append(_dot(p.astype(BF16), pw_ref[gi]))
            y = jnp.concatenate(outs, axis=1) * ps_ref[...]
            o_ref[0, pl.ds(s0 + r0, rows), :] = y.astype(o_ref.dtype)
            return carry

        lax.fori_loop(0, ln // rows, chunk, 0)

    stream(0, LC)
    stream(LC, N)


def _pool(pin, pw, ps, LC, N):
    B, S, C = pin.shape
    kern = functools.partial(_pool_kernel, LC=LC, N=N)
    return pl.pallas_call(
        kern, grid=(B,),
        in_specs=[pl.BlockSpec((1, S, C), lambda b: (b, 0, 0)),
                  pl.BlockSpec((len(POOL_SIZES), POOL_CH, POOL_CH), lambda b: (0, 0, 0)),
                  pl.BlockSpec((1, C), lambda b: (0, 0))],
        out_specs=pl.BlockSpec((1, S, C), lambda b: (b, 0, 0)),
        out_shape=jax.ShapeDtypeStruct((B, S, C), BF16),
        scratch_shapes=[pltpu.VMEM((max(LC, N) + 2 * POOL_PAD, C), F32)],
        compiler_params=_cparams(("parallel",)),
        name="pool_mixer",
    )(pin, pw, ps)


def _pack_halves(x):
    w = x.shape[1] // 2
    u = lax.bitcast_convert_type(x.astype(BF16).astype(F32), jnp.uint32)
    return (u[:, :w] >> 16) | (u[:, w:] & jnp.uint32(0xFFFF0000))


def _unpack_halves(p):
    lo = lax.bitcast_convert_type(p << 16, F32)
    hi = lax.bitcast_convert_type(p & jnp.uint32(0xFFFF0000), F32)
    return lo, hi


SUB = 8
ROW_WORDS = SUB * LANE


def _store_token_tiles(ref, packed):
    rows = packed.shape[0]
    for s in range(SUB):
        ref[pl.ds(s, rows, stride=SUB), :] = packed[:, s * LANE:(s + 1) * LANE]


def _load_token_tiles(ref, rows, s):
    return ref[pl.ds(s, rows, stride=SUB), :]


def _outproj_kernel(ya_ref, yb_ref, yc_ref, yd_ref, x_ref, mod_ref, g_ref, w_ref, wr_ref,
                    xo_ref, h2_ref, lg_ref):
    acc = None
    for i, ref in enumerate((ya_ref, yb_ref, yc_ref, yd_ref)):
        t = _dot(ref[...], w_ref[0, i * GROUP_W:(i + 1) * GROUP_W, :])
        acc = t if acc is None else acc + t
    m = mod_ref[0]
    x = x_ref[...] + m[2:3] * acc
    xo_ref[...] = x
    ms = jnp.mean(x * x, axis=-1, keepdims=True)
    h2 = x * lax.rsqrt(ms + EPS) * g_ref[...]
    h2 = h2 * (1.0 + m[4:5]) + m[3:4]
    _store_token_tiles(h2_ref, _pack_halves(h2))
    h_hi, h_lo = _split2(h2)
    both = _dot(h_hi, wr_ref[0])
    lg_ref[...] = both[:, :LANE] + (both[:, LANE:] + _dot(h_lo, wr_ref[0, :, :LANE]))


def _out_projection(ys, X2, modl, g2, w_out, wr, layer, tps, tiles_ctx, latent_only=False):
    T, D = X2.shape
    if latent_only:
        tl = tps - tiles_ctx
        steps = (T // TM) // tps * tl
        src = lambda i: ((i // tl) * tps + tiles_ctx + i % tl, 0)
        mrow = lambda i: ((i // tl) * 2 + 1, 0, 0)
        aliases = {}
    else:
        steps = T // TM
        src = lambda i: (i, 0)
        mrow = lambda i: (_mod_row(i, tps, tiles_ctx), 0, 0)
        aliases = {4: 0}
    rows = steps * TM
    ytile = pl.BlockSpec((TM, GROUP_W), src)
    return pl.pallas_call(
        _outproj_kernel,
        grid=(steps,),
        in_specs=[ytile, ytile, ytile, ytile,
                  pl.BlockSpec((TM, D), src),
                  pl.BlockSpec((1, 6, D), mrow),
                  _resident((1, D)),
                  _resident((4 * GROUP_W, D), layer),
                  _resident((D, 2 * LANE), layer)],
        out_specs=[pl.BlockSpec((TM, D), lambda i: (i, 0)),
                   pl.BlockSpec((TM * SUB, LANE), lambda i: (i, 0)),
                   pl.BlockSpec((TM, LANE), lambda i: (i, 0))],
        out_shape=[jax.ShapeDtypeStruct((rows, D), F32),
                   jax.ShapeDtypeStruct((rows * SUB, LANE), jnp.uint32),
                   jax.ShapeDtypeStruct((rows, LANE), F32)],
        input_output_aliases=aliases,
        compiler_params=_cparams(("parallel",)),
        name="out_projection",
    )(*ys, X2, modl, g2, w_out, wr)


META_E = 0
META_R = 2
META_W = 4


def _route_kernel(lg_ref, meta_ref, cnt_ref, carry):
    @pl.when(pl.program_id(0) == 0)
    def _():
        carry[...] = jnp.zeros(carry.shape, F32)

    lg = lg_ref[...]
    lane = lax.broadcasted_iota(jnp.int32, lg.shape, 1).astype(F32)
    big = 1e9
    rmax = lambda m: jnp.max(jnp.where(m, lg, NEG), axis=-1, keepdims=True)
    first = lambda m: jnp.min(jnp.where(m, lane, big), axis=-1, keepdims=True)

    gm = lane < MOE_GROUPS
    gmax = rmax(gm)
    gidx = first(gm & (lg == gmax))
    g_w = 1.0 / jnp.sum(jnp.where(gm, jnp.exp(lg - gmax), 0.0), axis=-1, keepdims=True)
    lo = MOE_GROUPS + MOE_PER_GROUP * gidx
    em = (lane >= lo) & (lane < lo + MOE_PER_GROUP)
    v1 = rmax(em)
    i1 = first(em & (lg == v1))
    em2 = em & (lane != i1)
    v2 = rmax(em2)
    i2 = first(em2 & (lg == v2))
    t = jnp.exp(v2 - v1)
    w1 = g_w / (1.0 + t)
    w2 = g_w * t / (1.0 + t)

    oh1 = jnp.where(lane == i1, 1.0, 0.0)
    oh2 = jnp.where(lane == i2, 1.0, 0.0)
    oh = oh1 + oh2
    rows = lg.shape[0]
    ri = lax.broadcasted_iota(jnp.int32, (rows, rows), 0)
    ci = lax.broadcasted_iota(jnp.int32, (rows, rows), 1)
    before = jnp.where(ci < ri, 1.0, 0.0).astype(BF16)
    base = _dot(before, oh.astype(BF16)) + carry[0:1, :]
    r1 = jnp.sum(oh1 * base, axis=-1, keepdims=True)
    r2 = jnp.sum(oh2 * base, axis=-1, keepdims=True)
    carry[0:1, :] = carry[0:1, :] + jnp.sum(oh, axis=0, keepdims=True)

    meta = jnp.zeros(lg.shape, F32)
    for k, val in enumerate((i1 - MOE_GROUPS, i2 - MOE_GROUPS, r1, r2, w1, w2)):
        meta = jnp.where(lane == k, val, meta)
    meta_ref[...] = meta
    cnt_ref[...] = jnp.broadcast_to(carry[0:1, :], cnt_ref.shape)


def _route(logits, n_tiles):
    T = logits.shape[0]
    rt = max(r for r in (4 * TM, 2 * TM, TM) if T % r == 0)
    meta, cnt = pl.pallas_call(
        _route_kernel,
        grid=(T // rt,),
        in_specs=[pl.BlockSpec((rt, LANE), lambda i: (i, 0))],
        out_specs=[pl.BlockSpec((rt, LANE), lambda i: (i, 0)),
                   pl.BlockSpec((8, LANE), lambda i: (0, 0))],
        out_shape=[jax.ShapeDtypeStruct((T, LANE), F32), jax.ShapeDtypeStruct((8, LANE), F32)],
        scratch_shapes=[pltpu.VMEM((8, LANE), F32)],
        compiler_params=_cparams(("arbitrary",)),
        name="moe_route",
    )(logits)
    idx = meta[:, :4].astype(jnp.int32).reshape(T // TM, TM, 4).transpose(0, 2, 1)
    counts = cnt[0, MOE_GROUPS:MOE_GROUPS + MOE_EXPERTS].astype(jnp.int32)
    ntile = (counts + TM - 1) // TM
    tile_end = jnp.cumsum(ntile)
    tile_start = (tile_end - ntile).astype(jnp.int32)
    total = tile_end[-1]
    tiles = jnp.arange(n_tiles, dtype=jnp.int32)
    active = tiles < total
    last = jnp.minimum(tiles, total - 1)
    t_exp = jnp.sum((last[:, None] >= tile_end[None, :]).astype(jnp.int32), axis=1)
    t_first = (active & (tiles == jnp.sum(jnp.where(t_exp[:, None] == jnp.arange(MOE_EXPERTS)[None, :],
                                                     tile_start[None, :], 0), axis=1))).astype(jnp.int32)
    return meta, idx, tile_start, t_exp, t_first, active.astype(jnp.int32)


def _row_dest(ts_ref, idx_ref, r, slot):
    return ts_ref[idx_ref[0, META_E + slot, r]] * TM + idx_ref[0, META_R + slot, r]


def _dispatch_kernel(ts_ref, idx_ref, h_ref, xs_in_ref, xs_ref, sem):
    del xs_in_ref

    def row_copy(r, dst):
        return pltpu.make_async_copy(h_ref.at[pl.ds(pl.multiple_of(r * SUB, SUB), SUB)],
                                     xs_ref.at[pl.ds(pl.multiple_of(dst * SUB, SUB), SUB)], sem)

    def start(r, carry):
        for slot in range(2):
            row_copy(r, _row_dest(ts_ref, idx_ref, r, slot)).start()
        return carry

    lax.fori_loop(0, TM, start, 0, unroll=8)
    for _ in range(2 * TM):
        row_copy(0, 0).wait()


def _moe_dispatch(h2p, idx, tile_start, n_tiles):
    T = h2p.shape[0] // SUB
    grid_spec = pltpu.PrefetchScalarGridSpec(
        num_scalar_prefetch=1,
        grid=(T // TM,),
        in_specs=[pl.BlockSpec((1, 4, TM), lambda i, ts: (i, 0, 0), memory_space=pltpu.SMEM),
                  pl.BlockSpec((TM * SUB, LANE), lambda i, ts: (i, 0)),
                  pl.BlockSpec(memory_space=pl.ANY)],
        out_specs=pl.BlockSpec(memory_space=pl.ANY),
        scratch_shapes=[pltpu.SemaphoreType.DMA(())])
    return pl.pallas_call(
        _dispatch_kernel, grid_spec=grid_spec,
        out_shape=jax.ShapeDtypeStruct((n_tiles * TM * SUB, LANE), jnp.uint32),
        input_output_aliases={3: 0},
        compiler_params=_cparams(("arbitrary",)),
        name="moe_dispatch",
    )(tile_start, idx, h2p, jnp.zeros((n_tiles * TM * SUB, LANE), jnp.uint32))


def _moe_kernel(te_ref, tf_ref, tv_ref, x_ref, wg_ref, wu_ref, wd_ref, o_ref, wg_b, wu_b, wd_b):
    i = pl.program_id(0)

    @pl.when(tf_ref[i] == 1)
    def _():
        wg_b[...] = wg_ref[0, 0].astype(BF16)
        wu_b[...] = wu_ref[0, 0].astype(BF16)
        wd_b[...] = wd_ref[0, 0].astype(BF16)

    @pl.when(tv_ref[i] == 1)
    def _():
        words = jnp.concatenate([_load_token_tiles(x_ref, TM, s) for s in range(SUB)], axis=1)
        lo, hi = _unpack_halves(words)
        lo = lo.astype(BF16)
        hi = hi.astype(BF16)
        half = lo.shape[1]
        g = _dot(lo, wg_b[:half, :]) + _dot(hi, wg_b[half:, :])
        u = _dot(lo, wu_b[:half, :]) + _dot(hi, wu_b[half:, :])
        hid = (_silu(g) * u).astype(BF16)
        _store_token_tiles(o_ref, _pack_halves(_dot(hid, wd_b[...])))

    @pl.when(tv_ref[i] == 0)
    def _():
        o_ref[...] = jnp.zeros(o_ref.shape, o_ref.dtype)


def _moe_experts(xs, t_exp, t_first, t_active, w_gate, w_up, w_down, layer):
    R = xs.shape[0] // SUB
    D = 2 * ROW_WORDS
    n_tiles = R // TM
    wmap = lambda i, te, tf, tv: (layer, te[i], 0, 0)
    grid_spec = pltpu.PrefetchScalarGridSpec(
        num_scalar_prefetch=3,
        grid=(n_tiles,),
        in_specs=[pl.BlockSpec((TM * SUB, LANE), lambda i, te, tf, tv: (i, 0)),
                  pl.BlockSpec((1, 1, D, D_EXPERT), wmap),
                  pl.BlockSpec((1, 1, D, D_EXPERT), wmap),
                  pl.BlockSpec((1, 1, D_EXPERT, D), wmap)],
        out_specs=pl.BlockSpec((TM * SUB, LANE), lambda i, te, tf, tv: (i, 0)),
        scratch_shapes=[pltpu.VMEM((D, D_EXPERT), BF16),
                        pltpu.VMEM((D, D_EXPERT), BF16),
                        pltpu.VMEM((D_EXPERT, D), BF16)])
    return pl.pallas_call(
        _moe_kernel, grid_spec=grid_spec,
        out_shape=jax.ShapeDtypeStruct((R * SUB, LANE), jnp.uint32),
        compiler_params=_cparams(("arbitrary",)),
        name="moe_experts",
    )(t_exp, t_first, t_active, xs, w_gate, w_up, w_down)


def _combine_kernel(ts_ref, idx_ref, idx_next_ref, x_ref, meta_ref, mod_ref, ye_ref, o_ref, ybuf, sems):
    i = pl.program_id(0)
    n = pl.num_programs(0)
    cur = i % 2

    def row_copy(src, buf, slot, r):
        return pltpu.make_async_copy(ye_ref.at[pl.ds(pl.multiple_of(src * SUB, SUB), SUB)],
                                     ybuf.at[buf, slot, pl.ds(pl.multiple_of(r * SUB, SUB), SUB)], sems.at[buf])

    def gather(iref, buf):
        def start(r, carry):
            for slot in range(2):
                row_copy(_row_dest(ts_ref, iref, r, slot), buf, slot, r).start()
            return carry

        lax.fori_loop(0, TM, start, 0, unroll=8)

    @pl.when(i == 0)
    def _():
        gather(idx_ref, 0)

    @pl.when(i + 1 < n)
    def _():
        gather(idx_next_ref, 1 - cur)

    for _ in range(2 * TM):
        row_copy(0, cur, 0, 0).wait()

    half = x_ref.shape[1] // 2
    meta = meta_ref[...]
    w1 = meta[:, META_W:META_W + 1]
    w2 = meta[:, META_W + 1:META_W + 2]
    g2 = mod_ref[0][5:6]
    for s in range(SUB):
        lo1, hi1 = _unpack_halves(_load_token_tiles(ybuf.at[cur, 0], TM, s))
        lo2, hi2 = _unpack_halves(_load_token_tiles(ybuf.at[cur, 1], TM, s))
        for base, a, b in ((s * LANE, lo1, lo2), (half + s * LANE, hi1, hi2)):
            cols = slice(base, base + LANE)
            o_ref[:, cols] = x_ref[:, cols] + g2[:, cols] * (w1 * a + w2 * b)


def _moe_combine(X2, ye, meta, idx, tile_start, modl, tps, tiles_ctx):
    T, D = X2.shape
    steps = T // TM
    tile = pl.BlockSpec((TM, D), lambda i, ts: (i, 0))
    grid_spec = pltpu.PrefetchScalarGridSpec(
        num_scalar_prefetch=1,
        grid=(steps,),
        in_specs=[pl.BlockSpec((1, 4, TM), lambda i, ts: (i, 0, 0), memory_space=pltpu.SMEM),
                  pl.BlockSpec((1, 4, TM), lambda i, ts: (jnp.minimum(i + 1, steps - 1), 0, 0),
                               memory_space=pltpu.SMEM),
                  tile,
                  pl.BlockSpec((TM, LANE), lambda i, ts: (i, 0)),
                  pl.BlockSpec((1, 6, D), lambda i, ts: (_mod_row(i, tps, tiles_ctx), 0, 0)),
                  pl.BlockSpec(memory_space=pl.ANY)],
        out_specs=tile,
        scratch_shapes=[pltpu.VMEM((2, 2, TM * SUB, LANE), jnp.uint32),
                        pltpu.SemaphoreType.DMA((2,))])
    return pl.pallas_call(
        _combine_kernel, grid_spec=grid_spec,
        out_shape=jax.ShapeDtypeStruct((T, D), F32),
        input_output_aliases={3: 0},
        compiler_params=_cparams(("arbitrary",)),
        name="moe_combine",
    )(tile_start, idx, idx, X2, meta, modl, ye)


def _rope_tables(n):
    rows = n // GRID_W
    row = jnp.repeat(jnp.arange(rows, dtype=F32), GRID_W)
    col = jnp.tile(jnp.arange(GRID_W, dtype=F32), rows)
    half = HEAD_DIM // 2
    inv_freq = 1.0 / (ROPE_THETA ** (jnp.arange(0, half, 2, dtype=F32) / half))
    ar = row[:, None] * inv_freq
    ac = col[:, None] * inv_freq
    cos = jnp.concatenate([jnp.cos(ar), jnp.cos(ar), jnp.cos(ac), jnp.cos(ac)], axis=-1)
    sin = jnp.concatenate([-jnp.sin(ar), jnp.sin(ar), -jnp.sin(ac), jnp.sin(ac)], axis=-1)
    return jnp.tile(cos, (1, 2)), jnp.tile(sin, (1, 2))


def kernel(x, c, ctx, c_ctx, norm1_g, norm2_g, w_mod, b_mod, w_in, w_out, q_norm_g, k_norm_g, attn_sink,
           conv_dw_w, conv_dw_b, conv_ln_g, conv_ln_b, conv_pw_w, conv_pw_b, ssd_conv_w, ssd_conv_b, ssd_a_log,
           ssd_dt_bias, ssd_d, ssd_norm_g, pool_w, pool_scale, moe_group_router, moe_expert_router, moe_w_gate,
           moe_w_up, moe_w_down):
    B, N, D = x.shape
    LC = ctx.shape[1]
    S = LC + N
    L = w_mod.shape[0]
    assert LC % TM == 0 and N % TM == 0 and N % GRID_W == 0 and D == 2 * ROW_WORDS
    tps = S // TM
    tiles_ctx = LC // TM
    T = B * S

    rows = -(-(B + 1) // 8) * 8
    cc = jnp.concatenate([c, c_ctx[None, :], jnp.zeros((rows - B - 1, D), F32)], axis=0)
    mod = _modulation(cc, w_mod, b_mod).reshape(L, rows, 6, D)

    w_in_p = jnp.concatenate([w_in[:, :, :DT_SRC + 16], jnp.zeros((L, D, LANE - 16), F32), w_in[:, :, DT_SRC + 16:]],
                             axis=-1).astype(BF16)
    w_out_b = w_out.astype(BF16)
    w_router = jnp.concatenate([moe_group_router, moe_expert_router,
                                jnp.zeros((L, D, LANE - MOE_GROUPS - MOE_EXPERTS), F32)], axis=-1)
    wr_hi = w_router.astype(BF16)
    wr = jnp.concatenate([wr_hi, (w_router - wr_hi.astype(F32)).astype(BF16)], axis=-1)

    cos, sin = _rope_tables(N)
    bd = (jnp.arange(LANE)[:, None] // HEAD_DIM == jnp.arange(LANE)[None, :] // HEAD_DIM).astype(BF16)
    pad_lane = lambda v: jnp.concatenate([v.reshape(-1), jnp.zeros((LANE - v.size,), F32)]).reshape(1, LANE)

    X = jnp.concatenate([ctx, x], axis=1).reshape(T, D)
    for l in range(L):
        modl = jnp.stack([jnp.broadcast_to(mod[l, B], (B, 6, D)), mod[l, :B]], axis=1).reshape(2 * B, 6, D)
        q, k, v, glu, z, xbc, dt, pin = _in_projection(X, modl, norm1_g[l][None], w_in_p, l, tps, tiles_ctx)
        r3 = lambda a: a.reshape(B, S, a.shape[-1])
        y_att = _attention(r3(q), r3(k), r3(v), cos, sin, jnp.tile(q_norm_g[l], 2)[None], jnp.tile(k_norm_g[l], 2)[None],
                           bd, attn_sink[l], LC, N)
        dww = jnp.concatenate([conv_dw_w[l], jnp.zeros((32 - CONV_K, GROUP_W), F32)], axis=0)
        y_conv = _conformer(r3(glu), dww, conv_dw_b[l][None], conv_ln_g[l][None], conv_ln_b[l][None],
                            conv_pw_w[l].astype(BF16), conv_pw_b[l][None], LC, N)
        y_ssd = _ssd(r3(xbc), r3(dt), r3(z), ssd_conv_w[l], ssd_conv_b[l][None], pad_lane(-jnp.exp(ssd_a_log[l])),
                     pad_lane(ssd_dt_bias[l]), jnp.repeat(ssd_d[l], HEAD_DIM)[None], ssd_norm_g[l][None], LC, N)
        y_pool = _pool(r3(pin), pool_w[l].astype(BF16), pool_scale[l][None], LC, N)
        ys = [a.reshape(T, GROUP_W) for a in (y_att, y_conv, y_ssd, y_pool)]
        last = l == L - 1
        X, h2p, logits = _out_projection(ys, X, modl, norm2_g[l][None], w_out_b, wr, l, tps, tiles_ctx,
                                         latent_only=last)
        n_tiles = (2 * X.shape[0]) // TM + MOE_EXPERTS
        meta, idx, tile_start, t_exp, t_first, t_active = _route(logits, n_tiles)
        xs = _moe_dispatch(h2p, idx, tile_start, n_tiles)
        ye = _moe_experts(xs, t_exp, t_first, t_active, moe_w_gate, moe_w_up, moe_w_down, l)
        if last:
            X = _moe_combine(X, ye, meta, idx, tile_start, modl, tps - tiles_ctx, 0)
        else:
            X = _moe_combine(X, ye, meta, idx, tile_start, modl, tps, tiles_ctx)
    return X.reshape(B, N, D)
```

```python
import functools
import math

import jax
import jax.numpy as jnp
from jax import lax
from jax.experimental import pallas as pl
from jax.experimental.pallas import tpu as pltpu

F32 = jnp.float32
BF16 = jnp.bfloat16

EPS = 1e-6
GRID_W = 64
HEAD_DIM = 64
ATT_HEADS = 8
ATT_KV_HEADS = 2
ATT_REP = ATT_HEADS // ATT_KV_HEADS
ATT_BLOCK = 128
ATT_WINDOW = 128
ROPE_THETA = 10000.0
GROUP_W = 512
CONV_K = 31
CONV_PAD = 16
CONV_ROWS = 128
SSD_HEADS = 8
SSD_GROUPS = 2
SSD_STATE = 128
SSD_CHUNK = 128
SSD_CONV = 4
SSD_PAD = 8
SSD_INNER = 512
SSD_XBC = SSD_INNER + 2 * SSD_GROUPS * SSD_STATE
POOL_SIZES = (2, 4, 8, 16)
POOL_CH = 128
POOL_PAD = 8
MOE_GROUPS = 4
MOE_PER_GROUP = 8
MOE_EXPERTS = 32
D_EXPERT = 512

TM = 256
LANE = 128
NEG = -1e30
VMEM_LIMIT = 56 * 1024 * 1024

IN_COLS = (("q", 0, 512), ("k", 512, 128), ("v", 640, 128), ("glu", 768, 1024), ("z", 1792, 512),
           ("xbc", 2304, 1024), ("dt", 3328, 128), ("pin", 3456, 512))
D_IN_PAD = 3968
DT_SRC = 3328


def _cparams(sem):
    return pltpu.CompilerParams(dimension_semantics=sem, vmem_limit_bytes=VMEM_LIMIT)


def _resident(shape, layer=None):
    nd = len(shape)
    if layer is None:
        return pl.BlockSpec(shape, lambda *_: (0,) * nd, pipeline_mode=pl.Buffered(1))
    return pl.BlockSpec((1,) + tuple(shape), lambda *_: (layer,) + (0,) * nd, pipeline_mode=pl.Buffered(1))


def _dot(a, b):
    return jnp.dot(a, b, preferred_element_type=F32)


def _dot_nt(a, b):
    return lax.dot_general(a, b, (((1,), (1,)), ((), ())), preferred_element_type=F32)


def _split2(x):
    hi = x.astype(BF16)
    lo = (x - hi.astype(F32)).astype(BF16)
    return hi, lo


def _split3(x):
    h1 = x.astype(BF16)
    r1 = x - h1.astype(F32)
    h2 = r1.astype(BF16)
    h3 = (r1 - h2.astype(F32)).astype(BF16)
    return h1, h2, h3


def _silu(x):
    return x * jax.nn.sigmoid(x)


def _mod_kernel(c_ref, w_ref, b_ref, o_ref):
    s = _silu(c_ref[...])
    o_ref[0] = _dot(s.astype(BF16), w_ref[0].astype(BF16)) + b_ref[0]


def _modulation(cc, w_mod, b_mod):
    L, D, D6 = w_mod.shape
    R = cc.shape[0]
    tn = 1024
    return pl.pallas_call(
        _mod_kernel,
        grid=(L, D6 // tn),
        in_specs=[pl.BlockSpec((R, D), lambda l, j: (0, 0)),
                  pl.BlockSpec((1, D, tn), lambda l, j: (l, 0, j)),
                  pl.BlockSpec((1, 1, tn), lambda l, j: (l, 0, j))],
        out_specs=pl.BlockSpec((1, R, tn), lambda l, j: (l, 0, j)),
        out_shape=jax.ShapeDtypeStruct((L, R, D6), F32),
        compiler_params=_cparams(("arbitrary", "arbitrary")),
        name="modulation",
    )(cc, w_mod, b_mod.reshape(L, 1, D6))


def _inproj_kernel(x_ref, mod_ref, g_ref, w_ref, *out_refs):
    x = x_ref[...]
    m = mod_ref[0]
    ms = jnp.mean(x * x, axis=-1, keepdims=True)
    h = x * lax.rsqrt(ms + EPS) * g_ref[...]
    h = h * (1.0 + m[1:2]) + m[0:1]
    hb = h.astype(BF16)
    for ref, (_, a, w) in zip(out_refs, IN_COLS):
        ref[...] = _dot(hb, w_ref[0, :, a:a + w]).astype(ref.dtype)


def _mod_row(i, tps, tiles_ctx):
    return (i // tps) * 2 + jnp.where((i % tps) >= tiles_ctx, 1, 0)


def _in_projection(X2, modl, g1, w_in_p, layer, tps, tiles_ctx):
    T, D = X2.shape
    dts = {"dt": F32}
    out_shape = [jax.ShapeDtypeStruct((T, w), dts.get(n, BF16)) for n, _, w in IN_COLS]
    out_specs = [pl.BlockSpec((TM, w), lambda i: (i, 0)) for _, _, w in IN_COLS]
    return pl.pallas_call(
        _inproj_kernel,
        grid=(T // TM,),
        in_specs=[pl.BlockSpec((TM, D), lambda i: (i, 0)),
                  pl.BlockSpec((1, 6, D), lambda i: (_mod_row(i, tps, tiles_ctx), 0, 0)),
                  _resident((1, D)),
                  _resident((D, D_IN_PAD), layer)],
        out_specs=out_specs,
        out_shape=out_shape,
        compiler_params=_cparams(("parallel",)),
        name="in_projection",
    )(X2, modl, g1, w_in_p)


def _norm_heads(x, g, bd):
    hi, lo = _split2(x * x)
    s = _dot(hi, bd) + _dot(lo, bd)
    return x * lax.rsqrt(s * (1.0 / HEAD_DIM) + EPS) * g


def _rope(x, cos, sin):
    lane = lax.broadcasted_iota(jnp.int32, x.shape, 1)
    sw = jnp.where((lane & 31) < 16, pltpu.roll(x, LANE - 16, 1), pltpu.roll(x, 16, 1))
    return x * cos + sw * sin


def _sink_softmax_pv(parts, sink_col):
    m = sink_col
    for s, _ in parts:
        m = jnp.maximum(m, jnp.max(s, axis=-1, keepdims=True))
    den = jnp.exp(sink_col - m)
    o = None
    for s, v in parts:
        p = jnp.exp(s - m)
        den = den + jnp.sum(p, axis=-1, keepdims=True)
        pv = _dot(p.astype(BF16), v)
        o = pv if o is None else o + pv
    return o / den


def _attn_kernel(sink_ref, q_ref, k_ref, v_ref, cos_ref, sin_ref, qg_ref, kg_ref, bd_ref, o_ref,
                 qh, kh, vh, *, LC, N):
    S = LC + N
    koff = LC + ATT_BLOCK
    bd = bd_ref[...]
    scale = HEAD_DIM ** -0.5
    zpad = jnp.zeros((ATT_BLOCK, HEAD_DIM), BF16)
    for h in range(ATT_KV_HEADS):
        for buf in (kh, vh):
            buf[h, LC:koff, :] = zpad
            buf[h, koff + N:koff + N + ATT_BLOCK, :] = zpad

    rc = TM
    for c0 in range(0, S, rc):
        lat = c0 >= LC
        dst = c0 + ATT_BLOCK if lat else c0
        if lat:
            cos = cos_ref[c0 - LC:c0 - LC + rc, :]
            sin = sin_ref[c0 - LC:c0 - LC + rc, :]
        kn = _norm_heads(k_ref[0, c0:c0 + rc, :].astype(F32), kg_ref[...], bd)
        if lat:
            kn = _rope(kn, cos, sin)
        knb = kn.astype(BF16)
        vv = v_ref[0, c0:c0 + rc, :]
        for h in range(ATT_KV_HEADS):
            kh[h, dst:dst + rc, :] = knb[:, h * HEAD_DIM:(h + 1) * HEAD_DIM]
            vh[h, dst:dst + rc, :] = vv[:, h * HEAD_DIM:(h + 1) * HEAD_DIM]
        for cb in range(ATT_HEADS // 2):
            qn = _norm_heads(q_ref[0, c0:c0 + rc, cb * LANE:(cb + 1) * LANE].astype(F32), qg_ref[...], bd)
            if lat:
                qn = _rope(qn, cos, sin)
            qnb = (qn * scale).astype(BF16)
            qh[2 * cb, c0:c0 + rc, :] = qnb[:, :HEAD_DIM]
            qh[2 * cb + 1, c0:c0 + rc, :] = qnb[:, HEAD_DIM:]

    def sink_column(h, rows):
        grp = lax.broadcasted_iota(jnp.int32, (ATT_REP * rows, 1), 0) // rows
        col = jnp.full((ATT_REP * rows, 1), sink_ref[ATT_REP * h], F32)
        for r in range(1, ATT_REP):
            col = jnp.where(grp == r, sink_ref[ATT_REP * h + r], col)
        return col

    def store_heads(o, h, row0, rows):
        for pr in range(ATT_REP // 2):
            pair = jnp.concatenate([o[(2 * pr) * rows:(2 * pr + 1) * rows],
                                    o[(2 * pr + 1) * rows:(2 * pr + 2) * rows]], axis=1)
            cb = (ATT_REP // 2) * h + pr
            o_ref[0, pl.ds(row0, rows), cb * LANE:(cb + 1) * LANE] = pair.astype(o_ref.dtype)

    for h in range(ATT_KV_HEADS):
        q4 = jnp.concatenate([qh[ATT_REP * h + r, 0:LC, :] for r in range(ATT_REP)], axis=0)
        kc = kh[h, 0:LC, :]
        vc = vh[h, 0:LC, :]
        o = _sink_softmax_pv([(_dot_nt(q4, kc), vc)], sink_column(h, LC))
        store_heads(o, h, 0, LC)

    nk = 3 * ATT_BLOCK
    qi = lax.broadcasted_iota(jnp.int32, (ATT_REP * ATT_BLOCK, nk), 0) & (ATT_BLOCK - 1)
    kj = lax.broadcasted_iota(jnp.int32, (ATT_REP * ATT_BLOCK, nk), 1)
    rel = kj - ATT_BLOCK - qi
    in_window = (rel <= ATT_WINDOW) & (rel >= -ATT_WINDOW)

    def block(j, carry):
        r0 = pl.multiple_of(j * ATT_BLOCK, ATT_BLOCK)
        kpos = kj + (j - 1) * ATT_BLOCK
        ok = in_window & (kpos >= 0) & (kpos < N)
        for h in range(ATT_KV_HEADS):
            q4 = jnp.concatenate([qh[ATT_REP * h + r, pl.ds(LC + r0, ATT_BLOCK), :] for r in range(ATT_REP)],
                                 axis=0)
            kb = kh[h, pl.ds(LC + r0, nk), :]
            vb = vh[h, pl.ds(LC + r0, nk), :]
            s_lat = jnp.where(ok, _dot_nt(q4, kb), NEG)
            s_ctx = _dot_nt(q4, kh[h, 0:LC, :])
            o = _sink_softmax_pv([(s_lat, vb), (s_ctx, vh[h, 0:LC, :])], sink_column(h, ATT_BLOCK))
            store_heads(o, h, LC + r0, ATT_BLOCK)
        return carry

    lax.fori_loop(0, N // ATT_BLOCK, block, 0)


def _attention(q, k, v, cos, sin, qg, kg, bd, sink, LC, N):
    B, S, _ = q.shape
    kern = functools.partial(_attn_kernel, LC=LC, N=N)
    grid_spec = pltpu.PrefetchScalarGridSpec(
        num_scalar_prefetch=1,
        grid=(B,),
        in_specs=[pl.BlockSpec((1, S, 512), lambda b, s: (b, 0, 0)),
                  pl.BlockSpec((1, S, 128), lambda b, s: (b, 0, 0)),
                  pl.BlockSpec((1, S, 128), lambda b, s: (b, 0, 0)),
                  pl.BlockSpec((N, LANE), lambda b, s: (0, 0)),
                  pl.BlockSpec((N, LANE), lambda b, s: (0, 0)),
                  pl.BlockSpec((1, LANE), lambda b, s: (0, 0)),
                  pl.BlockSpec((1, LANE), lambda b, s: (0, 0)),
                  pl.BlockSpec((LANE, LANE), lambda b, s: (0, 0))],
        out_specs=pl.BlockSpec((1, S, 512), lambda b, s: (b, 0, 0)),
        scratch_shapes=[pltpu.VMEM((ATT_HEADS, S, HEAD_DIM), BF16),
                        pltpu.VMEM((ATT_KV_HEADS, S + 2 * ATT_BLOCK, HEAD_DIM), BF16),
                        pltpu.VMEM((ATT_KV_HEADS, S + 2 * ATT_BLOCK, HEAD_DIM), BF16)])
    return pl.pallas_call(
        kern, grid_spec=grid_spec,
        out_shape=jax.ShapeDtypeStruct((B, S, 512), BF16),
        compiler_params=_cparams(("parallel",)),
        name="attention",
    )(sink, q, k, v, cos, sin, qg, kg, bd)


def _conv_kernel(glu_ref, dww_ref, dwb_ref, lng_ref, lnb_ref, pww_ref, pwb_ref, o_ref, hp, rot, *, LC, N):
    C = GROUP_W
    half = CONV_K // 2
    rows = CONV_ROWS

    def stream(s0, ln):
        hp[0:CONV_PAD, :] = jnp.zeros((CONV_PAD, C), F32)
        hp[CONV_PAD + ln:2 * CONV_PAD + ln, :] = jnp.zeros((CONV_PAD, C), F32)

        def fill(i, carry):
            r0 = pl.multiple_of(i * TM, TM)
            g = glu_ref[0, pl.ds(s0 + r0, TM), :]
            a = g[:, :C].astype(F32)
            gate = g[:, C:].astype(F32)
            hp[pl.ds(CONV_PAD + r0, TM), :] = a * jax.nn.sigmoid(gate)
            return carry

        lax.fori_loop(0, ln // TM, fill, 0)

        def chunk(i, carry):
            r0 = pl.multiple_of(i * rows, rows)
            parts = []
            for cb in range(C // LANE):
                win = hp[pl.ds(r0, rows + 2 * CONV_PAD), cb * LANE:(cb + 1) * LANE]
                for ph in range(8):
                    rot[cb, ph] = win[ph:ph + rot.shape[2]]
                acc = jnp.zeros((rows, LANE), F32)
                for t in range(CONV_K):
                    off = CONV_PAD - half + t
                    w = dww_ref[t:t + 1, cb * LANE:(cb + 1) * LANE]
                    acc = acc + w * rot[cb, off % 8, (off // 8) * 8:(off // 8) * 8 + rows, :]
                parts.append(acc)
            y = jnp.concatenate(parts, axis=1) + dwb_ref[...]
            mu = jnp.mean(y, axis=-1, keepdims=True)
            yc = y - mu
            var = jnp.mean(yc * yc, axis=-1, keepdims=True)
            z = yc * lax.rsqrt(var + EPS) * lng_ref[...] + lnb_ref[...]
            z = _silu(z)
            out = _dot(z.astype(BF16), pww_ref[...]) + pwb_ref[...]
            o_ref[0, pl.ds(s0 + r0, rows), :] = out.astype(o_ref.dtype)
            return carry

        lax.fori_loop(0, ln // rows, chunk, 0)

    stream(0, LC)
    stream(LC, N)


def _conformer(glu, dww, dwb, lng, lnb, pww, pwb, LC, N):
    B, S, _ = glu.shape
    C = GROUP_W
    kern = functools.partial(_conv_kernel, LC=LC, N=N)
    vec = pl.BlockSpec((1, C), lambda b: (0, 0))
    return pl.pallas_call(
        kern, grid=(B,),
        in_specs=[pl.BlockSpec((1, S, 2 * C), lambda b: (b, 0, 0)),
                  pl.BlockSpec((32, C), lambda b: (0, 0)),
                  vec, vec, vec,
                  pl.BlockSpec((C, C), lambda b: (0, 0)),
                  vec],
        out_specs=pl.BlockSpec((1, S, C), lambda b: (b, 0, 0)),
        out_shape=jax.ShapeDtypeStruct((B, S, C), BF16),
        scratch_shapes=[pltpu.VMEM((max(LC, N) + 2 * CONV_PAD, C), F32),
                        pltpu.VMEM((C // LANE, 8, CONV_ROWS + 2 * CONV_PAD - 8, LANE), F32)],
        compiler_params=_cparams(("parallel",)),
        name="conformer_conv",
    )(glu, dww, dwb, lng, lnb, pww, pwb)


def _ssd_kernel(xbc_ref, dt_ref, z_ref, cw_ref, cb_ref, arow_ref, dtb_ref, dsk_ref, ng_ref, o_ref,
                xp, xs, bm, cm, dts, yacc, state, *, LC, N):
    S = LC + N
    Q = SSD_CHUNK
    nc = S // Q
    nc_ctx = LC // Q
    ii = lax.broadcasted_iota(jnp.int32, (Q, Q), 0)
    jj = lax.broadcasted_iota(jnp.int32, (Q, Q), 1)
    lower = ii >= jj
    tri = (jnp.where(lower, 1.0, 0.0).astype(BF16), jnp.where(jj >= ii, 1.0, 0.0).astype(BF16))
    causal = (lower, jj >= ii)
    first_half = lax.broadcasted_iota(jnp.int32, (Q, LANE), 1) < SSD_STATE // 2

    def stream(s0, ln):
        xp[0:SSD_PAD, :] = jnp.zeros((SSD_PAD, SSD_XBC), F32)
        xp[SSD_PAD + ln:2 * SSD_PAD + ln, :] = jnp.zeros((SSD_PAD, SSD_XBC), F32)

        def fill(i, carry):
            r0 = pl.multiple_of(i * Q, Q)
            xp[pl.ds(SSD_PAD + r0, Q), :] = xbc_ref[0, pl.ds(s0 + r0, Q), :].astype(F32)
            return carry

        lax.fori_loop(0, ln // Q, fill, 0)

        def conv(i, carry):
            r0 = pl.multiple_of(i * Q, Q)
            for cb in range(SSD_XBC // LANE):
                cs = slice(cb * LANE, (cb + 1) * LANE)
                win = xp[pl.ds(r0, Q + 2 * SSD_PAD), cs]
                acc = jnp.zeros((Q, LANE), F32)
                for t in range(SSD_CONV):
                    acc = acc + cw_ref[t:t + 1, cs] * win[SSD_PAD - 2 + t:SSD_PAD - 2 + t + Q]
                y = _silu(acc + cb_ref[:, cs])
                if cb < SSD_INNER // LANE:
                    xs[pl.ds(s0 + r0, Q), cs] = y
                elif cb < (SSD_INNER + SSD_GROUPS * SSD_STATE) // LANE:
                    c2 = cb - SSD_INNER // LANE
                    bm[pl.ds(s0 + r0, Q), c2 * LANE:(c2 + 1) * LANE] = y
                else:
                    c2 = cb - (SSD_INNER + SSD_GROUPS * SSD_STATE) // LANE
                    cm[pl.ds(s0 + r0, Q), c2 * LANE:(c2 + 1) * LANE] = y
            return carry

        lax.fori_loop(0, ln // Q, conv, 0)

    stream(0, LC)
    stream(LC, N)

    def softplus_rows(i, carry):
        r0 = pl.multiple_of(i * Q, Q)
        v = dt_ref[0, pl.ds(r0, Q), :] + dtb_ref[...]
        dts[pl.ds(r0, Q), :] = jnp.maximum(v, 0.0) + jnp.log(1.0 + jnp.exp(-jnp.abs(v)))
        return carry

    lax.fori_loop(0, nc, softplus_rows, 0)

    def chunk_step(c, d):
        r0 = pl.multiple_of(c * Q, Q)
        dtc = dts[pl.ds(r0, Q), :]
        a1, a2, a3 = _split3(dtc * arow_ref[...])
        cum = _dot(tri[d], a1) + _dot(tri[d], a2) + _dot(tri[d], a3)
        cum_t = cum.T
        dt_t = dtc.T
        tot = cum[Q - 1:Q, :] if d == 0 else cum[0:1, :]
        e_cum = jnp.exp(cum)
        e_tot = jnp.exp(tot)
        w_state = jnp.exp(tot - cum) * dtc
        for g in range(SSD_GROUPS):
            bg = bm[pl.ds(r0, Q), g * SSD_STATE:(g + 1) * SSD_STATE]
            cg = cm[pl.ds(r0, Q), g * SSD_STATE:(g + 1) * SSD_STATE].astype(BF16)
            bg_t = bg.T.astype(BF16)
            cb_mat = _dot(cg, bg_t)
            for pr in range(SSD_HEADS // SSD_GROUPS // 2):
                pair = g * 2 + pr
                lanes = slice(pair * LANE, (pair + 1) * LANE)
                xpair = xs[pl.ds(r0, Q), lanes]
                y = None
                for sub in range(2):
                    hd = d * SSD_HEADS + pair * 2 + sub
                    seg = cum[:, hd:hd + 1] - cum_t[hd:hd + 1, :]
                    dec = jnp.where(causal[d], jnp.exp(jnp.minimum(seg, 0.0)), 0.0)
                    w = (cb_mat * dec * dt_t[hd:hd + 1, :]).astype(BF16)
                    xm = jnp.where(first_half if sub == 0 else ~first_half, xpair, 0.0).astype(BF16)
                    t = _dot(w, xm)
                    y = t if y is None else y + t
                h0 = d * SSD_HEADS + pair * 2
                sel = lambda m: jnp.where(first_half, m[:, h0:h0 + 1], m[:, h0 + 1:h0 + 2])
                st = state[d * (SSD_HEADS // 2) + pair]
                y = y + _dot(cg, st.astype(BF16)) * sel(e_cum)
                xw = (xpair * sel(w_state)).astype(BF16)
                state[d * (SSD_HEADS // 2) + pair] = st * sel(e_tot) + _dot(bg_t, xw)
                if d == 0:
                    yacc[pl.ds(r0, Q), lanes] = y
                else:
                    yacc[pl.ds(r0, Q), lanes] = yacc[pl.ds(r0, Q), lanes] + y

    state[...] = jnp.zeros(state.shape, F32)

    def fwd(c, carry):
        chunk_step(c, 0)
        return carry

    lax.fori_loop(0, nc, fwd, 0)

    def bwd_ctx(i, carry):
        chunk_step(nc_ctx - 1 - i, 1)
        return carry

    lax.fori_loop(0, nc_ctx, bwd_ctx, 0)

    def bwd_lat(i, carry):
        chunk_step(nc - 1 - i, 1)
        return carry

    lax.fori_loop(0, nc - nc_ctx, bwd_lat, 0)

    def gate_out(i, carry):
        r0 = pl.multiple_of(i * Q, Q)
        y = yacc[pl.ds(r0, Q), :] + xs[pl.ds(r0, Q), :] * dsk_ref[...]
        y = y * _silu(z_ref[0, pl.ds(r0, Q), :].astype(F32))
        ms = jnp.mean(y * y, axis=-1, keepdims=True)
        o_ref[0, pl.ds(r0, Q), :] = (y * lax.rsqrt(ms + EPS) * ng_ref[...]).astype(o_ref.dtype)
        return carry

    lax.fori_loop(0, nc, gate_out, 0)


def _ssd(xbc, dt, z, cw, cb, arow, dtb, dsk, ng, LC, N):
    B, S, _ = xbc.shape
    kern = functools.partial(_ssd_kernel, LC=LC, N=N)
    row = lambda w: pl.BlockSpec((1, w), lambda b: (0, 0))
    return pl.pallas_call(
        kern, grid=(B,),
        in_specs=[pl.BlockSpec((1, S, SSD_XBC), lambda b: (b, 0, 0)),
                  pl.BlockSpec((1, S, LANE), lambda b: (b, 0, 0)),
                  pl.BlockSpec((1, S, SSD_INNER), lambda b: (b, 0, 0)),
                  pl.BlockSpec((SSD_CONV, SSD_XBC), lambda b: (0, 0)),
                  row(SSD_XBC), row(LANE), row(LANE), row(SSD_INNER), row(SSD_INNER)],
        out_specs=pl.BlockSpec((1, S, SSD_INNER), lambda b: (b, 0, 0)),
        out_shape=jax.ShapeDtypeStruct((B, S, SSD_INNER), BF16),
        scratch_shapes=[pltpu.VMEM((max(LC, N) + 2 * SSD_PAD, SSD_XBC), F32),
                        pltpu.VMEM((S, SSD_INNER), F32),
                        pltpu.VMEM((S, SSD_GROUPS * SSD_STATE), F32),
                        pltpu.VMEM((S, SSD_GROUPS * SSD_STATE), F32),
                        pltpu.VMEM((S, LANE), F32),
                        pltpu.VMEM((S, SSD_INNER), F32),
                        pltpu.VMEM((SSD_HEADS, SSD_STATE, LANE), F32)],
        compiler_params=_cparams(("parallel",)),
        name="ssd",
    )(xbc, dt, z, cw, cb, arow, dtb, dsk, ng)


def _pool_kernel(pin_ref, pw_ref, ps_ref, o_ref, up, *, LC, N):
    C = GROUP_W
    rows = TM

    def stream(s0, ln):
        up[0:POOL_PAD, :] = jnp.zeros((POOL_PAD, C), F32)
        up[POOL_PAD + ln:2 * POOL_PAD + ln, :] = jnp.zeros((POOL_PAD, C), F32)

        def fill(i, carry):
            r0 = pl.multiple_of(i * rows, rows)
            up[pl.ds(POOL_PAD + r0, rows), :] = pin_ref[0, pl.ds(s0 + r0, rows), :].astype(F32)
            return carry

        lax.fori_loop(0, ln // rows, fill, 0)

        def chunk(i, carry):
            r0 = pl.multiple_of(i * rows, rows)
            t = r0 + lax.broadcasted_iota(jnp.int32, (rows, 1), 0)
            outs = []
            for gi, w in enumerate(POOL_SIZES):
                cs = slice(gi * POOL_CH, (gi + 1) * POOL_CH)
                win = up[pl.ds(r0, rows + 2 * POOL_PAD), cs]
                acc = jnp.zeros((rows, POOL_CH), F32)
                for d in range(-(w // 2), w - w // 2):
                    acc = acc + win[POOL_PAD + d:POOL_PAD + d + rows]
                cnt = jnp.minimum(t + (w - w // 2), ln) - jnp.maximum(t - w // 2, 0)
                p = acc / cnt.astype(F32) - win[POOL_PAD:POOL_PAD + rows]
                outs.append(_dot(p.astype(BF16), pw_ref[gi]))
            y = jnp.concatenate(outs, axis=1) * ps_ref[...]
            o_ref[0, pl.ds(s0 + r0, rows), :] = y.astype(o_ref.dtype)
            return carry

        lax.fori_loop(0, ln // rows, chunk, 0)

    stream(0, LC)
    stream(LC, N)


def _pool(pin, pw, ps, LC, N):
    B, S, C = pin.shape
    kern = functools.partial(_pool_kernel, LC=LC, N=N)
    return pl.pallas_call(
        kern, grid=(B,),
        in_specs=[pl.BlockSpec((1, S, C), lambda b: (b, 0, 0)),
                  pl.BlockSpec((len(POOL_SIZES), POOL_CH, POOL_CH), lambda b: (0, 0, 0)),
                  pl.BlockSpec((1, C), lambda b: (0, 0))],
        out_specs=pl.BlockSpec((1, S, C), lambda b: (b, 0, 0)),
        out_shape=jax.ShapeDtypeStruct((B, S, C), BF16),
        scratch_shapes=[pltpu.VMEM((max(LC, N) + 2 * POOL_PAD, C), F32)],
        compiler_params=_cparams(("parallel",)),
        name="pool_mixer",
    )(pin, pw, ps)


def _pack_halves(x):
    w = x.shape[1] // 2
    u = lax.bitcast_convert_type(x.astype(BF16).astype(F32), jnp.uint32)
    return (u[:, :w] >> 16) | (u[:, w:] & jnp.uint32(0xFFFF0000))


def _unpack_halves(p):
    lo = lax.bitcast_convert_type(p << 16, F32)
    hi = lax.bitcast_convert_type(p & jnp.uint32(0xFFFF0000), F32)
    return lo, hi


SUB = 8
ROW_WORDS = SUB * LANE


def _store_token_tiles(ref, packed):
    rows = packed.shape[0]
    for s in range(SUB):
        ref[pl.ds(s, rows, stride=SUB), :] = packed[:, s * LANE:(s + 1) * LANE]


def _load_token_tiles(ref, rows, s):
    return ref[pl.ds(s, rows, stride=SUB), :]


def _outproj_kernel(ya_ref, yb_ref, yc_ref, yd_ref, x_ref, mod_ref, g_ref, w_ref, wr_ref,
                    xo_ref, h2_ref, lg_ref):
    acc = None
    for i, ref in enumerate((ya_ref, yb_ref, yc_ref, yd_ref)):
        t = _dot(ref[...], w_ref[0, i * GROUP_W:(i + 1) * GROUP_W, :])
        acc = t if acc is None else acc + t
    m = mod_ref[0]
    x = x_ref[...] + m[2:3] * acc
    xo_ref[...] = x
    ms = jnp.mean(x * x, axis=-1, keepdims=True)
    h2 = x * lax.rsqrt(ms + EPS) * g_ref[...]
    h2 = h2 * (1.0 + m[4:5]) + m[3:4]
    _store_token_tiles(h2_ref, _pack_halves(h2))
    h_hi, h_lo = _split2(h2)
    both = _dot(h_hi, wr_ref[0])
    lg_ref[...] = both[:, :LANE] + (both[:, LANE:] + _dot(h_lo, wr_ref[0, :, :LANE]))


def _out_projection(ys, X2, modl, g2, w_out, wr, layer, tps, tiles_ctx, latent_only=False):
    T, D = X2.shape
    if latent_only:
        tl = tps - tiles_ctx
        steps = (T // TM) // tps * tl
        src = lambda i: ((i // tl) * tps + tiles_ctx + i % tl, 0)
        mrow = lambda i: ((i // tl) * 2 + 1, 0, 0)
        aliases = {}
    else:
        steps = T // TM
        src = lambda i: (i, 0)
        mrow = lambda i: (_mod_row(i, tps, tiles_ctx), 0, 0)
        aliases = {4: 0}
    rows = steps * TM
    ytile = pl.BlockSpec((TM, GROUP_W), src)
    return pl.pallas_call(
        _outproj_kernel,
        grid=(steps,),
        in_specs=[ytile, ytile, ytile, ytile,
                  pl.BlockSpec((TM, D), src),
                  pl.BlockSpec((1, 6, D), mrow),
                  _resident((1, D)),
                  _resident((4 * GROUP_W, D), layer),
                  _resident((D, 2 * LANE), layer)],
        out_specs=[pl.BlockSpec((TM, D), lambda i: (i, 0)),
                   pl.BlockSpec((TM * SUB, LANE), lambda i: (i, 0)),
                   pl.BlockSpec((TM, LANE), lambda i: (i, 0))],
        out_shape=[jax.ShapeDtypeStruct((rows, D), F32),
                   jax.ShapeDtypeStruct((rows * SUB, LANE), jnp.uint32),
                   jax.ShapeDtypeStruct((rows, LANE), F32)],
        input_output_aliases=aliases,
        compiler_params=_cparams(("parallel",)),
        name="out_projection",
    )(*ys, X2, modl, g2, w_out, wr)


META_E = 0
META_R = 2
META_W = 4


def _route_kernel(lg_ref, meta_ref, cnt_ref, carry):
    @pl.when(pl.program_id(0) == 0)
    def _():
        carry[...] = jnp.zeros(carry.shape, F32)

    lg = lg_ref[...]
    lane = lax.broadcasted_iota(jnp.int32, lg.shape, 1).astype(F32)
    big = 1e9
    rmax = lambda m: jnp.max(jnp.where(m, lg, NEG), axis=-1, keepdims=True)
    first = lambda m: jnp.min(jnp.where(m, lane, big), axis=-1, keepdims=True)

    gm = lane < MOE_GROUPS
    gmax = rmax(gm)
    gidx = first(gm & (lg == gmax))
    g_w = 1.0 / jnp.sum(jnp.where(gm, jnp.exp(lg - gmax), 0.0), axis=-1, keepdims=True)
    lo = MOE_GROUPS + MOE_PER_GROUP * gidx
    em = (lane >= lo) & (lane < lo + MOE_PER_GROUP)
    v1 = rmax(em)
    i1 = first(em & (lg == v1))
    em2 = em & (lane != i1)
    v2 = rmax(em2)
    i2 = first(em2 & (lg == v2))
    t = jnp.exp(v2 - v1)
    w1 = g_w / (1.0 + t)
    w2 = g_w * t / (1.0 + t)

    oh1 = jnp.where(lane == i1, 1.0, 0.0)
    oh2 = jnp.where(lane == i2, 1.0, 0.0)
    oh = oh1 + oh2
    rows = lg.shape[0]
    ri = lax.broadcasted_iota(jnp.int32, (rows, rows), 0)
    ci = lax.broadcasted_iota(jnp.int32, (rows, rows), 1)
    before = jnp.where(ci < ri, 1.0, 0.0).astype(BF16)
    base = _dot(before, oh.astype(BF16)) + carry[0:1, :]
    r1 = jnp.sum(oh1 * base, axis=-1, keepdims=True)
    r2 = jnp.sum(oh2 * base, axis=-1, keepdims=True)
    carry[0:1, :] = carry[0:1, :] + jnp.sum(oh, axis=0, keepdims=True)

    meta = jnp.zeros(lg.shape, F32)
    for k, val in enumerate((i1 - MOE_GROUPS, i2 - MOE_GROUPS, r1, r2, w1, w2)):
        meta = jnp.where(lane == k, val, meta)
    meta_ref[...] = meta
    cnt_ref[...] = jnp.broadcast_to(carry[0:1, :], cnt_ref.shape)


def _route(logits, n_tiles):
    T = logits.shape[0]
    rt = max(r for r in (4 * TM, 2 * TM, TM) if T % r == 0)
    meta, cnt = pl.pallas_call(
        _route_kernel,
        grid=(T // rt,),
        in_specs=[pl.BlockSpec((rt, LANE), lambda i: (i, 0))],
        out_specs=[pl.BlockSpec((rt, LANE), lambda i: (i, 0)),
                   pl.BlockSpec((8, LANE), lambda i: (0, 0))],
        out_shape=[jax.ShapeDtypeStruct((T, LANE), F32), jax.ShapeDtypeStruct((8, LANE), F32)],
        scratch_shapes=[pltpu.VMEM((8, LANE), F32)],
        compiler_params=_cparams(("arbitrary",)),
        name="moe_route",
    )(logits)
    routed = meta[:, :4].astype(jnp.int32)
    counts = cnt[0, MOE_GROUPS:MOE_GROUPS + MOE_EXPERTS].astype(jnp.int32)
    ntile = (counts + TM - 1) // TM
    tile_end = jnp.cumsum(ntile)
    tile_start = (tile_end - ntile).astype(jnp.int32)
    total = tile_end[-1]
    tiles = jnp.arange(n_tiles, dtype=jnp.int32)
    active = tiles < total
    last = jnp.minimum(tiles, total - 1)
    t_exp = jnp.sum((last[:, None] >= tile_end[None, :]).astype(jnp.int32), axis=1)
    t_first = (active & (tiles == jnp.sum(jnp.where(t_exp[:, None] == jnp.arange(MOE_EXPERTS)[None, :],
                                                     tile_start[None, :], 0), axis=1))).astype(jnp.int32)
    experts = jnp.arange(MOE_EXPERTS, dtype=jnp.int32)
    first_tile = jnp.sum(jnp.where(routed[:, META_E:META_E + 2, None] == experts, tile_start, 0), axis=-1)
    dest = (first_tile * TM + routed[:, META_R:META_R + 2]) * SUB
    dest = dest.astype(jnp.int32).reshape(T // TM, TM, 2).transpose(0, 2, 1)
    return meta, dest, t_exp, t_first, active.astype(jnp.int32)


def _dispatch_kernel(dest_ref, h_ref, xs_in_ref, xs_ref, sem):
    del xs_in_ref

    def row_copy(r, dst):
        return pltpu.make_async_copy(h_ref.at[pl.ds(pl.multiple_of(r * SUB, SUB), SUB)],
                                     xs_ref.at[pl.ds(pl.multiple_of(dst, SUB), SUB)], sem)

    def start(r, carry):
        for slot in range(2):
            row_copy(r, dest_ref[0, slot, r]).start()
        return carry

    lax.fori_loop(0, TM, start, 0, unroll=8)
    for _ in range(2 * TM):
        row_copy(0, 0).wait()


def _moe_dispatch(h2p, dest, xs_init):
    T = h2p.shape[0] // SUB
    return pl.pallas_call(
        _dispatch_kernel,
        grid=(T // TM,),
        in_specs=[pl.BlockSpec((1, 2, TM), lambda i: (i, 0, 0), memory_space=pltpu.SMEM),
                  pl.BlockSpec((TM * SUB, LANE), lambda i: (i, 0)),
                  pl.BlockSpec(memory_space=pl.ANY)],
        out_specs=pl.BlockSpec(memory_space=pl.ANY),
        out_shape=jax.ShapeDtypeStruct(xs_init.shape, jnp.uint32),
        scratch_shapes=[pltpu.SemaphoreType.DMA(())],
        input_output_aliases={2: 0},
        compiler_params=_cparams(("arbitrary",)),
        name="moe_dispatch",
    )(dest, h2p, xs_init)


def _moe_kernel(te_ref, tf_ref, tv_ref, x_ref, wg_ref, wu_ref, wd_ref, o_ref, wg_b, wu_b, wd_b):
    i = pl.program_id(0)

    @pl.when(tf_ref[i] == 1)
    def _():
        wg_b[...] = wg_ref[0, 0].astype(BF16)
        wu_b[...] = wu_ref[0, 0].astype(BF16)
        wd_b[...] = wd_ref[0, 0].astype(BF16)

    @pl.when(tv_ref[i] == 1)
    def _():
        words = jnp.concatenate([_load_token_tiles(x_ref, TM, s) for s in range(SUB)], axis=1)
        lo, hi = _unpack_halves(words)
        lo = lo.astype(BF16)
        hi = hi.astype(BF16)
        half = lo.shape[1]
        g = _dot(lo, wg_b[:half, :]) + _dot(hi, wg_b[half:, :])
        u = _dot(lo, wu_b[:half, :]) + _dot(hi, wu_b[half:, :])
        hid = (_silu(g) * u).astype(BF16)
        _store_token_tiles(o_ref, _pack_halves(_dot(hid, wd_b[...])))

    @pl.when(tv_ref[i] == 0)
    def _():
        o_ref[...] = jnp.zeros(o_ref.shape, o_ref.dtype)


def _moe_experts(xs, t_exp, t_first, t_active, w_gate, w_up, w_down, layer):
    R = xs.shape[0] // SUB
    D = 2 * ROW_WORDS
    n_tiles = R // TM
    wmap = lambda i, te, tf, tv: (layer, te[i], 0, 0)
    grid_spec = pltpu.PrefetchScalarGridSpec(
        num_scalar_prefetch=3,
        grid=(n_tiles,),
        in_specs=[pl.BlockSpec((TM * SUB, LANE), lambda i, te, tf, tv: (i, 0)),
                  pl.BlockSpec((1, 1, D, D_EXPERT), wmap),
                  pl.BlockSpec((1, 1, D, D_EXPERT), wmap),
                  pl.BlockSpec((1, 1, D_EXPERT, D), wmap)],
        out_specs=pl.BlockSpec((TM * SUB, LANE), lambda i, te, tf, tv: (i, 0)),
        scratch_shapes=[pltpu.VMEM((D, D_EXPERT), BF16),
                        pltpu.VMEM((D, D_EXPERT), BF16),
                        pltpu.VMEM((D_EXPERT, D), BF16)])
    return pl.pallas_call(
        _moe_kernel, grid_spec=grid_spec,
        out_shape=jax.ShapeDtypeStruct((R * SUB, LANE), jnp.uint32),
        compiler_params=_cparams(("arbitrary",)),
        name="moe_experts",
    )(t_exp, t_first, t_active, xs, w_gate, w_up, w_down)


def _combine_kernel(dest_ref, dest_next_ref, x_ref, meta_ref, mod_ref, ye_ref, o_ref, ybuf, sems):
    i = pl.program_id(0)
    n = pl.num_programs(0)
    cur = i % 2

    def row_copy(src, buf, slot, r):
        return pltpu.make_async_copy(ye_ref.at[pl.ds(pl.multiple_of(src, SUB), SUB)],
                                     ybuf.at[buf, slot, pl.ds(pl.multiple_of(r * SUB, SUB), SUB)], sems.at[buf])

    def gather(iref, buf):
        def start(r, carry):
            for slot in range(2):
                row_copy(iref[0, slot, r], buf, slot, r).start()
            return carry

        lax.fori_loop(0, TM, start, 0, unroll=8)

    @pl.when(i == 0)
    def _():
        gather(dest_ref, 0)

    @pl.when(i + 1 < n)
    def _():
        gather(dest_next_ref, 1 - cur)

    for _ in range(2 * TM):
        row_copy(0, cur, 0, 0).wait()

    half = x_ref.shape[1] // 2
    meta = meta_ref[...]
    w1 = meta[:, META_W:META_W + 1]
    w2 = meta[:, META_W + 1:META_W + 2]
    g2 = mod_ref[0][5:6]
    for s in range(SUB):
        lo1, hi1 = _unpack_halves(_load_token_tiles(ybuf.at[cur, 0], TM, s))
        lo2, hi2 = _unpack_halves(_load_token_tiles(ybuf.at[cur, 1], TM, s))
        for base, a, b in ((s * LANE, lo1, lo2), (half + s * LANE, hi1, hi2)):
            cols = slice(base, base + LANE)
            o_ref[:, cols] = x_ref[:, cols] + g2[:, cols] * (w1 * a + w2 * b)


def _moe_combine(X2, ye, meta, dest, modl, tps, tiles_ctx):
    T, D = X2.shape
    steps = T // TM
    tile = pl.BlockSpec((TM, D), lambda i: (i, 0))
    return pl.pallas_call(
        _combine_kernel,
        grid=(steps,),
        in_specs=[pl.BlockSpec((1, 2, TM), lambda i: (i, 0, 0), memory_space=pltpu.SMEM),
                  pl.BlockSpec((1, 2, TM), lambda i: (jnp.minimum(i + 1, steps - 1), 0, 0),
                               memory_space=pltpu.SMEM),
                  tile,
                  pl.BlockSpec((TM, LANE), lambda i: (i, 0)),
                  pl.BlockSpec((1, 6, D), lambda i: (_mod_row(i, tps, tiles_ctx), 0, 0)),
                  pl.BlockSpec(memory_space=pl.ANY)],
        out_specs=tile,
        out_shape=jax.ShapeDtypeStruct((T, D), F32),
        scratch_shapes=[pltpu.VMEM((2, 2, TM * SUB, LANE), jnp.uint32),
                        pltpu.SemaphoreType.DMA((2,))],
        input_output_aliases={2: 0},
        compiler_params=_cparams(("arbitrary",)),
        name="moe_combine",
    )(dest, dest, X2, meta, modl, ye)


def _rope_tables(n):
    rows = n // GRID_W
    row = jnp.repeat(jnp.arange(rows, dtype=F32), GRID_W)
    col = jnp.tile(jnp.arange(GRID_W, dtype=F32), rows)
    half = HEAD_DIM // 2
    inv_freq = 1.0 / (ROPE_THETA ** (jnp.arange(0, half, 2, dtype=F32) / half))
    ar = row[:, None] * inv_freq
    ac = col[:, None] * inv_freq
    cos = jnp.concatenate([jnp.cos(ar), jnp.cos(ar), jnp.cos(ac), jnp.cos(ac)], axis=-1)
    sin = jnp.concatenate([-jnp.sin(ar), jnp.sin(ar), -jnp.sin(ac), jnp.sin(ac)], axis=-1)
    return jnp.tile(cos, (1, 2)), jnp.tile(sin, (1, 2))


def kernel(x, c, ctx, c_ctx, norm1_g, norm2_g, w_mod, b_mod, w_in, w_out, q_norm_g, k_norm_g, attn_sink,
           conv_dw_w, conv_dw_b, conv_ln_g, conv_ln_b, conv_pw_w, conv_pw_b, ssd_conv_w, ssd_conv_b, ssd_a_log,
           ssd_dt_bias, ssd_d, ssd_norm_g, pool_w, pool_scale, moe_group_router, moe_expert_router, moe_w_gate,
           moe_w_up, moe_w_down):
    B, N, D = x.shape
    LC = ctx.shape[1]
    S = LC + N
    L = w_mod.shape[0]
    assert LC % TM == 0 and N % TM == 0 and N % GRID_W == 0 and D == 2 * ROW_WORDS
    tps = S // TM
    tiles_ctx = LC // TM
    T = B * S

    rows = -(-(B + 1) // 8) * 8
    cc = jnp.concatenate([c, c_ctx[None, :], jnp.zeros((rows - B - 1, D), F32)], axis=0)
    mod = _modulation(cc, w_mod, b_mod).reshape(L, rows, 6, D)

    w_in_p = jnp.concatenate([w_in[:, :, :DT_SRC + 16], jnp.zeros((L, D, LANE - 16), F32), w_in[:, :, DT_SRC + 16:]],
                             axis=-1).astype(BF16)
    w_out_b = w_out.astype(BF16)
    w_router = jnp.concatenate([moe_group_router, moe_expert_router,
                                jnp.zeros((L, D, LANE - MOE_GROUPS - MOE_EXPERTS), F32)], axis=-1)
    wr_hi = w_router.astype(BF16)
    wr = jnp.concatenate([wr_hi, (w_router - wr_hi.astype(F32)).astype(BF16)], axis=-1)

    cos, sin = _rope_tables(N)
    bd = (jnp.arange(LANE)[:, None] // HEAD_DIM == jnp.arange(LANE)[None, :] // HEAD_DIM).astype(BF16)
    pad_lane = lambda v: jnp.concatenate([v.reshape(-1), jnp.zeros((LANE - v.size,), F32)]).reshape(1, LANE)

    X = jnp.concatenate([ctx, x], axis=1).reshape(T, D)
    xs = None
    for l in range(L):
        modl = jnp.stack([jnp.broadcast_to(mod[l, B], (B, 6, D)), mod[l, :B]], axis=1).reshape(2 * B, 6, D)
        q, k, v, glu, z, xbc, dt, pin = _in_projection(X, modl, norm1_g[l][None], w_in_p, l, tps, tiles_ctx)
        r3 = lambda a: a.reshape(B, S, a.shape[-1])
        y_att = _attention(r3(q), r3(k), r3(v), cos, sin, jnp.tile(q_norm_g[l], 2)[None], jnp.tile(k_norm_g[l], 2)[None],
                           bd, attn_sink[l], LC, N)
        dww = jnp.concatenate([conv_dw_w[l], jnp.zeros((32 - CONV_K, GROUP_W), F32)], axis=0)
        y_conv = _conformer(r3(glu), dww, conv_dw_b[l][None], conv_ln_g[l][None], conv_ln_b[l][None],
                            conv_pw_w[l].astype(BF16), conv_pw_b[l][None], LC, N)
        y_ssd = _ssd(r3(xbc), r3(dt), r3(z), ssd_conv_w[l], ssd_conv_b[l][None], pad_lane(-jnp.exp(ssd_a_log[l])),
                     pad_lane(ssd_dt_bias[l]), jnp.repeat(ssd_d[l], HEAD_DIM)[None], ssd_norm_g[l][None], LC, N)
        y_pool = _pool(r3(pin), pool_w[l].astype(BF16), pool_scale[l][None], LC, N)
        ys = [a.reshape(T, GROUP_W) for a in (y_att, y_conv, y_ssd, y_pool)]
        last = l == L - 1
        X, h2p, logits = _out_projection(ys, X, modl, norm2_g[l][None], w_out_b, wr, l, tps, tiles_ctx,
                                         latent_only=last)
        n_tiles = (2 * X.shape[0]) // TM + MOE_EXPERTS
        meta, dest, t_exp, t_first, t_active = _route(logits, n_tiles)
        if xs is None or xs.shape[0] != n_tiles * TM * SUB:
            xs = jnp.zeros((n_tiles * TM * SUB, LANE), jnp.uint32)
        xs = _moe_dispatch(h2p, dest, xs)
        ye = _moe_experts(xs, t_exp, t_first, t_active, moe_w_gate, moe_w_up, moe_w_down, l)
        if last:
            X = _moe_combine(X, ye, meta, dest, modl, tps - tiles_ctx, 0)
        else:
            X = _moe_combine(X, ye, meta, dest, modl, tps, tiles_ctx)
    return X.reshape(B, N, D)
```

```python
import functools
import math

import jax
import jax.numpy as jnp
from jax import lax
from jax.experimental import pallas as pl
from jax.experimental.pallas import tpu as pltpu

F32 = jnp.float32
BF16 = jnp.bfloat16

EPS = 1e-6
GRID_W = 64
HEAD_DIM = 64
ATT_HEADS = 8
ATT_KV_HEADS = 2
ATT_REP = ATT_HEADS // ATT_KV_HEADS
ATT_BLOCK = 128
ATT_WINDOW = 128
ROPE_THETA = 10000.0
GROUP_W = 512
CONV_K = 31
CONV_PAD = 16
CONV_ROWS = 128
SSD_HEADS = 8
SSD_GROUPS = 2
SSD_STATE = 128
SSD_CHUNK = 128
SSD_CONV = 4
SSD_PAD = 8
SSD_INNER = 512
SSD_XBC = SSD_INNER + 2 * SSD_GROUPS * SSD_STATE
POOL_SIZES = (2, 4, 8, 16)
POOL_CH = 128
POOL_PAD = 8
MOE_GROUPS = 4
MOE_PER_GROUP = 8
MOE_EXPERTS = 32
D_EXPERT = 512

TM = 256
TE = 512
LANE = 128
NEG = -1e30
VMEM_LIMIT = 56 * 1024 * 1024

IN_COLS = (("q", 0, 512), ("k", 512, 128), ("v", 640, 128), ("glu", 768, 1024), ("z", 1792, 512),
           ("xbc", 2304, 1024), ("dt", 3328, 128), ("pin", 3456, 512))
D_IN_PAD = 3968
DT_SRC = 3328


def _cparams(sem):
    return pltpu.CompilerParams(dimension_semantics=sem, vmem_limit_bytes=VMEM_LIMIT)


def _resident(shape, layer=None):
    nd = len(shape)
    if layer is None:
        return pl.BlockSpec(shape, lambda *_: (0,) * nd, pipeline_mode=pl.Buffered(1))
    return pl.BlockSpec((1,) + tuple(shape), lambda *_: (layer,) + (0,) * nd, pipeline_mode=pl.Buffered(1))


def _dot(a, b):
    return jnp.dot(a, b, preferred_element_type=F32)


def _dot_nt(a, b):
    return lax.dot_general(a, b, (((1,), (1,)), ((), ())), preferred_element_type=F32)


def _split2(x):
    hi = x.astype(BF16)
    lo = (x - hi.astype(F32)).astype(BF16)
    return hi, lo


def _split3(x):
    h1 = x.astype(BF16)
    r1 = x - h1.astype(F32)
    h2 = r1.astype(BF16)
    h3 = (r1 - h2.astype(F32)).astype(BF16)
    return h1, h2, h3


def _silu(x):
    return x * jax.nn.sigmoid(x)


def _mod_kernel(c_ref, w_ref, b_ref, o_ref):
    s = _silu(c_ref[...])
    o_ref[0] = _dot(s.astype(BF16), w_ref[0].astype(BF16)) + b_ref[0]


def _modulation(cc, w_mod, b_mod):
    L, D, D6 = w_mod.shape
    R = cc.shape[0]
    tn = 1024
    return pl.pallas_call(
        _mod_kernel,
        grid=(L, D6 // tn),
        in_specs=[pl.BlockSpec((R, D), lambda l, j: (0, 0)),
                  pl.BlockSpec((1, D, tn), lambda l, j: (l, 0, j)),
                  pl.BlockSpec((1, 1, tn), lambda l, j: (l, 0, j))],
        out_specs=pl.BlockSpec((1, R, tn), lambda l, j: (l, 0, j)),
        out_shape=jax.ShapeDtypeStruct((L, R, D6), F32),
        compiler_params=_cparams(("arbitrary", "arbitrary")),
        name="modulation",
    )(cc, w_mod, b_mod.reshape(L, 1, D6))


def _inproj_kernel(x_ref, mod_ref, g_ref, w_ref, *out_refs):
    x = x_ref[...]
    m = mod_ref[0]
    ms = jnp.mean(x * x, axis=-1, keepdims=True)
    h = x * lax.rsqrt(ms + EPS) * g_ref[...]
    h = h * (1.0 + m[1:2]) + m[0:1]
    hb = h.astype(BF16)
    for ref, (_, a, w) in zip(out_refs, IN_COLS):
        ref[...] = _dot(hb, w_ref[0, :, a:a + w]).astype(ref.dtype)


def _mod_row(i, tps, tiles_ctx):
    return (i // tps) * 2 + jnp.where((i % tps) >= tiles_ctx, 1, 0)


def _in_projection(X2, modl, g1, w_in_p, layer, tps, tiles_ctx):
    T, D = X2.shape
    dts = {"dt": F32}
    out_shape = [jax.ShapeDtypeStruct((T, w), dts.get(n, BF16)) for n, _, w in IN_COLS]
    out_specs = [pl.BlockSpec((TM, w), lambda i: (i, 0)) for _, _, w in IN_COLS]
    return pl.pallas_call(
        _inproj_kernel,
        grid=(T // TM,),
        in_specs=[pl.BlockSpec((TM, D), lambda i: (i, 0)),
                  pl.BlockSpec((1, 6, D), lambda i: (_mod_row(i, tps, tiles_ctx), 0, 0)),
                  _resident((1, D)),
                  _resident((D, D_IN_PAD), layer)],
        out_specs=out_specs,
        out_shape=out_shape,
        compiler_params=_cparams(("parallel",)),
        name="in_projection",
    )(X2, modl, g1, w_in_p)


def _norm_heads(x, g, bd):
    hi, lo = _split2(x * x)
    s = _dot(hi, bd) + _dot(lo, bd)
    return x * lax.rsqrt(s * (1.0 / HEAD_DIM) + EPS) * g


def _rope(x, cos, sin):
    lane = lax.broadcasted_iota(jnp.int32, x.shape, 1)
    sw = jnp.where((lane & 31) < 16, pltpu.roll(x, LANE - 16, 1), pltpu.roll(x, 16, 1))
    return x * cos + sw * sin


def _sink_softmax_pv(parts, sink_col):
    m = sink_col
    for s, _ in parts:
        m = jnp.maximum(m, jnp.max(s, axis=-1, keepdims=True))
    den = jnp.exp(sink_col - m)
    o = None
    for s, v in parts:
        p = jnp.exp(s - m)
        den = den + jnp.sum(p, axis=-1, keepdims=True)
        pv = _dot(p.astype(BF16), v)
        o = pv if o is None else o + pv
    return o / den


def _attn_kernel(sink_ref, q_ref, k_ref, v_ref, cos_ref, sin_ref, qg_ref, kg_ref, bd_ref, o_ref,
                 qh, kh, vh, *, LC, N):
    S = LC + N
    koff = LC + ATT_BLOCK
    bd = bd_ref[...]
    scale = HEAD_DIM ** -0.5
    zpad = jnp.zeros((ATT_BLOCK, HEAD_DIM), BF16)
    for h in range(ATT_KV_HEADS):
        for buf in (kh, vh):
            buf[h, LC:koff, :] = zpad
            buf[h, koff + N:koff + N + ATT_BLOCK, :] = zpad

    rc = TM
    for c0 in range(0, S, rc):
        lat = c0 >= LC
        dst = c0 + ATT_BLOCK if lat else c0
        if lat:
            cos = cos_ref[c0 - LC:c0 - LC + rc, :]
            sin = sin_ref[c0 - LC:c0 - LC + rc, :]
        kn = _norm_heads(k_ref[0, c0:c0 + rc, :].astype(F32), kg_ref[...], bd)
        if lat:
            kn = _rope(kn, cos, sin)
        knb = kn.astype(BF16)
        vv = v_ref[0, c0:c0 + rc, :]
        for h in range(ATT_KV_HEADS):
            kh[h, dst:dst + rc, :] = knb[:, h * HEAD_DIM:(h + 1) * HEAD_DIM]
            vh[h, dst:dst + rc, :] = vv[:, h * HEAD_DIM:(h + 1) * HEAD_DIM]
        for cb in range(ATT_HEADS // 2):
            qn = _norm_heads(q_ref[0, c0:c0 + rc, cb * LANE:(cb + 1) * LANE].astype(F32), qg_ref[...], bd)
            if lat:
                qn = _rope(qn, cos, sin)
            qnb = (qn * scale).astype(BF16)
            qh[2 * cb, c0:c0 + rc, :] = qnb[:, :HEAD_DIM]
            qh[2 * cb + 1, c0:c0 + rc, :] = qnb[:, HEAD_DIM:]

    def sink_column(h, rows):
        grp = lax.broadcasted_iota(jnp.int32, (ATT_REP * rows, 1), 0) // rows
        col = jnp.full((ATT_REP * rows, 1), sink_ref[ATT_REP * h], F32)
        for r in range(1, ATT_REP):
            col = jnp.where(grp == r, sink_ref[ATT_REP * h + r], col)
        return col

    def store_heads(o, h, row0, rows):
        for pr in range(ATT_REP // 2):
            pair = jnp.concatenate([o[(2 * pr) * rows:(2 * pr + 1) * rows],
                                    o[(2 * pr + 1) * rows:(2 * pr + 2) * rows]], axis=1)
            cb = (ATT_REP // 2) * h + pr
            o_ref[0, pl.ds(row0, rows), cb * LANE:(cb + 1) * LANE] = pair.astype(o_ref.dtype)

    for h in range(ATT_KV_HEADS):
        q4 = jnp.concatenate([qh[ATT_REP * h + r, 0:LC, :] for r in range(ATT_REP)], axis=0)
        kc = kh[h, 0:LC, :]
        vc = vh[h, 0:LC, :]
        o = _sink_softmax_pv([(_dot_nt(q4, kc), vc)], sink_column(h, LC))
        store_heads(o, h, 0, LC)

    nk = 3 * ATT_BLOCK
    qi = lax.broadcasted_iota(jnp.int32, (ATT_REP * ATT_BLOCK, nk), 0) & (ATT_BLOCK - 1)
    kj = lax.broadcasted_iota(jnp.int32, (ATT_REP * ATT_BLOCK, nk), 1)
    rel = kj - ATT_BLOCK - qi
    in_window = (rel <= ATT_WINDOW) & (rel >= -ATT_WINDOW)

    def block(j, carry):
        r0 = pl.multiple_of(j * ATT_BLOCK, ATT_BLOCK)
        kpos = kj + (j - 1) * ATT_BLOCK
        ok = in_window & (kpos >= 0) & (kpos < N)
        for h in range(ATT_KV_HEADS):
            q4 = jnp.concatenate([qh[ATT_REP * h + r, pl.ds(LC + r0, ATT_BLOCK), :] for r in range(ATT_REP)],
                                 axis=0)
            kb = kh[h, pl.ds(LC + r0, nk), :]
            vb = vh[h, pl.ds(LC + r0, nk), :]
            s_lat = jnp.where(ok, _dot_nt(q4, kb), NEG)
            s_ctx = _dot_nt(q4, kh[h, 0:LC, :])
            o = _sink_softmax_pv([(s_lat, vb), (s_ctx, vh[h, 0:LC, :])], sink_column(h, ATT_BLOCK))
            store_heads(o, h, LC + r0, ATT_BLOCK)
        return carry

    lax.fori_loop(0, N // ATT_BLOCK, block, 0)


def _attention(q, k, v, cos, sin, qg, kg, bd, sink, LC, N):
    B, S, _ = q.shape
    kern = functools.partial(_attn_kernel, LC=LC, N=N)
    grid_spec = pltpu.PrefetchScalarGridSpec(
        num_scalar_prefetch=1,
        grid=(B,),
        in_specs=[pl.BlockSpec((1, S, 512), lambda b, s: (b, 0, 0)),
                  pl.BlockSpec((1, S, 128), lambda b, s: (b, 0, 0)),
                  pl.BlockSpec((1, S, 128), lambda b, s: (b, 0, 0)),
                  pl.BlockSpec((N, LANE), lambda b, s: (0, 0)),
                  pl.BlockSpec((N, LANE), lambda b, s: (0, 0)),
                  pl.BlockSpec((1, LANE), lambda b, s: (0, 0)),
                  pl.BlockSpec((1, LANE), lambda b, s: (0, 0)),
                  pl.BlockSpec((LANE, LANE), lambda b, s: (0, 0))],
        out_specs=pl.BlockSpec((1, S, 512), lambda b, s: (b, 0, 0)),
        scratch_shapes=[pltpu.VMEM((ATT_HEADS, S, HEAD_DIM), BF16),
                        pltpu.VMEM((ATT_KV_HEADS, S + 2 * ATT_BLOCK, HEAD_DIM), BF16),
                        pltpu.VMEM((ATT_KV_HEADS, S + 2 * ATT_BLOCK, HEAD_DIM), BF16)])
    return pl.pallas_call(
        kern, grid_spec=grid_spec,
        out_shape=jax.ShapeDtypeStruct((B, S, 512), BF16),
        compiler_params=_cparams(("parallel",)),
        name="attention",
    )(sink, q, k, v, cos, sin, qg, kg, bd)


def _conv_kernel(glu_ref, dww_ref, dwb_ref, lng_ref, lnb_ref, pww_ref, pwb_ref, o_ref, hp, rot, *, LC, N):
    C = GROUP_W
    half = CONV_K // 2
    rows = CONV_ROWS

    def stream(s0, ln):
        hp[0:CONV_PAD, :] = jnp.zeros((CONV_PAD, C), F32)
        hp[CONV_PAD + ln:2 * CONV_PAD + ln, :] = jnp.zeros((CONV_PAD, C), F32)

        def fill(i, carry):
            r0 = pl.multiple_of(i * TM, TM)
            g = glu_ref[0, pl.ds(s0 + r0, TM), :]
            a = g[:, :C].astype(F32)
            gate = g[:, C:].astype(F32)
            hp[pl.ds(CONV_PAD + r0, TM), :] = a * jax.nn.sigmoid(gate)
            return carry

        lax.fori_loop(0, ln // TM, fill, 0)

        def chunk(i, carry):
            r0 = pl.multiple_of(i * rows, rows)
            parts = []
            for cb in range(C // LANE):
                win = hp[pl.ds(r0, rows + 2 * CONV_PAD), cb * LANE:(cb + 1) * LANE]
                for ph in range(8):
                    rot[cb, ph] = win[ph:ph + rot.shape[2]]
                acc = jnp.zeros((rows, LANE), F32)
                for t in range(CONV_K):
                    off = CONV_PAD - half + t
                    w = dww_ref[t:t + 1, cb * LANE:(cb + 1) * LANE]
                    acc = acc + w * rot[cb, off % 8, (off // 8) * 8:(off // 8) * 8 + rows, :]
                parts.append(acc)
            y = jnp.concatenate(parts, axis=1) + dwb_ref[...]
            mu = jnp.mean(y, axis=-1, keepdims=True)
            yc = y - mu
            var = jnp.mean(yc * yc, axis=-1, keepdims=True)
            z = yc * lax.rsqrt(var + EPS) * lng_ref[...] + lnb_ref[...]
            z = _silu(z)
            out = _dot(z.astype(BF16), pww_ref[...]) + pwb_ref[...]
            o_ref[0, pl.ds(s0 + r0, rows), :] = out.astype(o_ref.dtype)
            return carry

        lax.fori_loop(0, ln // rows, chunk, 0)

    stream(0, LC)
    stream(LC, N)


def _conformer(glu, dww, dwb, lng, lnb, pww, pwb, LC, N):
    B, S, _ = glu.shape
    C = GROUP_W
    kern = functools.partial(_conv_kernel, LC=LC, N=N)
    vec = pl.BlockSpec((1, C), lambda b: (0, 0))
    return pl.pallas_call(
        kern, grid=(B,),
        in_specs=[pl.BlockSpec((1, S, 2 * C), lambda b: (b, 0, 0)),
                  pl.BlockSpec((32, C), lambda b: (0, 0)),
                  vec, vec, vec,
                  pl.BlockSpec((C, C), lambda b: (0, 0)),
                  vec],
        out_specs=pl.BlockSpec((1, S, C), lambda b: (b, 0, 0)),
        out_shape=jax.ShapeDtypeStruct((B, S, C), BF16),
        scratch_shapes=[pltpu.VMEM((max(LC, N) + 2 * CONV_PAD, C), F32),
                        pltpu.VMEM((C // LANE, 8, CONV_ROWS + 2 * CONV_PAD - 8, LANE), F32)],
        compiler_params=_cparams(("parallel",)),
        name="conformer_conv",
    )(glu, dww, dwb, lng, lnb, pww, pwb)


def _ssd_kernel(xbc_ref, dt_ref, z_ref, cw_ref, cb_ref, arow_ref, dtb_ref, dsk_ref, ng_ref, o_ref,
                xp, xs, bm, cm, dts, yacc, state, *, LC, N):
    S = LC + N
    Q = SSD_CHUNK
    nc = S // Q
    nc_ctx = LC // Q
    ii = lax.broadcasted_iota(jnp.int32, (Q, Q), 0)
    jj = lax.broadcasted_iota(jnp.int32, (Q, Q), 1)
    lower = ii >= jj
    tri = (jnp.where(lower, 1.0, 0.0).astype(BF16), jnp.where(jj >= ii, 1.0, 0.0).astype(BF16))
    causal = (lower, jj >= ii)
    first_half = lax.broadcasted_iota(jnp.int32, (Q, LANE), 1) < SSD_STATE // 2

    def stream(s0, ln):
        xp[0:SSD_PAD, :] = jnp.zeros((SSD_PAD, SSD_XBC), F32)
        xp[SSD_PAD + ln:2 * SSD_PAD + ln, :] = jnp.zeros((SSD_PAD, SSD_XBC), F32)

        def fill(i, carry):
            r0 = pl.multiple_of(i * Q, Q)
            xp[pl.ds(SSD_PAD + r0, Q), :] = xbc_ref[0, pl.ds(s0 + r0, Q), :].astype(F32)
            return carry

        lax.fori_loop(0, ln // Q, fill, 0)

        def conv(i, carry):
            r0 = pl.multiple_of(i * Q, Q)
            for cb in range(SSD_XBC // LANE):
                cs = slice(cb * LANE, (cb + 1) * LANE)
                win = xp[pl.ds(r0, Q + 2 * SSD_PAD), cs]
                acc = jnp.zeros((Q, LANE), F32)
                for t in range(SSD_CONV):
                    acc = acc + cw_ref[t:t + 1, cs] * win[SSD_PAD - 2 + t:SSD_PAD - 2 + t + Q]
                y = _silu(acc + cb_ref[:, cs])
                if cb < SSD_INNER // LANE:
                    xs[pl.ds(s0 + r0, Q), cs] = y
                elif cb < (SSD_INNER + SSD_GROUPS * SSD_STATE) // LANE:
                    c2 = cb - SSD_INNER // LANE
                    bm[pl.ds(s0 + r0, Q), c2 * LANE:(c2 + 1) * LANE] = y
                else:
                    c2 = cb - (SSD_INNER + SSD_GROUPS * SSD_STATE) // LANE
                    cm[pl.ds(s0 + r0, Q), c2 * LANE:(c2 + 1) * LANE] = y
            return carry

        lax.fori_loop(0, ln // Q, conv, 0)

    stream(0, LC)
    stream(LC, N)

    def softplus_rows(i, carry):
        r0 = pl.multiple_of(i * Q, Q)
        v = dt_ref[0, pl.ds(r0, Q), :] + dtb_ref[...]
        dts[pl.ds(r0, Q), :] = jnp.maximum(v, 0.0) + jnp.log(1.0 + jnp.exp(-jnp.abs(v)))
        return carry

    lax.fori_loop(0, nc, softplus_rows, 0)

    def chunk_step(c, d):
        r0 = pl.multiple_of(c * Q, Q)
        dtc = dts[pl.ds(r0, Q), :]
        a1, a2, a3 = _split3(dtc * arow_ref[...])
        cum = _dot(tri[d], a1) + _dot(tri[d], a2) + _dot(tri[d], a3)
        cum_t = cum.T
        dt_t = dtc.T
        tot = cum[Q - 1:Q, :] if d == 0 else cum[0:1, :]
        e_cum = jnp.exp(cum)
        e_tot = jnp.exp(tot)
        w_state = jnp.exp(tot - cum) * dtc
        for g in range(SSD_GROUPS):
            bg = bm[pl.ds(r0, Q), g * SSD_STATE:(g + 1) * SSD_STATE]
            cg = cm[pl.ds(r0, Q), g * SSD_STATE:(g + 1) * SSD_STATE].astype(BF16)
            bg_t = bg.T.astype(BF16)
            cb_mat = _dot(cg, bg_t)
            for pr in range(SSD_HEADS // SSD_GROUPS // 2):
                pair = g * 2 + pr
                lanes = slice(pair * LANE, (pair + 1) * LANE)
                xpair = xs[pl.ds(r0, Q), lanes]
                y = None
                for sub in range(2):
                    hd = d * SSD_HEADS + pair * 2 + sub
                    seg = cum[:, hd:hd + 1] - cum_t[hd:hd + 1, :]
                    dec = jnp.where(causal[d], jnp.exp(jnp.minimum(seg, 0.0)), 0.0)
                    w = (cb_mat * dec * dt_t[hd:hd + 1, :]).astype(BF16)
                    xm = jnp.where(first_half if sub == 0 else ~first_half, xpair, 0.0).astype(BF16)
                    t = _dot(w, xm)
                    y = t if y is None else y + t
                h0 = d * SSD_HEADS + pair * 2
                sel = lambda m: jnp.where(first_half, m[:, h0:h0 + 1], m[:, h0 + 1:h0 + 2])
                st = state[d * (SSD_HEADS // 2) + pair]
                y = y + _dot(cg, st.astype(BF16)) * sel(e_cum)
                xw = (xpair * sel(w_state)).astype(BF16)
                state[d * (SSD_HEADS // 2) + pair] = st * sel(e_tot) + _dot(bg_t, xw)
                if d == 0:
                    yacc[pl.ds(r0, Q), lanes] = y
                else:
                    yacc[pl.ds(r0, Q), lanes] = yacc[pl.ds(r0, Q), lanes] + y

    state[...] = jnp.zeros(state.shape, F32)

    def fwd(c, carry):
        chunk_step(c, 0)
        return carry

    lax.fori_loop(0, nc, fwd, 0)

    def bwd_ctx(i, carry):
        chunk_step(nc_ctx - 1 - i, 1)
        return carry

    lax.fori_loop(0, nc_ctx, bwd_ctx, 0)

    def bwd_lat(i, carry):
        chunk_step(nc - 1 - i, 1)
        return carry

    lax.fori_loop(0, nc - nc_ctx, bwd_lat, 0)

    def gate_out(i, carry):
        r0 = pl.multiple_of(i * Q, Q)
        y = yacc[pl.ds(r0, Q), :] + xs[pl.ds(r0, Q), :] * dsk_ref[...]
        y = y * _silu(z_ref[0, pl.ds(r0, Q), :].astype(F32))
        ms = jnp.mean(y * y, axis=-1, keepdims=True)
        o_ref[0, pl.ds(r0, Q), :] = (y * lax.rsqrt(ms + EPS) * ng_ref[...]).astype(o_ref.dtype)
        return carry

    lax.fori_loop(0, nc, gate_out, 0)


def _ssd(xbc, dt, z, cw, cb, arow, dtb, dsk, ng, LC, N):
    B, S, _ = xbc.shape
    kern = functools.partial(_ssd_kernel, LC=LC, N=N)
    row = lambda w: pl.BlockSpec((1, w), lambda b: (0, 0))
    return pl.pallas_call(
        kern, grid=(B,),
        in_specs=[pl.BlockSpec((1, S, SSD_XBC), lambda b: (b, 0, 0)),
                  pl.BlockSpec((1, S, LANE), lambda b: (b, 0, 0)),
                  pl.BlockSpec((1, S, SSD_INNER), lambda b: (b, 0, 0)),
                  pl.BlockSpec((SSD_CONV, SSD_XBC), lambda b: (0, 0)),
                  row(SSD_XBC), row(LANE), row(LANE), row(SSD_INNER), row(SSD_INNER)],
        out_specs=pl.BlockSpec((1, S, SSD_INNER), lambda b: (b, 0, 0)),
        out_shape=jax.ShapeDtypeStruct((B, S, SSD_INNER), BF16),
        scratch_shapes=[pltpu.VMEM((max(LC, N) + 2 * SSD_PAD, SSD_XBC), F32),
                        pltpu.VMEM((S, SSD_INNER), F32),
                        pltpu.VMEM((S, SSD_GROUPS * SSD_STATE), F32),
                        pltpu.VMEM((S, SSD_GROUPS * SSD_STATE), F32),
                        pltpu.VMEM((S, LANE), F32),
                        pltpu.VMEM((S, SSD_INNER), F32),
                        pltpu.VMEM((SSD_HEADS, SSD_STATE, LANE), F32)],
        compiler_params=_cparams(("parallel",)),
        name="ssd",
    )(xbc, dt, z, cw, cb, arow, dtb, dsk, ng)


def _pool_kernel(pin_ref, pw_ref, ps_ref, o_ref, up, *, LC, N):
    C = GROUP_W
    rows = TM

    def stream(s0, ln):
        up[0:POOL_PAD, :] = jnp.zeros((POOL_PAD, C), F32)
        up[POOL_PAD + ln:2 * POOL_PAD + ln, :] = jnp.zeros((POOL_PAD, C), F32)

        def fill(i, carry):
            r0 = pl.multiple_of(i * rows, rows)
            up[pl.ds(POOL_PAD + r0, rows), :] = pin_ref[0, pl.ds(s0 + r0, rows), :].astype(F32)
            return carry

        lax.fori_loop(0, ln // rows, fill, 0)

        def chunk(i, carry):
            r0 = pl.multiple_of(i * rows, rows)
            t = r0 + lax.broadcasted_iota(jnp.int32, (rows, 1), 0)
            outs = []
            for gi, w in enumerate(POOL_SIZES):
                cs = slice(gi * POOL_CH, (gi + 1) * POOL_CH)
                win = up[pl.ds(r0, rows + 2 * POOL_PAD), cs]
                acc = jnp.zeros((rows, POOL_CH), F32)
                for d in range(-(w // 2), w - w // 2):
                    acc = acc + win[POOL_PAD + d:POOL_PAD + d + rows]
                cnt = jnp.minimum(t + (w - w // 2), ln) - jnp.maximum(t - w // 2, 0)
                p = acc / cnt.astype(F32) - win[POOL_PAD:POOL_PAD + rows]
                outs.append(_dot(p.astype(BF16), pw_ref[gi]))
            y = jnp.concatenate(outs, axis=1) * ps_ref[...]
            o_ref[0, pl.ds(s0 + r0, rows), :] = y.astype(o_ref.dtype)
            return carry

        lax.fori_loop(0, ln // rows, chunk, 0)

    stream(0, LC)
    stream(LC, N)


def _pool(pin, pw, ps, LC, N):
    B, S, C = pin.shape
    kern = functools.partial(_pool_kernel, LC=LC, N=N)
    return pl.pallas_call(
        kern, grid=(B,),
        in_specs=[pl.BlockSpec((1, S, C), lambda b: (b, 0, 0)),
                  pl.BlockSpec((len(POOL_SIZES), POOL_CH, POOL_CH), lambda b: (0, 0, 0)),
                  pl.BlockSpec((1, C), lambda b: (0, 0))],
        out_specs=pl.BlockSpec((1, S, C), lambda b: (b, 0, 0)),
        out_shape=jax.ShapeDtypeStruct((B, S, C), BF16),
        scratch_shapes=[pltpu.VMEM((max(LC, N) + 2 * POOL_PAD, C), F32)],
        compiler_params=_cparams(("parallel",)),
        name="pool_mixer",
    )(pin, pw, ps)


def _pack_halves(x):
    w = x.shape[1] // 2
    u = lax.bitcast_convert_type(x.astype(BF16).astype(F32), jnp.uint32)
    return (u[:, :w] >> 16) | (u[:, w:] & jnp.uint32(0xFFFF0000))


def _unpack_halves(p):
    lo = lax.bitcast_convert_type(p << 16, F32)
    hi = lax.bitcast_convert_type(p & jnp.uint32(0xFFFF0000), F32)
    return lo, hi


SUB = 8
ROW_WORDS = SUB * LANE


def _store_token_tiles(ref, packed):
    rows = packed.shape[0]
    for s in range(SUB):
        ref[pl.ds(s, rows, stride=SUB), :] = packed[:, s * LANE:(s + 1) * LANE]


def _load_token_tiles(ref, rows, s):
    return ref[pl.ds(s, rows, stride=SUB), :]


def _outproj_kernel(ya_ref, yb_ref, yc_ref, yd_ref, x_ref, mod_ref, g_ref, w_ref, wr_ref,
                    xo_ref, h2_ref, lg_ref):
    acc = None
    for i, ref in enumerate((ya_ref, yb_ref, yc_ref, yd_ref)):
        t = _dot(ref[...], w_ref[0, i * GROUP_W:(i + 1) * GROUP_W, :])
        acc = t if acc is None else acc + t
    m = mod_ref[0]
    x = x_ref[...] + m[2:3] * acc
    xo_ref[...] = x
    ms = jnp.mean(x * x, axis=-1, keepdims=True)
    h2 = x * lax.rsqrt(ms + EPS) * g_ref[...]
    h2 = h2 * (1.0 + m[4:5]) + m[3:4]
    _store_token_tiles(h2_ref, _pack_halves(h2))
    h_hi, h_lo = _split2(h2)
    both = _dot(h_hi, wr_ref[0])
    lg_ref[...] = both[:, :LANE] + (both[:, LANE:] + _dot(h_lo, wr_ref[0, :, :LANE]))


def _out_projection(ys, X2, modl, g2, w_out, wr, layer, tps, tiles_ctx, latent_only=False):
    T, D = X2.shape
    if latent_only:
        tl = tps - tiles_ctx
        steps = (T // TM) // tps * tl
        src = lambda i: ((i // tl) * tps + tiles_ctx + i % tl, 0)
        mrow = lambda i: ((i // tl) * 2 + 1, 0, 0)
        aliases = {}
    else:
        steps = T // TM
        src = lambda i: (i, 0)
        mrow = lambda i: (_mod_row(i, tps, tiles_ctx), 0, 0)
        aliases = {4: 0}
    rows = steps * TM
    ytile = pl.BlockSpec((TM, GROUP_W), src)
    return pl.pallas_call(
        _outproj_kernel,
        grid=(steps,),
        in_specs=[ytile, ytile, ytile, ytile,
                  pl.BlockSpec((TM, D), src),
                  pl.BlockSpec((1, 6, D), mrow),
                  _resident((1, D)),
                  _resident((4 * GROUP_W, D), layer),
                  _resident((D, 2 * LANE), layer)],
        out_specs=[pl.BlockSpec((TM, D), lambda i: (i, 0)),
                   pl.BlockSpec((TM * SUB, LANE), lambda i: (i, 0)),
                   pl.BlockSpec((TM, LANE), lambda i: (i, 0))],
        out_shape=[jax.ShapeDtypeStruct((rows, D), F32),
                   jax.ShapeDtypeStruct((rows * SUB, LANE), jnp.uint32),
                   jax.ShapeDtypeStruct((rows, LANE), F32)],
        input_output_aliases=aliases,
        compiler_params=_cparams(("parallel",)),
        name="out_projection",
    )(*ys, X2, modl, g2, w_out, wr)


META_E = 0
META_R = 2
META_W = 4


def _route_kernel(lg_ref, meta_ref, cnt_ref, carry):
    @pl.when(pl.program_id(0) == 0)
    def _():
        carry[...] = jnp.zeros(carry.shape, F32)

    lg = lg_ref[...]
    lane = lax.broadcasted_iota(jnp.int32, lg.shape, 1).astype(F32)
    big = 1e9
    rmax = lambda m: jnp.max(jnp.where(m, lg, NEG), axis=-1, keepdims=True)
    first = lambda m: jnp.min(jnp.where(m, lane, big), axis=-1, keepdims=True)

    gm = lane < MOE_GROUPS
    gmax = rmax(gm)
    gidx = first(gm & (lg == gmax))
    g_w = 1.0 / jnp.sum(jnp.where(gm, jnp.exp(lg - gmax), 0.0), axis=-1, keepdims=True)
    lo = MOE_GROUPS + MOE_PER_GROUP * gidx
    em = (lane >= lo) & (lane < lo + MOE_PER_GROUP)
    v1 = rmax(em)
    i1 = first(em & (lg == v1))
    em2 = em & (lane != i1)
    v2 = rmax(em2)
    i2 = first(em2 & (lg == v2))
    t = jnp.exp(v2 - v1)
    w1 = g_w / (1.0 + t)
    w2 = g_w * t / (1.0 + t)

    oh1 = jnp.where(lane == i1, 1.0, 0.0)
    oh2 = jnp.where(lane == i2, 1.0, 0.0)
    oh = oh1 + oh2
    rows = lg.shape[0]
    ri = lax.broadcasted_iota(jnp.int32, (rows, rows), 0)
    ci = lax.broadcasted_iota(jnp.int32, (rows, rows), 1)
    before = jnp.where(ci < ri, 1.0, 0.0).astype(BF16)
    base = _dot(before, oh.astype(BF16)) + carry[0:1, :]
    r1 = jnp.sum(oh1 * base, axis=-1, keepdims=True)
    r2 = jnp.sum(oh2 * base, axis=-1, keepdims=True)
    carry[0:1, :] = carry[0:1, :] + jnp.sum(oh, axis=0, keepdims=True)

    meta = jnp.zeros(lg.shape, F32)
    for k, val in enumerate((i1 - MOE_GROUPS, i2 - MOE_GROUPS, r1, r2, w1, w2)):
        meta = jnp.where(lane == k, val, meta)
    meta_ref[...] = meta
    cnt_ref[...] = jnp.broadcast_to(carry[0:1, :], cnt_ref.shape)


def _route(logits, n_tiles):
    T = logits.shape[0]
    rt = max(r for r in (4 * TM, 2 * TM, TM) if T % r == 0)
    meta, cnt = pl.pallas_call(
        _route_kernel,
        grid=(T // rt,),
        in_specs=[pl.BlockSpec((rt, LANE), lambda i: (i, 0))],
        out_specs=[pl.BlockSpec((rt, LANE), lambda i: (i, 0)),
                   pl.BlockSpec((8, LANE), lambda i: (0, 0))],
        out_shape=[jax.ShapeDtypeStruct((T, LANE), F32), jax.ShapeDtypeStruct((8, LANE), F32)],
        scratch_shapes=[pltpu.VMEM((8, LANE), F32)],
        compiler_params=_cparams(("arbitrary",)),
        name="moe_route",
    )(logits)
    routed = meta[:, :4].astype(jnp.int32)
    counts = cnt[0, MOE_GROUPS:MOE_GROUPS + MOE_EXPERTS].astype(jnp.int32)
    ntile = (counts + TE - 1) // TE
    tile_end = jnp.cumsum(ntile)
    tile_start = (tile_end - ntile).astype(jnp.int32)
    total = tile_end[-1]
    tiles = jnp.arange(n_tiles, dtype=jnp.int32)
    active = tiles < total
    last = jnp.minimum(tiles, total - 1)
    t_exp = jnp.sum((last[:, None] >= tile_end[None, :]).astype(jnp.int32), axis=1)
    t_first = (active & (tiles == jnp.sum(jnp.where(t_exp[:, None] == jnp.arange(MOE_EXPERTS)[None, :],
                                                     tile_start[None, :], 0), axis=1))).astype(jnp.int32)
    experts = jnp.arange(MOE_EXPERTS, dtype=jnp.int32)
    first_tile = jnp.sum(jnp.where(routed[:, META_E:META_E + 2, None] == experts, tile_start, 0), axis=-1)
    dest = (first_tile * TE + routed[:, META_R:META_R + 2]) * SUB
    dest = dest.astype(jnp.int32).reshape(T // TM, TM, 2).transpose(0, 2, 1)
    return meta, dest, t_exp, t_first, active.astype(jnp.int32)


def _dispatch_kernel(dest_ref, h_ref, xs_in_ref, xs_ref, sem):
    del xs_in_ref

    def row_copy(r, dst):
        return pltpu.make_async_copy(h_ref.at[pl.ds(pl.multiple_of(r * SUB, SUB), SUB)],
                                     xs_ref.at[pl.ds(pl.multiple_of(dst, SUB), SUB)], sem)

    def start(r, carry):
        for slot in range(2):
            row_copy(r, dest_ref[0, slot, r]).start(priority=slot)
        return carry

    lax.fori_loop(0, TM, start, 0, unroll=8)
    for _ in range(2 * TM):
        row_copy(0, 0).wait()


def _moe_dispatch(h2p, dest, xs_init):
    T = h2p.shape[0] // SUB
    return pl.pallas_call(
        _dispatch_kernel,
        grid=(T // TM,),
        in_specs=[pl.BlockSpec((1, 2, TM), lambda i: (i, 0, 0), memory_space=pltpu.SMEM),
                  pl.BlockSpec((TM * SUB, LANE), lambda i: (i, 0)),
                  pl.BlockSpec(memory_space=pl.ANY)],
        out_specs=pl.BlockSpec(memory_space=pl.ANY),
        out_shape=jax.ShapeDtypeStruct(xs_init.shape, jnp.uint32),
        scratch_shapes=[pltpu.SemaphoreType.DMA(())],
        input_output_aliases={2: 0},
        compiler_params=_cparams(("arbitrary",)),
        name="moe_dispatch",
    )(dest, h2p, xs_init)


def _moe_kernel(te_ref, tf_ref, tv_ref, x_ref, wg_ref, wu_ref, wd_ref, o_ref, wg_b, wu_b, wd_b):
    i = pl.program_id(0)

    @pl.when(tf_ref[i] == 1)
    def _():
        wg_b[...] = wg_ref[0, 0].astype(BF16)
        wu_b[...] = wu_ref[0, 0].astype(BF16)
        wd_b[...] = wd_ref[0, 0].astype(BF16)

    @pl.when(tv_ref[i] == 1)
    def _():
        words = jnp.concatenate([_load_token_tiles(x_ref, TE, s) for s in range(SUB)], axis=1)
        lo, hi = _unpack_halves(words)
        lo = lo.astype(BF16)
        hi = hi.astype(BF16)
        half = lo.shape[1]
        g = _dot(lo, wg_b[:half, :]) + _dot(hi, wg_b[half:, :])
        u = _dot(lo, wu_b[:half, :]) + _dot(hi, wu_b[half:, :])
        hid = (_silu(g) * u).astype(BF16)
        _store_token_tiles(o_ref, _pack_halves(_dot(hid, wd_b[...])))

    @pl.when(tv_ref[i] == 0)
    def _():
        o_ref[...] = jnp.zeros(o_ref.shape, o_ref.dtype)


def _moe_experts(xs, t_exp, t_first, t_active, w_gate, w_up, w_down, layer):
    R = xs.shape[0] // SUB
    D = 2 * ROW_WORDS
    n_tiles = R // TE
    wmap = lambda i, te, tf, tv: (layer, te[i], 0, 0)
    grid_spec = pltpu.PrefetchScalarGridSpec(
        num_scalar_prefetch=3,
        grid=(n_tiles,),
        in_specs=[pl.BlockSpec((TE * SUB, LANE), lambda i, te, tf, tv: (i, 0)),
                  pl.BlockSpec((1, 1, D, D_EXPERT), wmap),
                  pl.BlockSpec((1, 1, D, D_EXPERT), wmap),
                  pl.BlockSpec((1, 1, D_EXPERT, D), wmap)],
        out_specs=pl.BlockSpec((TE * SUB, LANE), lambda i, te, tf, tv: (i, 0)),
        scratch_shapes=[pltpu.VMEM((D, D_EXPERT), BF16),
                        pltpu.VMEM((D, D_EXPERT), BF16),
                        pltpu.VMEM((D_EXPERT, D), BF16)])
    return pl.pallas_call(
        _moe_kernel, grid_spec=grid_spec,
        out_shape=jax.ShapeDtypeStruct((R * SUB, LANE), jnp.uint32),
        compiler_params=_cparams(("arbitrary",)),
        name="moe_experts",
    )(t_exp, t_first, t_active, xs, w_gate, w_up, w_down)


def _combine_kernel(dest_ref, dest_next_ref, x_ref, meta_ref, mod_ref, ye_ref, o_ref, ybuf, sems):
    i = pl.program_id(0)
    n = pl.num_programs(0)
    cur = i % 2

    def row_copy(src, buf, slot, r):
        return pltpu.make_async_copy(ye_ref.at[pl.ds(pl.multiple_of(src, SUB), SUB)],
                                     ybuf.at[buf, slot, pl.ds(pl.multiple_of(r * SUB, SUB), SUB)], sems.at[buf])

    def gather(iref, buf):
        def start(r, carry):
            for slot in range(2):
                row_copy(iref[0, slot, r], buf, slot, r).start(priority=slot)
            return carry

        lax.fori_loop(0, TM, start, 0, unroll=8)

    @pl.when(i == 0)
    def _():
        gather(dest_ref, 0)

    @pl.when(i + 1 < n)
    def _():
        gather(dest_next_ref, 1 - cur)

    for _ in range(2 * TM):
        row_copy(0, cur, 0, 0).wait()

    half = x_ref.shape[1] // 2
    meta = meta_ref[...]
    w1 = meta[:, META_W:META_W + 1]
    w2 = meta[:, META_W + 1:META_W + 2]
    g2 = mod_ref[0][5:6]
    for s in range(SUB):
        lo1, hi1 = _unpack_halves(_load_token_tiles(ybuf.at[cur, 0], TM, s))
        lo2, hi2 = _unpack_halves(_load_token_tiles(ybuf.at[cur, 1], TM, s))
        for base, a, b in ((s * LANE, lo1, lo2), (half + s * LANE, hi1, hi2)):
            cols = slice(base, base + LANE)
            o_ref[:, cols] = x_ref[:, cols] + g2[:, cols] * (w1 * a + w2 * b)


def _moe_combine(X2, ye, meta, dest, modl, tps, tiles_ctx):
    T, D = X2.shape
    steps = T // TM
    tile = pl.BlockSpec((TM, D), lambda i: (i, 0))
    return pl.pallas_call(
        _combine_kernel,
        grid=(steps,),
        in_specs=[pl.BlockSpec((1, 2, TM), lambda i: (i, 0, 0), memory_space=pltpu.SMEM),
                  pl.BlockSpec((1, 2, TM), lambda i: (jnp.minimum(i + 1, steps - 1), 0, 0),
                               memory_space=pltpu.SMEM),
                  tile,
                  pl.BlockSpec((TM, LANE), lambda i: (i, 0)),
                  pl.BlockSpec((1, 6, D), lambda i: (_mod_row(i, tps, tiles_ctx), 0, 0)),
                  pl.BlockSpec(memory_space=pl.ANY)],
        out_specs=tile,
        out_shape=jax.ShapeDtypeStruct((T, D), F32),
        scratch_shapes=[pltpu.VMEM((2, 2, TM * SUB, LANE), jnp.uint32),
                        pltpu.SemaphoreType.DMA((2,))],
        input_output_aliases={2: 0},
        compiler_params=_cparams(("arbitrary",)),
        name="moe_combine",
    )(dest, dest, X2, meta, modl, ye)


def _rope_tables(n):
    rows = n // GRID_W
    row = jnp.repeat(jnp.arange(rows, dtype=F32), GRID_W)
    col = jnp.tile(jnp.arange(GRID_W, dtype=F32), rows)
    half = HEAD_DIM // 2
    inv_freq = 1.0 / (ROPE_THETA ** (jnp.arange(0, half, 2, dtype=F32) / half))
    ar = row[:, None] * inv_freq
    ac = col[:, None] * inv_freq
    cos = jnp.concatenate([jnp.cos(ar), jnp.cos(ar), jnp.cos(ac), jnp.cos(ac)], axis=-1)
    sin = jnp.concatenate([-jnp.sin(ar), jnp.sin(ar), -jnp.sin(ac), jnp.sin(ac)], axis=-1)
    return jnp.tile(cos, (1, 2)), jnp.tile(sin, (1, 2))


def kernel(x, c, ctx, c_ctx, norm1_g, norm2_g, w_mod, b_mod, w_in, w_out, q_norm_g, k_norm_g, attn_sink,
           conv_dw_w, conv_dw_b, conv_ln_g, conv_ln_b, conv_pw_w, conv_pw_b, ssd_conv_w, ssd_conv_b, ssd_a_log,
           ssd_dt_bias, ssd_d, ssd_norm_g, pool_w, pool_scale, moe_group_router, moe_expert_router, moe_w_gate,
           moe_w_up, moe_w_down):
    B, N, D = x.shape
    LC = ctx.shape[1]
    S = LC + N
    L = w_mod.shape[0]
    assert LC % TM == 0 and N % TM == 0 and N % GRID_W == 0 and D == 2 * ROW_WORDS
    tps = S // TM
    tiles_ctx = LC // TM
    T = B * S

    rows = -(-(B + 1) // 8) * 8
    cc = jnp.concatenate([c, c_ctx[None, :], jnp.zeros((rows - B - 1, D), F32)], axis=0)
    mod = _modulation(cc, w_mod, b_mod).reshape(L, rows, 6, D)

    w_in_p = jnp.concatenate([w_in[:, :, :DT_SRC + 16], jnp.zeros((L, D, LANE - 16), F32), w_in[:, :, DT_SRC + 16:]],
                             axis=-1).astype(BF16)
    w_out_b = w_out.astype(BF16)
    w_router = jnp.concatenate([moe_group_router, moe_expert_router,
                                jnp.zeros((L, D, LANE - MOE_GROUPS - MOE_EXPERTS), F32)], axis=-1)
    wr_hi = w_router.astype(BF16)
    wr = jnp.concatenate([wr_hi, (w_router - wr_hi.astype(F32)).astype(BF16)], axis=-1)

    cos, sin = _rope_tables(N)
    bd = (jnp.arange(LANE)[:, None] // HEAD_DIM == jnp.arange(LANE)[None, :] // HEAD_DIM).astype(BF16)
    pad_lane = lambda v: jnp.concatenate([v.reshape(-1), jnp.zeros((LANE - v.size,), F32)]).reshape(1, LANE)

    X = jnp.concatenate([ctx, x], axis=1).reshape(T, D)
    xs = None
    for l in range(L):
        modl = jnp.stack([jnp.broadcast_to(mod[l, B], (B, 6, D)), mod[l, :B]], axis=1).reshape(2 * B, 6, D)
        q, k, v, glu, z, xbc, dt, pin = _in_projection(X, modl, norm1_g[l][None], w_in_p, l, tps, tiles_ctx)
        r3 = lambda a: a.reshape(B, S, a.shape[-1])
        y_att = _attention(r3(q), r3(k), r3(v), cos, sin, jnp.tile(q_norm_g[l], 2)[None], jnp.tile(k_norm_g[l], 2)[None],
                           bd, attn_sink[l], LC, N)
        dww = jnp.concatenate([conv_dw_w[l], jnp.zeros((32 - CONV_K, GROUP_W), F32)], axis=0)
        y_conv = _conformer(r3(glu), dww, conv_dw_b[l][None], conv_ln_g[l][None], conv_ln_b[l][None],
                            conv_pw_w[l].astype(BF16), conv_pw_b[l][None], LC, N)
        y_ssd = _ssd(r3(xbc), r3(dt), r3(z), ssd_conv_w[l], ssd_conv_b[l][None], pad_lane(-jnp.exp(ssd_a_log[l])),
                     pad_lane(ssd_dt_bias[l]), jnp.repeat(ssd_d[l], HEAD_DIM)[None], ssd_norm_g[l][None], LC, N)
        y_pool = _pool(r3(pin), pool_w[l].astype(BF16), pool_scale[l][None], LC, N)
        ys = [a.reshape(T, GROUP_W) for a in (y_att, y_conv, y_ssd, y_pool)]
        last = l == L - 1
        X, h2p, logits = _out_projection(ys, X, modl, norm2_g[l][None], w_out_b, wr, l, tps, tiles_ctx,
                                         latent_only=last)
        n_tiles = (2 * X.shape[0]) // TE + MOE_EXPERTS
        meta, dest, t_exp, t_first, t_active = _route(logits, n_tiles)
        if xs is None or xs.shape[0] != n_tiles * TE * SUB:
            xs = jnp.zeros((n_tiles * TE * SUB, LANE), jnp.uint32)
        xs = _moe_dispatch(h2p, dest, xs)
        ye = _moe_experts(xs, t_exp, t_first, t_active, moe_w_gate, moe_w_up, moe_w_down, l)
        if last:
            X = _moe_combine(X, ye, meta, dest, modl, tps - tiles_ctx, 0)
        else:
            X = _moe_combine(X, ye, meta, dest, modl, tps, tiles_ctx)
    return X.reshape(B, N, D)
```

```python
import functools
import math

import jax
import jax.numpy as jnp
from jax import lax
from jax.experimental import pallas as pl
from jax.experimental.pallas import tpu as pltpu

F32 = jnp.float32
BF16 = jnp.bfloat16

EPS = 1e-6
GRID_W = 64
HEAD_DIM = 64
ATT_HEADS = 8
ATT_KV_HEADS = 2
ATT_REP = ATT_HEADS // ATT_KV_HEADS
ATT_BLOCK = 128
ATT_WINDOW = 128
ROPE_THETA = 10000.0
GROUP_W = 512
CONV_K = 31
CONV_PAD = 16
CONV_ROWS = 128
SSD_HEADS = 8
SSD_GROUPS = 2
SSD_STATE = 128
SSD_CHUNK = 128
SSD_CONV = 4
SSD_PAD = 8
SSD_INNER = 512
SSD_XBC = SSD_INNER + 2 * SSD_GROUPS * SSD_STATE
POOL_SIZES = (2, 4, 8, 16)
POOL_CH = 128
POOL_PAD = 8
MOE_GROUPS = 4
MOE_PER_GROUP = 8
MOE_EXPERTS = 32
D_EXPERT = 512

TM = 256
TE = 512
LANE = 128
NEG = -1e30
VMEM_LIMIT = 56 * 1024 * 1024

IN_COLS = (("q", 0, 512), ("k", 512, 128), ("v", 640, 128), ("glu", 768, 1024), ("z", 1792, 512),
           ("xbc", 2304, 1024), ("dt", 3328, 128), ("pin", 3456, 512))
D_IN_PAD = 3968
DT_SRC = 3328


def _cparams(sem):
    return pltpu.CompilerParams(dimension_semantics=sem, vmem_limit_bytes=VMEM_LIMIT)


def _resident(shape, layer=None):
    nd = len(shape)
    if layer is None:
        return pl.BlockSpec(shape, lambda *_: (0,) * nd, pipeline_mode=pl.Buffered(1))
    return pl.BlockSpec((1,) + tuple(shape), lambda *_: (layer,) + (0,) * nd, pipeline_mode=pl.Buffered(1))


def _dot(a, b):
    return jnp.dot(a, b, preferred_element_type=F32)


def _dot_nt(a, b):
    return lax.dot_general(a, b, (((1,), (1,)), ((), ())), preferred_element_type=F32)


def _split2(x):
    hi = x.astype(BF16)
    lo = (x - hi.astype(F32)).astype(BF16)
    return hi, lo


def _split3(x):
    h1 = x.astype(BF16)
    r1 = x - h1.astype(F32)
    h2 = r1.astype(BF16)
    h3 = (r1 - h2.astype(F32)).astype(BF16)
    return h1, h2, h3


def _silu(x):
    return x * jax.nn.sigmoid(x)


def _mod_kernel(c_ref, w_ref, b_ref, o_ref):
    s = _silu(c_ref[...])
    o_ref[0] = _dot(s.astype(BF16), w_ref[0].astype(BF16)) + b_ref[0]


def _modulation(cc, w_mod, b_mod):
    L, D, D6 = w_mod.shape
    R = cc.shape[0]
    tn = 1024
    return pl.pallas_call(
        _mod_kernel,
        grid=(L, D6 // tn),
        in_specs=[pl.BlockSpec((R, D), lambda l, j: (0, 0)),
                  pl.BlockSpec((1, D, tn), lambda l, j: (l, 0, j)),
                  pl.BlockSpec((1, 1, tn), lambda l, j: (l, 0, j))],
        out_specs=pl.BlockSpec((1, R, tn), lambda l, j: (l, 0, j)),
        out_shape=jax.ShapeDtypeStruct((L, R, D6), F32),
        compiler_params=_cparams(("arbitrary", "arbitrary")),
        name="modulation",
    )(cc, w_mod, b_mod.reshape(L, 1, D6))


def _inproj_kernel(x_ref, mod_ref, g_ref, w_ref, *out_refs):
    x = x_ref[...]
    m = mod_ref[0]
    ms = jnp.mean(x * x, axis=-1, keepdims=True)
    h = x * lax.rsqrt(ms + EPS) * g_ref[...]
    h = h * (1.0 + m[1:2]) + m[0:1]
    hb = h.astype(BF16)
    for ref, (_, a, w) in zip(out_refs, IN_COLS):
        ref[...] = _dot(hb, w_ref[0, :, a:a + w]).astype(ref.dtype)


def _mod_row(i, tps, tiles_ctx):
    return (i // tps) * 2 + jnp.where((i % tps) >= tiles_ctx, 1, 0)


def _in_projection(X2, modl, g1, w_in_p, layer, tps, tiles_ctx):
    T, D = X2.shape
    dts = {"dt": F32}
    out_shape = [jax.ShapeDtypeStruct((T, w), dts.get(n, BF16)) for n, _, w in IN_COLS]
    out_specs = [pl.BlockSpec((TM, w), lambda i: (i, 0)) for _, _, w in IN_COLS]
    return pl.pallas_call(
        _inproj_kernel,
        grid=(T // TM,),
        in_specs=[pl.BlockSpec((TM, D), lambda i: (i, 0)),
                  pl.BlockSpec((1, 6, D), lambda i: (_mod_row(i, tps, tiles_ctx), 0, 0)),
                  _resident((1, D)),
                  _resident((D, D_IN_PAD), layer)],
        out_specs=out_specs,
        out_shape=out_shape,
        compiler_params=_cparams(("parallel",)),
        name="in_projection",
    )(X2, modl, g1, w_in_p)


def _norm_heads(x, g, bd):
    hi, lo = _split2(x * x)
    s = _dot(hi, bd) + _dot(lo, bd)
    return x * lax.rsqrt(s * (1.0 / HEAD_DIM) + EPS) * g


def _rope(x, cos, sin):
    lane = lax.broadcasted_iota(jnp.int32, x.shape, 1)
    sw = jnp.where((lane & 31) < 16, pltpu.roll(x, LANE - 16, 1), pltpu.roll(x, 16, 1))
    return x * cos + sw * sin


def _sink_softmax_pv(parts, sink_col):
    m = sink_col
    for s, _ in parts:
        m = jnp.maximum(m, jnp.max(s, axis=-1, keepdims=True))
    den = jnp.exp(sink_col - m)
    o = None
    for s, v in parts:
        p = jnp.exp(s - m)
        den = den + jnp.sum(p, axis=-1, keepdims=True)
        pv = _dot(p.astype(BF16), v)
        o = pv if o is None else o + pv
    return o / den


def _attn_kernel(sink_ref, q_ref, k_ref, v_ref, cos_ref, sin_ref, qg_ref, kg_ref, bd_ref, o_ref,
                 qh, kh, vh, *, LC, N):
    S = LC + N
    koff = LC + ATT_BLOCK
    bd = bd_ref[...]
    scale = HEAD_DIM ** -0.5
    zpad = jnp.zeros((ATT_BLOCK, HEAD_DIM), BF16)
    for h in range(ATT_KV_HEADS):
        for buf in (kh, vh):
            buf[h, LC:koff, :] = zpad
            buf[h, koff + N:koff + N + ATT_BLOCK, :] = zpad

    rc = TM
    for c0 in range(0, S, rc):
        lat = c0 >= LC
        dst = c0 + ATT_BLOCK if lat else c0
        if lat:
            cos = cos_ref[c0 - LC:c0 - LC + rc, :]
            sin = sin_ref[c0 - LC:c0 - LC + rc, :]
        kn = _norm_heads(k_ref[0, c0:c0 + rc, :].astype(F32), kg_ref[...], bd)
        if lat:
            kn = _rope(kn, cos, sin)
        knb = kn.astype(BF16)
        vv = v_ref[0, c0:c0 + rc, :]
        for h in range(ATT_KV_HEADS):
            kh[h, dst:dst + rc, :] = knb[:, h * HEAD_DIM:(h + 1) * HEAD_DIM]
            vh[h, dst:dst + rc, :] = vv[:, h * HEAD_DIM:(h + 1) * HEAD_DIM]
        for cb in range(ATT_HEADS // 2):
            qn = _norm_heads(q_ref[0, c0:c0 + rc, cb * LANE:(cb + 1) * LANE].astype(F32), qg_ref[...], bd)
            if lat:
                qn = _rope(qn, cos, sin)
            qnb = (qn * scale).astype(BF16)
            qh[2 * cb, c0:c0 + rc, :] = qnb[:, :HEAD_DIM]
            qh[2 * cb + 1, c0:c0 + rc, :] = qnb[:, HEAD_DIM:]

    def sink_column(h, rows):
        grp = lax.broadcasted_iota(jnp.int32, (ATT_REP * rows, 1), 0) // rows
        col = jnp.full((ATT_REP * rows, 1), sink_ref[ATT_REP * h], F32)
        for r in range(1, ATT_REP):
            col = jnp.where(grp == r, sink_ref[ATT_REP * h + r], col)
        return col

    def store_heads(o, h, row0, rows):
        for pr in range(ATT_REP // 2):
            pair = jnp.concatenate([o[(2 * pr) * rows:(2 * pr + 1) * rows],
                                    o[(2 * pr + 1) * rows:(2 * pr + 2) * rows]], axis=1)
            cb = (ATT_REP // 2) * h + pr
            o_ref[0, pl.ds(row0, rows), cb * LANE:(cb + 1) * LANE] = pair.astype(o_ref.dtype)

    for h in range(ATT_KV_HEADS):
        q4 = jnp.concatenate([qh[ATT_REP * h + r, 0:LC, :] for r in range(ATT_REP)], axis=0)
        kc = kh[h, 0:LC, :]
        vc = vh[h, 0:LC, :]
        o = _sink_softmax_pv([(_dot_nt(q4, kc), vc)], sink_column(h, LC))
        store_heads(o, h, 0, LC)

    nk = 3 * ATT_BLOCK
    qi = lax.broadcasted_iota(jnp.int32, (ATT_REP * ATT_BLOCK, nk), 0) & (ATT_BLOCK - 1)
    kj = lax.broadcasted_iota(jnp.int32, (ATT_REP * ATT_BLOCK, nk), 1)
    rel = kj - ATT_BLOCK - qi
    in_window = (rel <= ATT_WINDOW) & (rel >= -ATT_WINDOW)

    def block(j, carry):
        r0 = pl.multiple_of(j * ATT_BLOCK, ATT_BLOCK)
        kpos = kj + (j - 1) * ATT_BLOCK
        ok = in_window & (kpos >= 0) & (kpos < N)
        for h in range(ATT_KV_HEADS):
            q4 = jnp.concatenate([qh[ATT_REP * h + r, pl.ds(LC + r0, ATT_BLOCK), :] for r in range(ATT_REP)],
                                 axis=0)
            kb = kh[h, pl.ds(LC + r0, nk), :]
            vb = vh[h, pl.ds(LC + r0, nk), :]
            s_lat = jnp.where(ok, _dot_nt(q4, kb), NEG)
            s_ctx = _dot_nt(q4, kh[h, 0:LC, :])
            o = _sink_softmax_pv([(s_lat, vb), (s_ctx, vh[h, 0:LC, :])], sink_column(h, ATT_BLOCK))
            store_heads(o, h, LC + r0, ATT_BLOCK)
        return carry

    lax.fori_loop(0, N // ATT_BLOCK, block, 0)


def _attention(q, k, v, cos, sin, qg, kg, bd, sink, LC, N):
    B, S, _ = q.shape
    kern = functools.partial(_attn_kernel, LC=LC, N=N)
    grid_spec = pltpu.PrefetchScalarGridSpec(
        num_scalar_prefetch=1,
        grid=(B,),
        in_specs=[pl.BlockSpec((1, S, 512), lambda b, s: (b, 0, 0)),
                  pl.BlockSpec((1, S, 128), lambda b, s: (b, 0, 0)),
                  pl.BlockSpec((1, S, 128), lambda b, s: (b, 0, 0)),
                  pl.BlockSpec((N, LANE), lambda b, s: (0, 0)),
                  pl.BlockSpec((N, LANE), lambda b, s: (0, 0)),
                  pl.BlockSpec((1, LANE), lambda b, s: (0, 0)),
                  pl.BlockSpec((1, LANE), lambda b, s: (0, 0)),
                  pl.BlockSpec((LANE, LANE), lambda b, s: (0, 0))],
        out_specs=pl.BlockSpec((1, S, 512), lambda b, s: (b, 0, 0)),
        scratch_shapes=[pltpu.VMEM((ATT_HEADS, S, HEAD_DIM), BF16),
                        pltpu.VMEM((ATT_KV_HEADS, S + 2 * ATT_BLOCK, HEAD_DIM), BF16),
                        pltpu.VMEM((ATT_KV_HEADS, S + 2 * ATT_BLOCK, HEAD_DIM), BF16)])
    return pl.pallas_call(
        kern, grid_spec=grid_spec,
        out_shape=jax.ShapeDtypeStruct((B, S, 512), BF16),
        compiler_params=_cparams(("parallel",)),
        name="attention",
    )(sink, q, k, v, cos, sin, qg, kg, bd)


def _conv_kernel(glu_ref, dww_ref, dwb_ref, lng_ref, lnb_ref, pww_ref, pwb_ref, o_ref, hp, rot, *, LC, N):
    C = GROUP_W
    half = CONV_K // 2
    rows = CONV_ROWS

    def stream(s0, ln):
        hp[0:CONV_PAD, :] = jnp.zeros((CONV_PAD, C), F32)
        hp[CONV_PAD + ln:2 * CONV_PAD + ln, :] = jnp.zeros((CONV_PAD, C), F32)

        def fill(i, carry):
            r0 = pl.multiple_of(i * TM, TM)
            g = glu_ref[0, pl.ds(s0 + r0, TM), :]
            a = g[:, :C].astype(F32)
            gate = g[:, C:].astype(F32)
            hp[pl.ds(CONV_PAD + r0, TM), :] = a * jax.nn.sigmoid(gate)
            return carry

        lax.fori_loop(0, ln // TM, fill, 0)

        def chunk(i, carry):
            r0 = pl.multiple_of(i * rows, rows)
            parts = []
            for cb in range(C // LANE):
                win = hp[pl.ds(r0, rows + 2 * CONV_PAD), cb * LANE:(cb + 1) * LANE]
                for ph in range(8):
                    rot[cb, ph] = win[ph:ph + rot.shape[2]]
                acc = jnp.zeros((rows, LANE), F32)
                for t in range(CONV_K):
                    off = CONV_PAD - half + t
                    w = dww_ref[t:t + 1, cb * LANE:(cb + 1) * LANE]
                    acc = acc + w * rot[cb, off % 8, (off // 8) * 8:(off // 8) * 8 + rows, :]
                parts.append(acc)
            y = jnp.concatenate(parts, axis=1) + dwb_ref[...]
            mu = jnp.mean(y, axis=-1, keepdims=True)
            yc = y - mu
            var = jnp.mean(yc * yc, axis=-1, keepdims=True)
            z = yc * lax.rsqrt(var + EPS) * lng_ref[...] + lnb_ref[...]
            z = _silu(z)
            out = _dot(z.astype(BF16), pww_ref[...]) + pwb_ref[...]
            o_ref[0, pl.ds(s0 + r0, rows), :] = out.astype(o_ref.dtype)
            return carry

        lax.fori_loop(0, ln // rows, chunk, 0)

    stream(0, LC)
    stream(LC, N)


def _conformer(glu, dww, dwb, lng, lnb, pww, pwb, LC, N):
    B, S, _ = glu.shape
    C = GROUP_W
    kern = functools.partial(_conv_kernel, LC=LC, N=N)
    vec = pl.BlockSpec((1, C), lambda b: (0, 0))
    return pl.pallas_call(
        kern, grid=(B,),
        in_specs=[pl.BlockSpec((1, S, 2 * C), lambda b: (b, 0, 0)),
                  pl.BlockSpec((32, C), lambda b: (0, 0)),
                  vec, vec, vec,
                  pl.BlockSpec((C, C), lambda b: (0, 0)),
                  vec],
        out_specs=pl.BlockSpec((1, S, C), lambda b: (b, 0, 0)),
        out_shape=jax.ShapeDtypeStruct((B, S, C), BF16),
        scratch_shapes=[pltpu.VMEM((max(LC, N) + 2 * CONV_PAD, C), F32),
                        pltpu.VMEM((C // LANE, 8, CONV_ROWS + 2 * CONV_PAD - 8, LANE), F32)],
        compiler_params=_cparams(("parallel",)),
        name="conformer_conv",
    )(glu, dww, dwb, lng, lnb, pww, pwb)


def _ssd_kernel(xbc_ref, dt_ref, z_ref, cw_ref, cb_ref, arow_ref, dtb_ref, dsk_ref, ng_ref, o_ref,
                xp, xs, bm, cm, dts, yacc, state, *, LC, N):
    S = LC + N
    Q = SSD_CHUNK
    nc = S // Q
    nc_ctx = LC // Q
    ii = lax.broadcasted_iota(jnp.int32, (Q, Q), 0)
    jj = lax.broadcasted_iota(jnp.int32, (Q, Q), 1)
    lower = ii >= jj
    tri = (jnp.where(lower, 1.0, 0.0).astype(BF16), jnp.where(jj >= ii, 1.0, 0.0).astype(BF16))
    causal = (lower, jj >= ii)
    first_half = lax.broadcasted_iota(jnp.int32, (Q, LANE), 1) < SSD_STATE // 2

    def stream(s0, ln):
        xp[0:SSD_PAD, :] = jnp.zeros((SSD_PAD, SSD_XBC), F32)
        xp[SSD_PAD + ln:2 * SSD_PAD + ln, :] = jnp.zeros((SSD_PAD, SSD_XBC), F32)

        def fill(i, carry):
            r0 = pl.multiple_of(i * Q, Q)
            xp[pl.ds(SSD_PAD + r0, Q), :] = xbc_ref[0, pl.ds(s0 + r0, Q), :].astype(F32)
            return carry

        lax.fori_loop(0, ln // Q, fill, 0)

        def conv(i, carry):
            r0 = pl.multiple_of(i * Q, Q)
            for cb in range(SSD_XBC // LANE):
                cs = slice(cb * LANE, (cb + 1) * LANE)
                win = xp[pl.ds(r0, Q + 2 * SSD_PAD), cs]
                acc = jnp.zeros((Q, LANE), F32)
                for t in range(SSD_CONV):
                    acc = acc + cw_ref[t:t + 1, cs] * win[SSD_PAD - 2 + t:SSD_PAD - 2 + t + Q]
                y = _silu(acc + cb_ref[:, cs])
                if cb < SSD_INNER // LANE:
                    xs[pl.ds(s0 + r0, Q), cs] = y
                elif cb < (SSD_INNER + SSD_GROUPS * SSD_STATE) // LANE:
                    c2 = cb - SSD_INNER // LANE
                    bm[pl.ds(s0 + r0, Q), c2 * LANE:(c2 + 1) * LANE] = y
                else:
                    c2 = cb - (SSD_INNER + SSD_GROUPS * SSD_STATE) // LANE
                    cm[pl.ds(s0 + r0, Q), c2 * LANE:(c2 + 1) * LANE] = y
            return carry

        lax.fori_loop(0, ln // Q, conv, 0)

    stream(0, LC)
    stream(LC, N)

    def softplus_rows(i, carry):
        r0 = pl.multiple_of(i * Q, Q)
        v = dt_ref[0, pl.ds(r0, Q), :] + dtb_ref[...]
        dts[pl.ds(r0, Q), :] = jnp.maximum(v, 0.0) + jnp.log(1.0 + jnp.exp(-jnp.abs(v)))
        return carry

    lax.fori_loop(0, nc, softplus_rows, 0)

    def chunk_step(c, d):
        r0 = pl.multiple_of(c * Q, Q)
        dtc = dts[pl.ds(r0, Q), :]
        a1, a2, a3 = _split3(dtc * arow_ref[...])
        cum = _dot(tri[d], a1) + _dot(tri[d], a2) + _dot(tri[d], a3)
        cum_t = cum.T
        dt_t = dtc.T
        tot = cum[Q - 1:Q, :] if d == 0 else cum[0:1, :]
        e_cum = jnp.exp(cum)
        e_tot = jnp.exp(tot)
        w_state = jnp.exp(tot - cum) * dtc
        for g in range(SSD_GROUPS):
            bg = bm[pl.ds(r0, Q), g * SSD_STATE:(g + 1) * SSD_STATE]
            cg = cm[pl.ds(r0, Q), g * SSD_STATE:(g + 1) * SSD_STATE].astype(BF16)
            bg_t = bg.T.astype(BF16)
            cb_mat = _dot(cg, bg_t)
            for pr in range(SSD_HEADS // SSD_GROUPS // 2):
                pair = g * 2 + pr
                lanes = slice(pair * LANE, (pair + 1) * LANE)
                xpair = xs[pl.ds(r0, Q), lanes]
                y = None
                for sub in range(2):
                    hd = d * SSD_HEADS + pair * 2 + sub
                    seg = cum[:, hd:hd + 1] - cum_t[hd:hd + 1, :]
                    dec = jnp.where(causal[d], jnp.exp(jnp.minimum(seg, 0.0)), 0.0)
                    w = (cb_mat * dec * dt_t[hd:hd + 1, :]).astype(BF16)
                    xm = jnp.where(first_half if sub == 0 else ~first_half, xpair, 0.0).astype(BF16)
                    t = _dot(w, xm)
                    y = t if y is None else y + t
                h0 = d * SSD_HEADS + pair * 2
                sel = lambda m: jnp.where(first_half, m[:, h0:h0 + 1], m[:, h0 + 1:h0 + 2])
                st = state[d * (SSD_HEADS // 2) + pair]
                y = y + _dot(cg, st.astype(BF16)) * sel(e_cum)
                xw = (xpair * sel(w_state)).astype(BF16)
                state[d * (SSD_HEADS // 2) + pair] = st * sel(e_tot) + _dot(bg_t, xw)
                if d == 0:
                    yacc[pl.ds(r0, Q), lanes] = y
                else:
                    yacc[pl.ds(r0, Q), lanes] = yacc[pl.ds(r0, Q), lanes] + y

    state[...] = jnp.zeros(state.shape, F32)

    def fwd(c, carry):
        chunk_step(c, 0)
        return carry

    lax.fori_loop(0, nc, fwd, 0)

    def bwd_ctx(i, carry):
        chunk_step(nc_ctx - 1 - i, 1)
        return carry

    lax.fori_loop(0, nc_ctx, bwd_ctx, 0)

    def bwd_lat(i, carry):
        chunk_step(nc - 1 - i, 1)
        return carry

    lax.fori_loop(0, nc - nc_ctx, bwd_lat, 0)

    def gate_out(i, carry):
        r0 = pl.multiple_of(i * Q, Q)
        y = yacc[pl.ds(r0, Q), :] + xs[pl.ds(r0, Q), :] * dsk_ref[...]
        y = y * _silu(z_ref[0, pl.ds(r0, Q), :].astype(F32))
        ms = jnp.mean(y * y, axis=-1, keepdims=True)
        o_ref[0, pl.ds(r0, Q), :] = (y * lax.rsqrt(ms + EPS) * ng_ref[...]).astype(o_ref.dtype)
        return carry

    lax.fori_loop(0, nc, gate_out, 0)


def _ssd(xbc, dt, z, cw, cb, arow, dtb, dsk, ng, LC, N):
    B, S, _ = xbc.shape
    kern = functools.partial(_ssd_kernel, LC=LC, N=N)
    row = lambda w: pl.BlockSpec((1, w), lambda b: (0, 0))
    return pl.pallas_call(
        kern, grid=(B,),
        in_specs=[pl.BlockSpec((1, S, SSD_XBC), lambda b: (b, 0, 0)),
                  pl.BlockSpec((1, S, LANE), lambda b: (b, 0, 0)),
                  pl.BlockSpec((1, S, SSD_INNER), lambda b: (b, 0, 0)),
                  pl.BlockSpec((SSD_CONV, SSD_XBC), lambda b: (0, 0)),
                  row(SSD_XBC), row(LANE), row(LANE), row(SSD_INNER), row(SSD_INNER)],
        out_specs=pl.BlockSpec((1, S, SSD_INNER), lambda b: (b, 0, 0)),
        out_shape=jax.ShapeDtypeStruct((B, S, SSD_INNER), BF16),
        scratch_shapes=[pltpu.VMEM((max(LC, N) + 2 * SSD_PAD, SSD_XBC), F32),
                        pltpu.VMEM((S, SSD_INNER), F32),
                        pltpu.VMEM((S, SSD_GROUPS * SSD_STATE), F32),
                        pltpu.VMEM((S, SSD_GROUPS * SSD_STATE), F32),
                        pltpu.VMEM((S, LANE), F32),
                        pltpu.VMEM((S, SSD_INNER), F32),
                        pltpu.VMEM((SSD_HEADS, SSD_STATE, LANE), F32)],
        compiler_params=_cparams(("parallel",)),
        name="ssd",
    )(xbc, dt, z, cw, cb, arow, dtb, dsk, ng)


def _pool_kernel(pin_ref, pw_ref, ps_ref, o_ref, up, *, LC, N):
    C = GROUP_W
    rows = TM

    def stream(s0, ln):
        up[0:POOL_PAD, :] = jnp.zeros((POOL_PAD, C), F32)
        up[POOL_PAD + ln:2 * POOL_PAD + ln, :] = jnp.zeros((POOL_PAD, C), F32)

        def fill(i, carry):
            r0 = pl.multiple_of(i * rows, rows)
            up[pl.ds(POOL_PAD + r0, rows), :] = pin_ref[0, pl.ds(s0 + r0, rows), :].astype(F32)
            return carry

        lax.fori_loop(0, ln // rows, fill, 0)

        def chunk(i, carry):
            r0 = pl.multiple_of(i * rows, rows)
            t = r0 + lax.broadcasted_iota(jnp.int32, (rows, 1), 0)
            outs = []
            for gi, w in enumerate(POOL_SIZES):
                cs = slice(gi * POOL_CH, (gi + 1) * POOL_CH)
                win = up[pl.ds(r0, rows + 2 * POOL_PAD), cs]
                acc = jnp.zeros((rows, POOL_CH), F32)
                for d in range(-(w // 2), w - w // 2):
                    acc = acc + win[POOL_PAD + d:POOL_PAD + d + rows]
                cnt = jnp.minimum(t + (w - w // 2), ln) - jnp.maximum(t - w // 2, 0)
                p = acc / cnt.astype(F32) - win[POOL_PAD:POOL_PAD + rows]
                outs.append(_dot(p.astype(BF16), pw_ref[gi]))
            y = jnp.concatenate(outs, axis=1) * ps_ref[...]
            o_ref[0, pl.ds(s0 + r0, rows), :] = y.astype(o_ref.dtype)
            return carry

        lax.fori_loop(0, ln // rows, chunk, 0)

    stream(0, LC)
    stream(LC, N)


def _pool(pin, pw, ps, LC, N):
    B, S, C = pin.shape
    kern = functools.partial(_pool_kernel, LC=LC, N=N)
    return pl.pallas_call(
        kern, grid=(B,),
        in_specs=[pl.BlockSpec((1, S, C), lambda b: (b, 0, 0)),
                  pl.BlockSpec((len(POOL_SIZES), POOL_CH, POOL_CH), lambda b: (0, 0, 0)),
                  pl.BlockSpec((1, C), lambda b: (0, 0))],
        out_specs=pl.BlockSpec((1, S, C), lambda b: (b, 0, 0)),
        out_shape=jax.ShapeDtypeStruct((B, S, C), BF16),
        scratch_shapes=[pltpu.VMEM((max(LC, N) + 2 * POOL_PAD, C), F32)],
        compiler_params=_cparams(("parallel",)),
        name="pool_mixer",
    )(pin, pw, ps)


def _pack_halves(x):
    w = x.shape[1] // 2
    u = lax.bitcast_convert_type(x.astype(BF16).astype(F32), jnp.uint32)
    return (u[:, :w] >> 16) | (u[:, w:] & jnp.uint32(0xFFFF0000))


def _unpack_halves(p):
    lo = lax.bitcast_convert_type(p << 16, F32)
    hi = lax.bitcast_convert_type(p & jnp.uint32(0xFFFF0000), F32)
    return lo, hi


SUB = 8
ROW_WORDS = SUB * LANE


def _store_token_tiles(ref, packed):
    rows = packed.shape[0]
    for s in range(SUB):
        ref[pl.ds(s, rows, stride=SUB), :] = packed[:, s * LANE:(s + 1) * LANE]


def _load_token_tiles(ref, rows, s):
    return ref[pl.ds(s, rows, stride=SUB), :]


def _outproj_kernel(ya_ref, yb_ref, yc_ref, yd_ref, x_ref, mod_ref, g_ref, w_ref, wr_ref,
                    xo_ref, h2_ref, lg_ref):
    acc = None
    for i, ref in enumerate((ya_ref, yb_ref, yc_ref, yd_ref)):
        t = _dot(ref[...], w_ref[0, i * GROUP_W:(i + 1) * GROUP_W, :])
        acc = t if acc is None else acc + t
    m = mod_ref[0]
    x = x_ref[...] + m[2:3] * acc
    xo_ref[...] = x
    ms = jnp.mean(x * x, axis=-1, keepdims=True)
    h2 = x * lax.rsqrt(ms + EPS) * g_ref[...]
    h2 = h2 * (1.0 + m[4:5]) + m[3:4]
    _store_token_tiles(h2_ref, _pack_halves(h2))
    h_hi, h_lo = _split2(h2)
    both = _dot(h_hi, wr_ref[0])
    lg_ref[...] = both[:, :LANE] + (both[:, LANE:] + _dot(h_lo, wr_ref[0, :, :LANE]))


def _out_projection(ys, X2, modl, g2, w_out, wr, layer, tps, tiles_ctx, latent_only=False):
    T, D = X2.shape
    if latent_only:
        tl = tps - tiles_ctx
        steps = (T // TM) // tps * tl
        src = lambda i: ((i // tl) * tps + tiles_ctx + i % tl, 0)
        mrow = lambda i: ((i // tl) * 2 + 1, 0, 0)
        aliases = {}
    else:
        steps = T // TM
        src = lambda i: (i, 0)
        mrow = lambda i: (_mod_row(i, tps, tiles_ctx), 0, 0)
        aliases = {4: 0}
    rows = steps * TM
    ytile = pl.BlockSpec((TM, GROUP_W), src)
    return pl.pallas_call(
        _outproj_kernel,
        grid=(steps,),
        in_specs=[ytile, ytile, ytile, ytile,
                  pl.BlockSpec((TM, D), src),
                  pl.BlockSpec((1, 6, D), mrow),
                  _resident((1, D)),
                  _resident((4 * GROUP_W, D), layer),
                  _resident((D, 2 * LANE), layer)],
        out_specs=[pl.BlockSpec((TM, D), lambda i: (i, 0)),
                   pl.BlockSpec((TM * SUB, LANE), lambda i: (i, 0)),
                   pl.BlockSpec((TM, LANE), lambda i: (i, 0))],
        out_shape=[jax.ShapeDtypeStruct((rows, D), F32),
                   jax.ShapeDtypeStruct((rows * SUB, LANE), jnp.uint32),
                   jax.ShapeDtypeStruct((rows, LANE), F32)],
        input_output_aliases=aliases,
        compiler_params=_cparams(("parallel",)),
        name="out_projection",
    )(*ys, X2, modl, g2, w_out, wr)


META_E = 0
META_R = 2
META_W = 4


def _route_kernel(lg_ref, meta_ref, cnt_ref, carry):
    @pl.when(pl.program_id(0) == 0)
    def _():
        carry[...] = jnp.zeros(carry.shape, F32)

    lg = lg_ref[...]
    lane = lax.broadcasted_iota(jnp.int32, lg.shape, 1).astype(F32)
    big = 1e9
    rmax = lambda m: jnp.max(jnp.where(m, lg, NEG), axis=-1, keepdims=True)
    first = lambda m: jnp.min(jnp.where(m, lane, big), axis=-1, keepdims=True)

    gm = lane < MOE_GROUPS
    gmax = rmax(gm)
    gidx = first(gm & (lg == gmax))
    g_w = 1.0 / jnp.sum(jnp.where(gm, jnp.exp(lg - gmax), 0.0), axis=-1, keepdims=True)
    lo = MOE_GROUPS + MOE_PER_GROUP * gidx
    em = (lane >= lo) & (lane < lo + MOE_PER_GROUP)
    v1 = rmax(em)
    i1 = first(em & (lg == v1))
    em2 = em & (lane != i1)
    v2 = rmax(em2)
    i2 = first(em2 & (lg == v2))
    t = jnp.exp(v2 - v1)
    w1 = g_w / (1.0 + t)
    w2 = g_w * t / (1.0 + t)

    oh1 = jnp.where(lane == i1, 1.0, 0.0)
    oh2 = jnp.where(lane == i2, 1.0, 0.0)
    oh = oh1 + oh2
    rows = lg.shape[0]
    ri = lax.broadcasted_iota(jnp.int32, (rows, rows), 0)
    ci = lax.broadcasted_iota(jnp.int32, (rows, rows), 1)
    before = jnp.where(ci < ri, 1.0, 0.0).astype(BF16)
    base = _dot(before, oh.astype(BF16)) + carry[0:1, :]
    r1 = jnp.sum(oh1 * base, axis=-1, keepdims=True)
    r2 = jnp.sum(oh2 * base, axis=-1, keepdims=True)
    carry[0:1, :] = carry[0:1, :] + jnp.sum(oh, axis=0, keepdims=True)

    meta = jnp.zeros(lg.shape, F32)
    for k, val in enumerate((i1 - MOE_GROUPS, i2 - MOE_GROUPS, r1, r2, w1, w2)):
        meta = jnp.where(lane == k, val, meta)
    meta_ref[...] = meta
    cnt_ref[...] = jnp.broadcast_to(carry[0:1, :], cnt_ref.shape)


def _route(logits, n_tiles):
    T = logits.shape[0]
    rt = max(r for r in (4 * TM, 2 * TM, TM) if T % r == 0)
    meta, cnt = pl.pallas_call(
        _route_kernel,
        grid=(T // rt,),
        in_specs=[pl.BlockSpec((rt, LANE), lambda i: (i, 0))],
        out_specs=[pl.BlockSpec((rt, LANE), lambda i: (i, 0)),
                   pl.BlockSpec((8, LANE), lambda i: (0, 0))],
        out_shape=[jax.ShapeDtypeStruct((T, LANE), F32), jax.ShapeDtypeStruct((8, LANE), F32)],
        scratch_shapes=[pltpu.VMEM((8, LANE), F32)],
        compiler_params=_cparams(("arbitrary",)),
        name="moe_route",
    )(logits)
    routed = meta[:, :4].astype(jnp.int32)
    counts = cnt[0, MOE_GROUPS:MOE_GROUPS + MOE_EXPERTS].astype(jnp.int32)
    ntile = (counts + TE - 1) // TE
    tile_end = jnp.cumsum(ntile)
    tile_start = (tile_end - ntile).astype(jnp.int32)
    total = tile_end[-1]
    tiles = jnp.arange(n_tiles, dtype=jnp.int32)
    active = tiles < total
    last = jnp.minimum(tiles, total - 1)
    t_exp = jnp.sum((last[:, None] >= tile_end[None, :]).astype(jnp.int32), axis=1)
    t_first = (active & (tiles == jnp.sum(jnp.where(t_exp[:, None] == jnp.arange(MOE_EXPERTS)[None, :],
                                                     tile_start[None, :], 0), axis=1))).astype(jnp.int32)
    experts = jnp.arange(MOE_EXPERTS, dtype=jnp.int32)
    used = ntile > 0
    later_used = used[None, :] & (experts[None, :] > experts[:, None])
    next_used = jnp.min(jnp.where(later_used, experts[None, :], MOE_EXPERTS), axis=1)
    slot_of = (jnp.cumsum(used.astype(jnp.int32)) - 1) % 2
    per_tile = lambda table: jnp.sum(jnp.where(t_exp[:, None] == experts[None, :], table[None, :], 0),
                                     axis=1).astype(jnp.int32)
    tables = (t_exp.astype(jnp.int32), t_first, active.astype(jnp.int32), per_tile(next_used), per_tile(slot_of))
    first_tile = jnp.sum(jnp.where(routed[:, META_E:META_E + 2, None] == experts, tile_start, 0), axis=-1)
    dest = (first_tile * TE + routed[:, META_R:META_R + 2]) * SUB
    dest = dest.astype(jnp.int32).reshape(T // TM, TM, 2).transpose(0, 2, 1)
    return meta, dest, tables


def _dispatch_kernel(dest_ref, h_ref, xs_in_ref, xs_ref, sem):
    del xs_in_ref

    def row_copy(r, dst):
        return pltpu.make_async_copy(h_ref.at[pl.ds(pl.multiple_of(r * SUB, SUB), SUB)],
                                     xs_ref.at[pl.ds(pl.multiple_of(dst, SUB), SUB)], sem)

    def start(r, carry):
        for slot in range(2):
            row_copy(r, dest_ref[0, slot, r]).start(priority=slot)
        return carry

    lax.fori_loop(0, TM, start, 0, unroll=8)
    for _ in range(2 * TM):
        row_copy(0, 0).wait()


def _moe_dispatch(h2p, dest, xs_init):
    T = h2p.shape[0] // SUB
    return pl.pallas_call(
        _dispatch_kernel,
        grid=(T // TM,),
        in_specs=[pl.BlockSpec((1, 2, TM), lambda i: (i, 0, 0), memory_space=pltpu.SMEM),
                  pl.BlockSpec((TM * SUB, LANE), lambda i: (i, 0)),
                  pl.BlockSpec(memory_space=pl.ANY)],
        out_specs=pl.BlockSpec(memory_space=pl.ANY),
        out_shape=jax.ShapeDtypeStruct(xs_init.shape, jnp.uint32),
        scratch_shapes=[pltpu.SemaphoreType.DMA(())],
        input_output_aliases={2: 0},
        compiler_params=_cparams(("arbitrary",)),
        name="moe_dispatch",
    )(dest, h2p, xs_init)


def _moe_kernel(te_ref, tf_ref, tv_ref, tn_ref, ts_ref, x_ref, wg_hbm, wu_hbm, wd_hbm, o_ref,
                wg_f, wu_f, wd_f, wg_b, wu_b, wd_b, sems, *, layer):
    i = pl.program_id(0)

    def weight_copies(e, slot):
        return [pltpu.make_async_copy(src.at[layer, e], dst.at[slot], sems.at[slot])
                for src, dst in ((wg_hbm, wg_f), (wu_hbm, wu_f), (wd_hbm, wd_f))]

    @pl.when(tf_ref[i] == 1)
    def _():
        slot = ts_ref[i]

        @pl.when(i == 0)
        def _():
            for c in weight_copies(te_ref[0], 0):
                c.start()

        for c in weight_copies(te_ref[i], slot):
            c.wait()

        @pl.when(tn_ref[i] < MOE_EXPERTS)
        def _():
            for c in weight_copies(tn_ref[i], 1 - slot):
                c.start()

        wg_b[...] = wg_f[slot].astype(BF16)
        wu_b[...] = wu_f[slot].astype(BF16)
        wd_b[...] = wd_f[slot].astype(BF16)

    @pl.when(tv_ref[i] == 1)
    def _():
        words = jnp.concatenate([_load_token_tiles(x_ref, TE, s) for s in range(SUB)], axis=1)
        lo, hi = _unpack_halves(words)
        lo = lo.astype(BF16)
        hi = hi.astype(BF16)
        half = lo.shape[1]
        g = _dot(lo, wg_b[:half, :]) + _dot(hi, wg_b[half:, :])
        u = _dot(lo, wu_b[:half, :]) + _dot(hi, wu_b[half:, :])
        hid = (_silu(g) * u).astype(BF16)
        _store_token_tiles(o_ref, _pack_halves(_dot(hid, wd_b[...])))

    @pl.when(tv_ref[i] == 0)
    def _():
        o_ref[...] = jnp.zeros(o_ref.shape, o_ref.dtype)


def _moe_experts(xs, tables, w_gate, w_up, w_down, layer):
    R = xs.shape[0] // SUB
    D = 2 * ROW_WORDS
    n_tiles = R // TE
    tile = pl.BlockSpec((TE * SUB, LANE), lambda i, *_: (i, 0))
    hbm = pl.BlockSpec(memory_space=pl.ANY)
    grid_spec = pltpu.PrefetchScalarGridSpec(
        num_scalar_prefetch=len(tables),
        grid=(n_tiles,),
        in_specs=[tile, hbm, hbm, hbm],
        out_specs=tile,
        scratch_shapes=[pltpu.VMEM((2, D, D_EXPERT), F32),
                        pltpu.VMEM((2, D, D_EXPERT), F32),
                        pltpu.VMEM((2, D_EXPERT, D), F32),
                        pltpu.VMEM((D, D_EXPERT), BF16),
                        pltpu.VMEM((D, D_EXPERT), BF16),
                        pltpu.VMEM((D_EXPERT, D), BF16),
                        pltpu.SemaphoreType.DMA((2,))])
    return pl.pallas_call(
        functools.partial(_moe_kernel, layer=layer), grid_spec=grid_spec,
        out_shape=jax.ShapeDtypeStruct((R * SUB, LANE), jnp.uint32),
        compiler_params=_cparams(("arbitrary",)),
        name="moe_experts",
    )(*tables, xs, w_gate, w_up, w_down)


def _combine_kernel(dest_ref, dest_next_ref, x_ref, meta_ref, mod_ref, ye_ref, o_ref, ybuf, sems):
    i = pl.program_id(0)
    n = pl.num_programs(0)
    cur = i % 2

    def row_copy(src, buf, slot, r):
        return pltpu.make_async_copy(ye_ref.at[pl.ds(pl.multiple_of(src, SUB), SUB)],
                                     ybuf.at[buf, slot, pl.ds(pl.multiple_of(r * SUB, SUB), SUB)], sems.at[buf])

    def gather(iref, buf):
        def start(r, carry):
            for slot in range(2):
                row_copy(iref[0, slot, r], buf, slot, r).start(priority=slot)
            return carry

        lax.fori_loop(0, TM, start, 0, unroll=8)

    @pl.when(i == 0)
    def _():
        gather(dest_ref, 0)

    @pl.when(i + 1 < n)
    def _():
        gather(dest_next_ref, 1 - cur)

    for _ in range(2 * TM):
        row_copy(0, cur, 0, 0).wait()

    half = x_ref.shape[1] // 2
    meta = meta_ref[...]
    w1 = meta[:, META_W:META_W + 1]
    w2 = meta[:, META_W + 1:META_W + 2]
    g2 = mod_ref[0][5:6]
    for s in range(SUB):
        lo1, hi1 = _unpack_halves(_load_token_tiles(ybuf.at[cur, 0], TM, s))
        lo2, hi2 = _unpack_halves(_load_token_tiles(ybuf.at[cur, 1], TM, s))
        for base, a, b in ((s * LANE, lo1, lo2), (half + s * LANE, hi1, hi2)):
            cols = slice(base, base + LANE)
            o_ref[:, cols] = x_ref[:, cols] + g2[:, cols] * (w1 * a + w2 * b)


def _moe_combine(X2, ye, meta, dest, modl, tps, tiles_ctx):
    T, D = X2.shape
    steps = T // TM
    tile = pl.BlockSpec((TM, D), lambda i: (i, 0))
    return pl.pallas_call(
        _combine_kernel,
        grid=(steps,),
        in_specs=[pl.BlockSpec((1, 2, TM), lambda i: (i, 0, 0), memory_space=pltpu.SMEM),
                  pl.BlockSpec((1, 2, TM), lambda i: (jnp.minimum(i + 1, steps - 1), 0, 0),
                               memory_space=pltpu.SMEM),
                  tile,
                  pl.BlockSpec((TM, LANE), lambda i: (i, 0)),
                  pl.BlockSpec((1, 6, D), lambda i: (_mod_row(i, tps, tiles_ctx), 0, 0)),
                  pl.BlockSpec(memory_space=pl.ANY)],
        out_specs=tile,
        out_shape=jax.ShapeDtypeStruct((T, D), F32),
        scratch_shapes=[pltpu.VMEM((2, 2, TM * SUB, LANE), jnp.uint32),
                        pltpu.SemaphoreType.DMA((2,))],
        input_output_aliases={2: 0},
        compiler_params=_cparams(("arbitrary",)),
        name="moe_combine",
    )(dest, dest, X2, meta, modl, ye)


def _rope_tables(n):
    rows = n // GRID_W
    row = jnp.repeat(jnp.arange(rows, dtype=F32), GRID_W)
    col = jnp.tile(jnp.arange(GRID_W, dtype=F32), rows)
    half = HEAD_DIM // 2
    inv_freq = 1.0 / (ROPE_THETA ** (jnp.arange(0, half, 2, dtype=F32) / half))
    ar = row[:, None] * inv_freq
    ac = col[:, None] * inv_freq
    cos = jnp.concatenate([jnp.cos(ar), jnp.cos(ar), jnp.cos(ac), jnp.cos(ac)], axis=-1)
    sin = jnp.concatenate([-jnp.sin(ar), jnp.sin(ar), -jnp.sin(ac), jnp.sin(ac)], axis=-1)
    return jnp.tile(cos, (1, 2)), jnp.tile(sin, (1, 2))


def kernel(x, c, ctx, c_ctx, norm1_g, norm2_g, w_mod, b_mod, w_in, w_out, q_norm_g, k_norm_g, attn_sink,
           conv_dw_w, conv_dw_b, conv_ln_g, conv_ln_b, conv_pw_w, conv_pw_b, ssd_conv_w, ssd_conv_b, ssd_a_log,
           ssd_dt_bias, ssd_d, ssd_norm_g, pool_w, pool_scale, moe_group_router, moe_expert_router, moe_w_gate,
           moe_w_up, moe_w_down):
    B, N, D = x.shape
    LC = ctx.shape[1]
    S = LC + N
    L = w_mod.shape[0]
    assert LC % TM == 0 and N % TM == 0 and N % GRID_W == 0 and D == 2 * ROW_WORDS
    tps = S // TM
    tiles_ctx = LC // TM
    T = B * S

    rows = -(-(B + 1) // 8) * 8
    cc = jnp.concatenate([c, c_ctx[None, :], jnp.zeros((rows - B - 1, D), F32)], axis=0)
    mod = _modulation(cc, w_mod, b_mod).reshape(L, rows, 6, D)

    w_in_p = jnp.concatenate([w_in[:, :, :DT_SRC + 16], jnp.zeros((L, D, LANE - 16), F32), w_in[:, :, DT_SRC + 16:]],
                             axis=-1).astype(BF16)
    w_out_b = w_out.astype(BF16)
    w_router = jnp.concatenate([moe_group_router, moe_expert_router,
                                jnp.zeros((L, D, LANE - MOE_GROUPS - MOE_EXPERTS), F32)], axis=-1)
    wr_hi = w_router.astype(BF16)
    wr = jnp.concatenate([wr_hi, (w_router - wr_hi.astype(F32)).astype(BF16)], axis=-1)

    cos, sin = _rope_tables(N)
    bd = (jnp.arange(LANE)[:, None] // HEAD_DIM == jnp.arange(LANE)[None, :] // HEAD_DIM).astype(BF16)
    pad_lane = lambda v: jnp.concatenate([v.reshape(-1), jnp.zeros((LANE - v.size,), F32)]).reshape(1, LANE)

    X = jnp.concatenate([ctx, x], axis=1).reshape(T, D)
    xs = None
    for l in range(L):
        modl = jnp.stack([jnp.broadcast_to(mod[l, B], (B, 6, D)), mod[l, :B]], axis=1).reshape(2 * B, 6, D)
        q, k, v, glu, z, xbc, dt, pin = _in_projection(X, modl, norm1_g[l][None], w_in_p, l, tps, tiles_ctx)
        r3 = lambda a: a.reshape(B, S, a.shape[-1])
        y_att = _attention(r3(q), r3(k), r3(v), cos, sin, jnp.tile(q_norm_g[l], 2)[None], jnp.tile(k_norm_g[l], 2)[None],
                           bd, attn_sink[l], LC, N)
        dww = jnp.concatenate([conv_dw_w[l], jnp.zeros((32 - CONV_K, GROUP_W), F32)], axis=0)
        y_conv = _conformer(r3(glu), dww, conv_dw_b[l][None], conv_ln_g[l][None], conv_ln_b[l][None],
                            conv_pw_w[l].astype(BF16), conv_pw_b[l][None], LC, N)
        y_ssd = _ssd(r3(xbc), r3(dt), r3(z), ssd_conv_w[l], ssd_conv_b[l][None], pad_lane(-jnp.exp(ssd_a_log[l])),
                     pad_lane(ssd_dt_bias[l]), jnp.repeat(ssd_d[l], HEAD_DIM)[None], ssd_norm_g[l][None], LC, N)
        y_pool = _pool(r3(pin), pool_w[l].astype(BF16), pool_scale[l][None], LC, N)
        ys = [a.reshape(T, GROUP_W) for a in (y_att, y_conv, y_ssd, y_pool)]
        last = l == L - 1
        X, h2p, logits = _out_projection(ys, X, modl, norm2_g[l][None], w_out_b, wr, l, tps, tiles_ctx,
                                         latent_only=last)
        n_tiles = (2 * X.shape[0]) // TE + MOE_EXPERTS
        meta, dest, tables = _route(logits, n_tiles)
        if xs is None or xs.shape[0] != n_tiles * TE * SUB:
            xs = jnp.zeros((n_tiles * TE * SUB, LANE), jnp.uint32)
        xs = _moe_dispatch(h2p, dest, xs)
        ye = _moe_experts(xs, tables, moe_w_gate, moe_w_up, moe_w_down, l)
        if last:
            X = _moe_combine(X, ye, meta, dest, modl, tps - tiles_ctx, 0)
        else:
            X = _moe_combine(X, ye, meta, dest, modl, tps, tiles_ctx)
    return X.reshape(B, N, D)
```

```python
import functools
import math

import jax
import jax.numpy as jnp
from jax import lax
from jax.experimental import pallas as pl
from jax.experimental.pallas import tpu as pltpu

F32 = jnp.float32
BF16 = jnp.bfloat16

EPS = 1e-6
GRID_W = 64
HEAD_DIM = 64
ATT_HEADS = 8
ATT_KV_HEADS = 2
ATT_REP = ATT_HEADS // ATT_KV_HEADS
ATT_BLOCK = 128
ATT_WINDOW = 128
ROPE_THETA = 10000.0
GROUP_W = 512
CONV_K = 31
CONV_PAD = 16
CONV_ROWS = 128
SSD_HEADS = 8
SSD_GROUPS = 2
SSD_STATE = 128
SSD_CHUNK = 128
SSD_CONV = 4
SSD_PAD = 8
SSD_INNER = 512
SSD_XBC = SSD_INNER + 2 * SSD_GROUPS * SSD_STATE
POOL_SIZES = (2, 4, 8, 16)
POOL_CH = 128
POOL_PAD = 8
MOE_GROUPS = 4
MOE_PER_GROUP = 8
MOE_EXPERTS = 32
D_EXPERT = 512

TM = 256
TE = 512
LANE = 128
NEG = -1e30
VMEM_LIMIT = 56 * 1024 * 1024

IN_COLS = (("q", 0, 512), ("k", 512, 128), ("v", 640, 128), ("glu", 768, 1024), ("z", 1792, 512),
           ("xbc", 2304, 1024), ("dt", 3328, 128), ("pin", 3456, 512))
D_IN_PAD = 3968
DT_SRC = 3328


def _cparams(sem):
    return pltpu.CompilerParams(dimension_semantics=sem, vmem_limit_bytes=VMEM_LIMIT)


def _resident(shape, layer=None):
    nd = len(shape)
    if layer is None:
        return pl.BlockSpec(shape, lambda *_: (0,) * nd, pipeline_mode=pl.Buffered(1))
    return pl.BlockSpec((1,) + tuple(shape), lambda *_: (layer,) + (0,) * nd, pipeline_mode=pl.Buffered(1))


def _dot(a, b):
    return jnp.dot(a, b, preferred_element_type=F32)


def _dot_nt(a, b):
    return lax.dot_general(a, b, (((1,), (1,)), ((), ())), preferred_element_type=F32)


def _split2(x):
    hi = x.astype(BF16)
    lo = (x - hi.astype(F32)).astype(BF16)
    return hi, lo


def _split3(x):
    h1 = x.astype(BF16)
    r1 = x - h1.astype(F32)
    h2 = r1.astype(BF16)
    h3 = (r1 - h2.astype(F32)).astype(BF16)
    return h1, h2, h3


def _silu(x):
    return x * jax.nn.sigmoid(x)


def _mod_kernel(c_ref, w_ref, b_ref, o_ref):
    s = _silu(c_ref[...])
    o_ref[0] = _dot(s.astype(BF16), w_ref[0].astype(BF16)) + b_ref[0]


def _modulation(cc, w_mod, b_mod):
    L, D, D6 = w_mod.shape
    R = cc.shape[0]
    tn = 1024
    return pl.pallas_call(
        _mod_kernel,
        grid=(L, D6 // tn),
        in_specs=[pl.BlockSpec((R, D), lambda l, j: (0, 0)),
                  pl.BlockSpec((1, D, tn), lambda l, j: (l, 0, j)),
                  pl.BlockSpec((1, 1, tn), lambda l, j: (l, 0, j))],
        out_specs=pl.BlockSpec((1, R, tn), lambda l, j: (l, 0, j)),
        out_shape=jax.ShapeDtypeStruct((L, R, D6), F32),
        compiler_params=_cparams(("arbitrary", "arbitrary")),
        name="modulation",
    )(cc, w_mod, b_mod.reshape(L, 1, D6))


def _inproj_kernel(x_ref, mod_ref, g_ref, w_ref, *out_refs):
    x = x_ref[...]
    m = mod_ref[0]
    ms = jnp.mean(x * x, axis=-1, keepdims=True)
    h = x * lax.rsqrt(ms + EPS) * g_ref[...]
    h = h * (1.0 + m[1:2]) + m[0:1]
    hb = h.astype(BF16)
    for ref, (_, a, w) in zip(out_refs, IN_COLS):
        ref[...] = _dot(hb, w_ref[0, :, a:a + w]).astype(ref.dtype)


def _mod_row(i, tps, tiles_ctx):
    return (i // tps) * 2 + jnp.where((i % tps) >= tiles_ctx, 1, 0)


def _in_projection(X2, modl, g1, w_in_p, layer, tps, tiles_ctx):
    T, D = X2.shape
    dts = {"dt": F32}
    out_shape = [jax.ShapeDtypeStruct((T, w), dts.get(n, BF16)) for n, _, w in IN_COLS]
    out_specs = [pl.BlockSpec((TM, w), lambda i: (i, 0)) for _, _, w in IN_COLS]
    return pl.pallas_call(
        _inproj_kernel,
        grid=(T // TM,),
        in_specs=[pl.BlockSpec((TM, D), lambda i: (i, 0)),
                  pl.BlockSpec((1, 6, D), lambda i: (_mod_row(i, tps, tiles_ctx), 0, 0)),
                  _resident((1, D)),
                  _resident((D, D_IN_PAD), layer)],
        out_specs=out_specs,
        out_shape=out_shape,
        compiler_params=_cparams(("parallel",)),
        name="in_projection",
    )(X2, modl, g1, w_in_p)


def _norm_heads(x, g, bd):
    hi, lo = _split2(x * x)
    s = _dot(hi, bd) + _dot(lo, bd)
    return x * lax.rsqrt(s * (1.0 / HEAD_DIM) + EPS) * g


def _rope(x, cos, sin):
    lane = lax.broadcasted_iota(jnp.int32, x.shape, 1)
    sw = jnp.where((lane & 31) < 16, pltpu.roll(x, LANE - 16, 1), pltpu.roll(x, 16, 1))
    return x * cos + sw * sin


def _sink_softmax_pv(parts, sink_col):
    m = sink_col
    for s, _ in parts:
        m = jnp.maximum(m, jnp.max(s, axis=-1, keepdims=True))
    den = jnp.exp(sink_col - m)
    o = None
    for s, v in parts:
        p = jnp.exp(s - m)
        den = den + jnp.sum(p, axis=-1, keepdims=True)
        pv = _dot(p.astype(BF16), v)
        o = pv if o is None else o + pv
    return o / den


def _attn_kernel(sink_ref, q_ref, k_ref, v_ref, cos_ref, sin_ref, qg_ref, kg_ref, bd_ref, o_ref,
                 qh, kh, vh, *, LC, N):
    S = LC + N
    koff = LC + ATT_BLOCK
    bd = bd_ref[...]
    scale = HEAD_DIM ** -0.5
    zpad = jnp.zeros((ATT_BLOCK, HEAD_DIM), BF16)
    for h in range(ATT_KV_HEADS):
        for buf in (kh, vh):
            buf[h, LC:koff, :] = zpad
            buf[h, koff + N:koff + N + ATT_BLOCK, :] = zpad

    rc = TM
    for c0 in range(0, S, rc):
        lat = c0 >= LC
        dst = c0 + ATT_BLOCK if lat else c0
        if lat:
            cos = cos_ref[c0 - LC:c0 - LC + rc, :]
            sin = sin_ref[c0 - LC:c0 - LC + rc, :]
        kn = _norm_heads(k_ref[0, c0:c0 + rc, :].astype(F32), kg_ref[...], bd)
        if lat:
            kn = _rope(kn, cos, sin)
        knb = kn.astype(BF16)
        vv = v_ref[0, c0:c0 + rc, :]
        for h in range(ATT_KV_HEADS):
            kh[h, dst:dst + rc, :] = knb[:, h * HEAD_DIM:(h + 1) * HEAD_DIM]
            vh[h, dst:dst + rc, :] = vv[:, h * HEAD_DIM:(h + 1) * HEAD_DIM]
        for cb in range(ATT_HEADS // 2):
            qn = _norm_heads(q_ref[0, c0:c0 + rc, cb * LANE:(cb + 1) * LANE].astype(F32), qg_ref[...], bd)
            if lat:
                qn = _rope(qn, cos, sin)
            qnb = (qn * scale).astype(BF16)
            qh[2 * cb, c0:c0 + rc, :] = qnb[:, :HEAD_DIM]
            qh[2 * cb + 1, c0:c0 + rc, :] = qnb[:, HEAD_DIM:]

    def sink_column(h, rows):
        grp = lax.broadcasted_iota(jnp.int32, (ATT_REP * rows, 1), 0) // rows
        col = jnp.full((ATT_REP * rows, 1), sink_ref[ATT_REP * h], F32)
        for r in range(1, ATT_REP):
            col = jnp.where(grp == r, sink_ref[ATT_REP * h + r], col)
        return col

    def store_heads(o, h, row0, rows):
        for pr in range(ATT_REP // 2):
            pair = jnp.concatenate([o[(2 * pr) * rows:(2 * pr + 1) * rows],
                                    o[(2 * pr + 1) * rows:(2 * pr + 2) * rows]], axis=1)
            cb = (ATT_REP // 2) * h + pr
            o_ref[0, pl.ds(row0, rows), cb * LANE:(cb + 1) * LANE] = pair.astype(o_ref.dtype)

    for h in range(ATT_KV_HEADS):
        q4 = jnp.concatenate([qh[ATT_REP * h + r, 0:LC, :] for r in range(ATT_REP)], axis=0)
        kc = kh[h, 0:LC, :]
        vc = vh[h, 0:LC, :]
        o = _sink_softmax_pv([(_dot_nt(q4, kc), vc)], sink_column(h, LC))
        store_heads(o, h, 0, LC)

    nk = 3 * ATT_BLOCK
    qi = lax.broadcasted_iota(jnp.int32, (ATT_REP * ATT_BLOCK, nk), 0) & (ATT_BLOCK - 1)
    kj = lax.broadcasted_iota(jnp.int32, (ATT_REP * ATT_BLOCK, nk), 1)
    rel = kj - ATT_BLOCK - qi
    in_window = (rel <= ATT_WINDOW) & (rel >= -ATT_WINDOW)

    def block(j, carry):
        r0 = pl.multiple_of(j * ATT_BLOCK, ATT_BLOCK)
        kpos = kj + (j - 1) * ATT_BLOCK
        ok = in_window & (kpos >= 0) & (kpos < N)
        for h in range(ATT_KV_HEADS):
            q4 = jnp.concatenate([qh[ATT_REP * h + r, pl.ds(LC + r0, ATT_BLOCK), :] for r in range(ATT_REP)],
                                 axis=0)
            kb = kh[h, pl.ds(LC + r0, nk), :]
            vb = vh[h, pl.ds(LC + r0, nk), :]
            s_lat = jnp.where(ok, _dot_nt(q4, kb), NEG)
            s_ctx = _dot_nt(q4, kh[h, 0:LC, :])
            o = _sink_softmax_pv([(s_lat, vb), (s_ctx, vh[h, 0:LC, :])], sink_column(h, ATT_BLOCK))
            store_heads(o, h, LC + r0, ATT_BLOCK)
        return carry

    lax.fori_loop(0, N // ATT_BLOCK, block, 0)


def _attention(q, k, v, cos, sin, qg, kg, bd, sink, LC, N):
    B, S, _ = q.shape
    kern = functools.partial(_attn_kernel, LC=LC, N=N)
    grid_spec = pltpu.PrefetchScalarGridSpec(
        num_scalar_prefetch=1,
        grid=(B,),
        in_specs=[pl.BlockSpec((1, S, 512), lambda b, s: (b, 0, 0)),
                  pl.BlockSpec((1, S, 128), lambda b, s: (b, 0, 0)),
                  pl.BlockSpec((1, S, 128), lambda b, s: (b, 0, 0)),
                  pl.BlockSpec((N, LANE), lambda b, s: (0, 0)),
                  pl.BlockSpec((N, LANE), lambda b, s: (0, 0)),
                  pl.BlockSpec((1, LANE), lambda b, s: (0, 0)),
                  pl.BlockSpec((1, LANE), lambda b, s: (0, 0)),
                  pl.BlockSpec((LANE, LANE), lambda b, s: (0, 0))],
        out_specs=pl.BlockSpec((1, S, 512), lambda b, s: (b, 0, 0)),
        scratch_shapes=[pltpu.VMEM((ATT_HEADS, S, HEAD_DIM), BF16),
                        pltpu.VMEM((ATT_KV_HEADS, S + 2 * ATT_BLOCK, HEAD_DIM), BF16),
                        pltpu.VMEM((ATT_KV_HEADS, S + 2 * ATT_BLOCK, HEAD_DIM), BF16)])
    return pl.pallas_call(
        kern, grid_spec=grid_spec,
        out_shape=jax.ShapeDtypeStruct((B, S, 512), BF16),
        compiler_params=_cparams(("parallel",)),
        name="attention",
    )(sink, q, k, v, cos, sin, qg, kg, bd)


def _conv_kernel(glu_ref, dww_ref, dwb_ref, lng_ref, lnb_ref, pww_ref, pwb_ref, o_ref, hp, rot, *, LC, N):
    C = GROUP_W
    half = CONV_K // 2
    rows = CONV_ROWS

    def stream(s0, ln):
        hp[0:CONV_PAD, :] = jnp.zeros((CONV_PAD, C), F32)
        hp[CONV_PAD + ln:2 * CONV_PAD + ln, :] = jnp.zeros((CONV_PAD, C), F32)

        def fill(i, carry):
            r0 = pl.multiple_of(i * TM, TM)
            g = glu_ref[0, pl.ds(s0 + r0, TM), :]
            a = g[:, :C].astype(F32)
            gate = g[:, C:].astype(F32)
            hp[pl.ds(CONV_PAD + r0, TM), :] = a * jax.nn.sigmoid(gate)
            return carry

        lax.fori_loop(0, ln // TM, fill, 0)

        def chunk(i, carry):
            r0 = pl.multiple_of(i * rows, rows)
            parts = []
            for cb in range(C // LANE):
                win = hp[pl.ds(r0, rows + 2 * CONV_PAD), cb * LANE:(cb + 1) * LANE]
                for ph in range(8):
                    rot[cb, ph] = win[ph:ph + rot.shape[2]]
                acc = jnp.zeros((rows, LANE), F32)
                for t in range(CONV_K):
                    off = CONV_PAD - half + t
                    w = dww_ref[t:t + 1, cb * LANE:(cb + 1) * LANE]
                    acc = acc + w * rot[cb, off % 8, (off // 8) * 8:(off // 8) * 8 + rows, :]
                parts.append(acc)
            y = jnp.concatenate(parts, axis=1) + dwb_ref[...]
            mu = jnp.mean(y, axis=-1, keepdims=True)
            yc = y - mu
            var = jnp.mean(yc * yc, axis=-1, keepdims=True)
            z = yc * lax.rsqrt(var + EPS) * lng_ref[...] + lnb_ref[...]
            z = _silu(z)
            out = _dot(z.astype(BF16), pww_ref[...]) + pwb_ref[...]
            o_ref[0, pl.ds(s0 + r0, rows), :] = out.astype(o_ref.dtype)
            return carry

        lax.fori_loop(0, ln // rows, chunk, 0)

    stream(0, LC)
    stream(LC, N)


def _conformer(glu, dww, dwb, lng, lnb, pww, pwb, LC, N):
    B, S, _ = glu.shape
    C = GROUP_W
    kern = functools.partial(_conv_kernel, LC=LC, N=N)
    vec = pl.BlockSpec((1, C), lambda b: (0, 0))
    return pl.pallas_call(
        kern, grid=(B,),
        in_specs=[pl.BlockSpec((1, S, 2 * C), lambda b: (b, 0, 0)),
                  pl.BlockSpec((32, C), lambda b: (0, 0)),
                  vec, vec, vec,
                  pl.BlockSpec((C, C), lambda b: (0, 0)),
                  vec],
        out_specs=pl.BlockSpec((1, S, C), lambda b: (b, 0, 0)),
        out_shape=jax.ShapeDtypeStruct((B, S, C), BF16),
        scratch_shapes=[pltpu.VMEM((max(LC, N) + 2 * CONV_PAD, C), F32),
                        pltpu.VMEM((C // LANE, 8, CONV_ROWS + 2 * CONV_PAD - 8, LANE), F32)],
        compiler_params=_cparams(("parallel",)),
        name="conformer_conv",
    )(glu, dww, dwb, lng, lnb, pww, pwb)


def _ssd_kernel(xbc_ref, dt_ref, z_ref, cw_ref, cb_ref, arow_ref, dtb_ref, dsk_ref, ng_ref, o_ref,
                xp, xs, bm, cm, dts, yacc, state, *, LC, N):
    S = LC + N
    Q = SSD_CHUNK
    nc = S // Q
    nc_ctx = LC // Q
    ii = lax.broadcasted_iota(jnp.int32, (Q, Q), 0)
    jj = lax.broadcasted_iota(jnp.int32, (Q, Q), 1)
    lower = ii >= jj
    tri = (jnp.where(lower, 1.0, 0.0).astype(BF16), jnp.where(jj >= ii, 1.0, 0.0).astype(BF16))
    causal = (lower, jj >= ii)
    first_half = lax.broadcasted_iota(jnp.int32, (Q, LANE), 1) < SSD_STATE // 2

    def stream(s0, ln):
        xp[0:SSD_PAD, :] = jnp.zeros((SSD_PAD, SSD_XBC), F32)
        xp[SSD_PAD + ln:2 * SSD_PAD + ln, :] = jnp.zeros((SSD_PAD, SSD_XBC), F32)

        def fill(i, carry):
            r0 = pl.multiple_of(i * Q, Q)
            xp[pl.ds(SSD_PAD + r0, Q), :] = xbc_ref[0, pl.ds(s0 + r0, Q), :].astype(F32)
            return carry

        lax.fori_loop(0, ln // Q, fill, 0)

        def conv(i, carry):
            r0 = pl.multiple_of(i * Q, Q)
            for cb in range(SSD_XBC // LANE):
                cs = slice(cb * LANE, (cb + 1) * LANE)
                win = xp[pl.ds(r0, Q + 2 * SSD_PAD), cs]
                acc = jnp.zeros((Q, LANE), F32)
                for t in range(SSD_CONV):
                    acc = acc + cw_ref[t:t + 1, cs] * win[SSD_PAD - 2 + t:SSD_PAD - 2 + t + Q]
                y = _silu(acc + cb_ref[:, cs])
                if cb < SSD_INNER // LANE:
                    xs[pl.ds(s0 + r0, Q), cs] = y
                elif cb < (SSD_INNER + SSD_GROUPS * SSD_STATE) // LANE:
                    c2 = cb - SSD_INNER // LANE
                    bm[pl.ds(s0 + r0, Q), c2 * LANE:(c2 + 1) * LANE] = y
                else:
                    c2 = cb - (SSD_INNER + SSD_GROUPS * SSD_STATE) // LANE
                    cm[pl.ds(s0 + r0, Q), c2 * LANE:(c2 + 1) * LANE] = y
            return carry

        lax.fori_loop(0, ln // Q, conv, 0)

    stream(0, LC)
    stream(LC, N)

    def softplus_rows(i, carry):
        r0 = pl.multiple_of(i * Q, Q)
        v = dt_ref[0, pl.ds(r0, Q), :] + dtb_ref[...]
        dts[pl.ds(r0, Q), :] = jnp.maximum(v, 0.0) + jnp.log(1.0 + jnp.exp(-jnp.abs(v)))
        return carry

    lax.fori_loop(0, nc, softplus_rows, 0)

    def chunk_step(c, d):
        r0 = pl.multiple_of(c * Q, Q)
        dtc = dts[pl.ds(r0, Q), :]
        a1, a2, a3 = _split3(dtc * arow_ref[...])
        cum = _dot(tri[d], a1) + _dot(tri[d], a2) + _dot(tri[d], a3)
        cum_t = cum.T
        dt_t = dtc.T
        tot = cum[Q - 1:Q, :] if d == 0 else cum[0:1, :]
        e_cum = jnp.exp(cum)
        e_tot = jnp.exp(tot)
        w_state = jnp.exp(tot - cum) * dtc
        for g in range(SSD_GROUPS):
            bg = bm[pl.ds(r0, Q), g * SSD_STATE:(g + 1) * SSD_STATE]
            cg = cm[pl.ds(r0, Q), g * SSD_STATE:(g + 1) * SSD_STATE].astype(BF16)
            bg_t = bg.T.astype(BF16)
            cb_mat = _dot(cg, bg_t)
            for pr in range(SSD_HEADS // SSD_GROUPS // 2):
                pair = g * 2 + pr
                lanes = slice(pair * LANE, (pair + 1) * LANE)
                xpair = xs[pl.ds(r0, Q), lanes]
                y = None
                for sub in range(2):
                    hd = d * SSD_HEADS + pair * 2 + sub
                    seg = cum[:, hd:hd + 1] - cum_t[hd:hd + 1, :]
                    dec = jnp.where(causal[d], jnp.exp(jnp.minimum(seg, 0.0)), 0.0)
                    w = (cb_mat * dec * dt_t[hd:hd + 1, :]).astype(BF16)
                    xm = jnp.where(first_half if sub == 0 else ~first_half, xpair, 0.0).astype(BF16)
                    t = _dot(w, xm)
                    y = t if y is None else y + t
                h0 = d * SSD_HEADS + pair * 2
                sel = lambda m: jnp.where(first_half, m[:, h0:h0 + 1], m[:, h0 + 1:h0 + 2])
                st = state[d * (SSD_HEADS // 2) + pair]
                y = y + _dot(cg, st.astype(BF16)) * sel(e_cum)
                xw = (xpair * sel(w_state)).astype(BF16)
                state[d * (SSD_HEADS // 2) + pair] = st * sel(e_tot) + _dot(bg_t, xw)
                if d == 0:
                    yacc[pl.ds(r0, Q), lanes] = y
                else:
                    yacc[pl.ds(r0, Q), lanes] = yacc[pl.ds(r0, Q), lanes] + y

    state[...] = jnp.zeros(state.shape, F32)

    def fwd(c, carry):
        chunk_step(c, 0)
        return carry

    lax.fori_loop(0, nc, fwd, 0)

    def bwd_ctx(i, carry):
        chunk_step(nc_ctx - 1 - i, 1)
        return carry

    lax.fori_loop(0, nc_ctx, bwd_ctx, 0)

    def bwd_lat(i, carry):
        chunk_step(nc - 1 - i, 1)
        return carry

    lax.fori_loop(0, nc - nc_ctx, bwd_lat, 0)

    def gate_out(i, carry):
        r0 = pl.multiple_of(i * Q, Q)
        y = yacc[pl.ds(r0, Q), :] + xs[pl.ds(r0, Q), :] * dsk_ref[...]
        y = y * _silu(z_ref[0, pl.ds(r0, Q), :].astype(F32))
        ms = jnp.mean(y * y, axis=-1, keepdims=True)
        o_ref[0, pl.ds(r0, Q), :] = (y * lax.rsqrt(ms + EPS) * ng_ref[...]).astype(o_ref.dtype)
        return carry

    lax.fori_loop(0, nc, gate_out, 0)


def _ssd(xbc, dt, z, cw, cb, arow, dtb, dsk, ng, LC, N):
    B, S, _ = xbc.shape
    kern = functools.partial(_ssd_kernel, LC=LC, N=N)
    row = lambda w: pl.BlockSpec((1, w), lambda b: (0, 0))
    return pl.pallas_call(
        kern, grid=(B,),
        in_specs=[pl.BlockSpec((1, S, SSD_XBC), lambda b: (b, 0, 0)),
                  pl.BlockSpec((1, S, LANE), lambda b: (b, 0, 0)),
                  pl.BlockSpec((1, S, SSD_INNER), lambda b: (b, 0, 0)),
                  pl.BlockSpec((SSD_CONV, SSD_XBC), lambda b: (0, 0)),
                  row(SSD_XBC), row(LANE), row(LANE), row(SSD_INNER), row(SSD_INNER)],
        out_specs=pl.BlockSpec((1, S, SSD_INNER), lambda b: (b, 0, 0)),
        out_shape=jax.ShapeDtypeStruct((B, S, SSD_INNER), BF16),
        scratch_shapes=[pltpu.VMEM((max(LC, N) + 2 * SSD_PAD, SSD_XBC), F32),
                        pltpu.VMEM((S, SSD_INNER), F32),
                        pltpu.VMEM((S, SSD_GROUPS * SSD_STATE), F32),
                        pltpu.VMEM((S, SSD_GROUPS * SSD_STATE), F32),
                        pltpu.VMEM((S, LANE), F32),
                        pltpu.VMEM((S, SSD_INNER), F32),
                        pltpu.VMEM((SSD_HEADS, SSD_STATE, LANE), F32)],
        compiler_params=_cparams(("parallel",)),
        name="ssd",
    )(xbc, dt, z, cw, cb, arow, dtb, dsk, ng)


def _pool_kernel(pin_ref, pw_ref, ps_ref, o_ref, up, *, LC, N):
    C = GROUP_W
    rows = TM

    def stream(s0, ln):
        up[0:POOL_PAD, :] = jnp.zeros((POOL_PAD, C), F32)
        up[POOL_PAD + ln:2 * POOL_PAD + ln, :] = jnp.zeros((POOL_PAD, C), F32)

        def fill(i, carry):
            r0 = pl.multiple_of(i * rows, rows)
            up[pl.ds(POOL_PAD + r0, rows), :] = pin_ref[0, pl.ds(s0 + r0, rows), :].astype(F32)
            return carry

        lax.fori_loop(0, ln // rows, fill, 0)

        def chunk(i, carry):
            r0 = pl.multiple_of(i * rows, rows)
            t = r0 + lax.broadcasted_iota(jnp.int32, (rows, 1), 0)
            outs = []
            for gi, w in enumerate(POOL_SIZES):
                cs = slice(gi * POOL_CH, (gi + 1) * POOL_CH)
                win = up[pl.ds(r0, rows + 2 * POOL_PAD), cs]
                acc = jnp.zeros((rows, POOL_CH), F32)
                for d in range(-(w // 2), w - w // 2):
                    acc = acc + win[POOL_PAD + d:POOL_PAD + d + rows]
                cnt = jnp.minimum(t + (w - w // 2), ln) - jnp.maximum(t - w // 2, 0)
                p = acc / cnt.astype(F32) - win[POOL_PAD:POOL_PAD + rows]
                outs.append(_dot(p.astype(BF16), pw_ref[gi]))
            y = jnp.concatenate(outs, axis=1) * ps_ref[...]
            o_ref[0, pl.ds(s0 + r0, rows), :] = y.astype(o_ref.dtype)
            return carry

        lax.fori_loop(0, ln // rows, chunk, 0)

    stream(0, LC)
    stream(LC, N)


def _pool(pin, pw, ps, LC, N):
    B, S, C = pin.shape
    kern = functools.partial(_pool_kernel, LC=LC, N=N)
    return pl.pallas_call(
        kern, grid=(B,),
        in_specs=[pl.BlockSpec((1, S, C), lambda b: (b, 0, 0)),
                  pl.BlockSpec((len(POOL_SIZES), POOL_CH, POOL_CH), lambda b: (0, 0, 0)),
                  pl.BlockSpec((1, C), lambda b: (0, 0))],
        out_specs=pl.BlockSpec((1, S, C), lambda b: (b, 0, 0)),
        out_shape=jax.ShapeDtypeStruct((B, S, C), BF16),
        scratch_shapes=[pltpu.VMEM((max(LC, N) + 2 * POOL_PAD, C), F32)],
        compiler_params=_cparams(("parallel",)),
        name="pool_mixer",
    )(pin, pw, ps)


def _pack_halves(x):
    w = x.shape[1] // 2
    u = lax.bitcast_convert_type(x.astype(BF16).astype(F32), jnp.uint32)
    return (u[:, :w] >> 16) | (u[:, w:] & jnp.uint32(0xFFFF0000))


def _unpack_halves(p):
    lo = lax.bitcast_convert_type(p << 16, F32)
    hi = lax.bitcast_convert_type(p & jnp.uint32(0xFFFF0000), F32)
    return lo, hi


SUB = 8
ROW_WORDS = SUB * LANE


def _store_token_tiles(ref, packed):
    rows = packed.shape[0]
    for s in range(SUB):
        ref[pl.ds(s, rows, stride=SUB), :] = packed[:, s * LANE:(s + 1) * LANE]


def _load_token_tiles(ref, rows, s):
    return ref[pl.ds(s, rows, stride=SUB), :]


def _outproj_kernel(ya_ref, yb_ref, yc_ref, yd_ref, x_ref, mod_ref, g_ref, w_ref, wr_ref,
                    xo_ref, h2_ref, lg_ref):
    acc = None
    for i, ref in enumerate((ya_ref, yb_ref, yc_ref, yd_ref)):
        t = _dot(ref[...], w_ref[0, i * GROUP_W:(i + 1) * GROUP_W, :])
        acc = t if acc is None else acc + t
    m = mod_ref[0]
    x = x_ref[...] + m[2:3] * acc
    xo_ref[...] = x
    ms = jnp.mean(x * x, axis=-1, keepdims=True)
    h2 = x * lax.rsqrt(ms + EPS) * g_ref[...]
    h2 = h2 * (1.0 + m[4:5]) + m[3:4]
    _store_token_tiles(h2_ref, _pack_halves(h2))
    h_hi, h_lo = _split2(h2)
    both = _dot(h_hi, wr_ref[0])
    lg_ref[...] = both[:, :LANE] + (both[:, LANE:] + _dot(h_lo, wr_ref[0, :, :LANE]))


def _out_projection(ys, X2, modl, g2, w_out, wr, layer, tps, tiles_ctx, latent_only=False):
    T, D = X2.shape
    if latent_only:
        tl = tps - tiles_ctx
        steps = (T // TM) // tps * tl
        src = lambda i: ((i // tl) * tps + tiles_ctx + i % tl, 0)
        mrow = lambda i: ((i // tl) * 2 + 1, 0, 0)
        aliases = {}
    else:
        steps = T // TM
        src = lambda i: (i, 0)
        mrow = lambda i: (_mod_row(i, tps, tiles_ctx), 0, 0)
        aliases = {4: 0}
    rows = steps * TM
    ytile = pl.BlockSpec((TM, GROUP_W), src)
    return pl.pallas_call(
        _outproj_kernel,
        grid=(steps,),
        in_specs=[ytile, ytile, ytile, ytile,
                  pl.BlockSpec((TM, D), src),
                  pl.BlockSpec((1, 6, D), mrow),
                  _resident((1, D)),
                  _resident((4 * GROUP_W, D), layer),
                  _resident((D, 2 * LANE), layer)],
        out_specs=[pl.BlockSpec((TM, D), lambda i: (i, 0)),
                   pl.BlockSpec((TM * SUB, LANE), lambda i: (i, 0)),
                   pl.BlockSpec((TM, LANE), lambda i: (i, 0))],
        out_shape=[jax.ShapeDtypeStruct((rows, D), F32),
                   jax.ShapeDtypeStruct((rows * SUB, LANE), jnp.uint32),
                   jax.ShapeDtypeStruct((rows, LANE), F32)],
        input_output_aliases=aliases,
        compiler_params=_cparams(("parallel",)),
        name="out_projection",
    )(*ys, X2, modl, g2, w_out, wr)


META_E = 0
META_R = 2
META_W = 4


def _route_kernel(lg_ref, meta_ref, cnt_ref, carry):
    @pl.when(pl.program_id(0) == 0)
    def _():
        carry[...] = jnp.zeros(carry.shape, F32)

    lg = lg_ref[...]
    lane = lax.broadcasted_iota(jnp.int32, lg.shape, 1).astype(F32)
    big = 1e9
    rmax = lambda m: jnp.max(jnp.where(m, lg, NEG), axis=-1, keepdims=True)
    first = lambda m: jnp.min(jnp.where(m, lane, big), axis=-1, keepdims=True)

    gm = lane < MOE_GROUPS
    gmax = rmax(gm)
    gidx = first(gm & (lg == gmax))
    g_w = 1.0 / jnp.sum(jnp.where(gm, jnp.exp(lg - gmax), 0.0), axis=-1, keepdims=True)
    lo = MOE_GROUPS + MOE_PER_GROUP * gidx
    em = (lane >= lo) & (lane < lo + MOE_PER_GROUP)
    v1 = rmax(em)
    i1 = first(em & (lg == v1))
    em2 = em & (lane != i1)
    v2 = rmax(em2)
    i2 = first(em2 & (lg == v2))
    t = jnp.exp(v2 - v1)
    w1 = g_w / (1.0 + t)
    w2 = g_w * t / (1.0 + t)

    oh1 = jnp.where(lane == i1, 1.0, 0.0)
    oh2 = jnp.where(lane == i2, 1.0, 0.0)
    oh = oh1 + oh2
    rows = lg.shape[0]
    ri = lax.broadcasted_iota(jnp.int32, (rows, rows), 0)
    ci = lax.broadcasted_iota(jnp.int32, (rows, rows), 1)
    before = jnp.where(ci < ri, 1.0, 0.0).astype(BF16)
    base = _dot(before, oh.astype(BF16)) + carry[0:1, :]
    r1 = jnp.sum(oh1 * base, axis=-1, keepdims=True)
    r2 = jnp.sum(oh2 * base, axis=-1, keepdims=True)
    carry[0:1, :] = carry[0:1, :] + jnp.sum(oh, axis=0, keepdims=True)

    meta = jnp.zeros(lg.shape, F32)
    for k, val in enumerate((i1 - MOE_GROUPS, i2 - MOE_GROUPS, r1, r2, w1, w2)):
        meta = jnp.where(lane == k, val, meta)
    meta_ref[...] = meta
    cnt_ref[...] = jnp.broadcast_to(carry[0:1, :], cnt_ref.shape)


def _route(logits, n_tiles):
    T = logits.shape[0]
    rt = max(r for r in (4 * TM, 2 * TM, TM) if T % r == 0)
    meta, cnt = pl.pallas_call(
        _route_kernel,
        grid=(T // rt,),
        in_specs=[pl.BlockSpec((rt, LANE), lambda i: (i, 0))],
        out_specs=[pl.BlockSpec((rt, LANE), lambda i: (i, 0)),
                   pl.BlockSpec((8, LANE), lambda i: (0, 0))],
        out_shape=[jax.ShapeDtypeStruct((T, LANE), F32), jax.ShapeDtypeStruct((8, LANE), F32)],
        scratch_shapes=[pltpu.VMEM((8, LANE), F32)],
        compiler_params=_cparams(("arbitrary",)),
        name="moe_route",
    )(logits)
    routed = meta[:, :4].astype(jnp.int32)
    counts = cnt[0, MOE_GROUPS:MOE_GROUPS + MOE_EXPERTS].astype(jnp.int32)
    ntile = (counts + TE - 1) // TE
    tile_end = jnp.cumsum(ntile)
    tile_start = (tile_end - ntile).astype(jnp.int32)
    total = tile_end[-1]
    tiles = jnp.arange(n_tiles, dtype=jnp.int32)
    active = tiles < total
    last = jnp.minimum(tiles, total - 1)
    t_exp = jnp.sum((last[:, None] >= tile_end[None, :]).astype(jnp.int32), axis=1)
    t_first = (active & (tiles == jnp.sum(jnp.where(t_exp[:, None] == jnp.arange(MOE_EXPERTS)[None, :],
                                                     tile_start[None, :], 0), axis=1))).astype(jnp.int32)
    experts = jnp.arange(MOE_EXPERTS, dtype=jnp.int32)
    used = ntile > 0
    later_used = used[None, :] & (experts[None, :] > experts[:, None])
    next_used = jnp.min(jnp.where(later_used, experts[None, :], MOE_EXPERTS), axis=1)
    slot_of = (jnp.cumsum(used.astype(jnp.int32)) - 1) % 2
    per_tile = lambda table: jnp.sum(jnp.where(t_exp[:, None] == experts[None, :], table[None, :], 0),
                                     axis=1).astype(jnp.int32)
    tables = (t_exp.astype(jnp.int32), t_first, active.astype(jnp.int32), per_tile(next_used), per_tile(slot_of))
    first_tile = jnp.sum(jnp.where(routed[:, META_E:META_E + 2, None] == experts, tile_start, 0), axis=-1)
    dest = (first_tile * TE + routed[:, META_R:META_R + 2]) * SUB
    dest = dest.astype(jnp.int32).reshape(T // TM, TM, 2).transpose(0, 2, 1)
    return meta, dest, tables


def _dispatch_kernel(dest_ref, h_ref, xs_in_ref, xs_ref, sem):
    del xs_in_ref

    def row_copy(r, dst):
        return pltpu.make_async_copy(h_ref.at[pl.ds(pl.multiple_of(r * SUB, SUB), SUB)],
                                     xs_ref.at[pl.ds(pl.multiple_of(dst, SUB), SUB)], sem)

    def start(r, carry):
        for slot in range(2):
            row_copy(r, dest_ref[0, slot, r]).start(priority=slot)
        return carry

    lax.fori_loop(0, TM, start, 0, unroll=8)
    for _ in range(2 * TM):
        row_copy(0, 0).wait()


def _moe_dispatch(h2p, dest, xs_init):
    T = h2p.shape[0] // SUB
    return pl.pallas_call(
        _dispatch_kernel,
        grid=(T // TM,),
        in_specs=[pl.BlockSpec((1, 2, TM), lambda i: (i, 0, 0), memory_space=pltpu.SMEM),
                  pl.BlockSpec((TM * SUB, LANE), lambda i: (i, 0)),
                  pl.BlockSpec(memory_space=pl.ANY)],
        out_specs=pl.BlockSpec(memory_space=pl.ANY),
        out_shape=jax.ShapeDtypeStruct(xs_init.shape, jnp.uint32),
        scratch_shapes=[pltpu.SemaphoreType.DMA(())],
        input_output_aliases={2: 0},
        compiler_params=_cparams(("arbitrary",)),
        name="moe_dispatch",
    )(dest, h2p, xs_init)


def _moe_kernel(te_ref, tf_ref, tv_ref, tn_ref, ts_ref, x_ref, wg_hbm, wu_hbm, wd_hbm, o_ref,
                wg_f, wu_f, wd_f, wg_b, wu_b, wd_b, sems, *, layer):
    i = pl.program_id(0)

    def weight_copies(e, slot):
        return [pltpu.make_async_copy(src.at[layer, e], dst.at[slot], sems.at[slot])
                for src, dst in ((wg_hbm, wg_f), (wu_hbm, wu_f), (wd_hbm, wd_f))]

    @pl.when(tf_ref[i] == 1)
    def _():
        slot = ts_ref[i]

        @pl.when(i == 0)
        def _():
            for c in weight_copies(te_ref[0], 0):
                c.start()

        for c in weight_copies(te_ref[i], slot):
            c.wait()

        @pl.when(tn_ref[i] < MOE_EXPERTS)
        def _():
            for c in weight_copies(tn_ref[i], 1 - slot):
                c.start()

        wg_b[...] = wg_f[slot].astype(BF16)
        wu_b[...] = wu_f[slot].astype(BF16)
        wd_b[...] = wd_f[slot].astype(BF16)

    @pl.when(tv_ref[i] == 1)
    def _():
        words = jnp.concatenate([_load_token_tiles(x_ref, TE, s) for s in range(SUB)], axis=1)
        lo, hi = _unpack_halves(words)
        lo = lo.astype(BF16)
        hi = hi.astype(BF16)
        half = lo.shape[1]
        g = _dot(lo, wg_b[:half, :]) + _dot(hi, wg_b[half:, :])
        u = _dot(lo, wu_b[:half, :]) + _dot(hi, wu_b[half:, :])
        hid = (_silu(g) * u).astype(BF16)
        _store_token_tiles(o_ref, _pack_halves(_dot(hid, wd_b[...])))

    @pl.when(tv_ref[i] == 0)
    def _():
        o_ref[...] = jnp.zeros(o_ref.shape, o_ref.dtype)


def _moe_experts(xs, tables, w_gate, w_up, w_down, layer):
    R = xs.shape[0] // SUB
    D = 2 * ROW_WORDS
    n_tiles = R // TE
    tile = pl.BlockSpec((TE * SUB, LANE), lambda i, *_: (i, 0))
    hbm = pl.BlockSpec(memory_space=pl.ANY)
    grid_spec = pltpu.PrefetchScalarGridSpec(
        num_scalar_prefetch=len(tables),
        grid=(n_tiles,),
        in_specs=[tile, hbm, hbm, hbm],
        out_specs=tile,
        scratch_shapes=[pltpu.VMEM((2, D, D_EXPERT), F32),
                        pltpu.VMEM((2, D, D_EXPERT), F32),
                        pltpu.VMEM((2, D_EXPERT, D), F32),
                        pltpu.VMEM((D, D_EXPERT), BF16),
                        pltpu.VMEM((D, D_EXPERT), BF16),
                        pltpu.VMEM((D_EXPERT, D), BF16),
                        pltpu.SemaphoreType.DMA((2,))])
    return pl.pallas_call(
        functools.partial(_moe_kernel, layer=layer), grid_spec=grid_spec,
        out_shape=jax.ShapeDtypeStruct((R * SUB, LANE), jnp.uint32),
        compiler_params=_cparams(("arbitrary",)),
        name="moe_experts",
    )(*tables, xs, w_gate, w_up, w_down)


def _combine_kernel(dest_ref, dest_next_ref, x_ref, meta_ref, mod_ref, ye_ref, o_ref, ybuf, sems):
    i = pl.program_id(0)
    n = pl.num_programs(0)
    cur = i % 2

    def row_copy(src, buf, slot, r):
        return pltpu.make_async_copy(ye_ref.at[pl.ds(pl.multiple_of(src, SUB), SUB)],
                                     ybuf.at[buf, slot, pl.ds(pl.multiple_of(r * SUB, SUB), SUB)], sems.at[buf])

    def gather(iref, buf):
        def start(r, carry):
            for slot in range(2):
                row_copy(iref[0, slot, r], buf, slot, r).start(priority=slot)
            return carry

        lax.fori_loop(0, TM, start, 0, unroll=8)

    @pl.when(i == 0)
    def _():
        gather(dest_ref, 0)

    @pl.when(i + 1 < n)
    def _():
        gather(dest_next_ref, 1 - cur)

    for _ in range(2 * TM):
        row_copy(0, cur, 0, 0).wait()

    half = x_ref.shape[1] // 2
    meta = meta_ref[...]
    w1 = meta[:, META_W:META_W + 1]
    w2 = meta[:, META_W + 1:META_W + 2]
    g2 = mod_ref[0][5:6]
    for s in range(SUB):
        lo1, hi1 = _unpack_halves(_load_token_tiles(ybuf.at[cur, 0], TM, s))
        lo2, hi2 = _unpack_halves(_load_token_tiles(ybuf.at[cur, 1], TM, s))
        for base, a, b in ((s * LANE, lo1, lo2), (half + s * LANE, hi1, hi2)):
            cols = slice(base, base + LANE)
            o_ref[:, cols] = x_ref[:, cols] + g2[:, cols] * (w1 * a + w2 * b)


def _moe_combine(X2, ye, meta, dest, modl, tps, tiles_ctx):
    T, D = X2.shape
    steps = T // TM
    tile = pl.BlockSpec((TM, D), lambda i: (i, 0))
    return pl.pallas_call(
        _combine_kernel,
        grid=(steps,),
        in_specs=[pl.BlockSpec((1, 2, TM), lambda i: (i, 0, 0), memory_space=pltpu.SMEM),
                  pl.BlockSpec((1, 2, TM), lambda i: (jnp.minimum(i + 1, steps - 1), 0, 0),
                               memory_space=pltpu.SMEM),
                  tile,
                  pl.BlockSpec((TM, LANE), lambda i: (i, 0)),
                  pl.BlockSpec((1, 6, D), lambda i: (_mod_row(i, tps, tiles_ctx), 0, 0)),
                  pl.BlockSpec(memory_space=pl.ANY)],
        out_specs=tile,
        out_shape=jax.ShapeDtypeStruct((T, D), F32),
        scratch_shapes=[pltpu.VMEM((2, 2, TM * SUB, LANE), jnp.uint32),
                        pltpu.SemaphoreType.DMA((2,))],
        input_output_aliases={2: 0},
        compiler_params=_cparams(("arbitrary",)),
        name="moe_combine",
    )(dest, dest, X2, meta, modl, ye)


def _rope_tables(n):
    rows = n // GRID_W
    row = jnp.repeat(jnp.arange(rows, dtype=F32), GRID_W)
    col = jnp.tile(jnp.arange(GRID_W, dtype=F32), rows)
    half = HEAD_DIM // 2
    inv_freq = 1.0 / (ROPE_THETA ** (jnp.arange(0, half, 2, dtype=F32) / half))
    ar = row[:, None] * inv_freq
    ac = col[:, None] * inv_freq
    cos = jnp.concatenate([jnp.cos(ar), jnp.cos(ar), jnp.cos(ac), jnp.cos(ac)], axis=-1)
    sin = jnp.concatenate([-jnp.sin(ar), jnp.sin(ar), -jnp.sin(ac), jnp.sin(ac)], axis=-1)
    return jnp.tile(cos, (1, 2)), jnp.tile(sin, (1, 2))


def kernel(x, c, ctx, c_ctx, norm1_g, norm2_g, w_mod, b_mod, w_in, w_out, q_norm_g, k_norm_g, attn_sink,
           conv_dw_w, conv_dw_b, conv_ln_g, conv_ln_b, conv_pw_w, conv_pw_b, ssd_conv_w, ssd_conv_b, ssd_a_log,
           ssd_dt_bias, ssd_d, ssd_norm_g, pool_w, pool_scale, moe_group_router, moe_expert_router, moe_w_gate,
           moe_w_up, moe_w_down):
    B, N, D = x.shape
    LC = ctx.shape[1]
    S = LC + N
    L = w_mod.shape[0]
    assert LC % TM == 0 and N % TM == 0 and N % GRID_W == 0 and D == 2 * ROW_WORDS
    tps = S // TM
    tiles_ctx = LC // TM
    T = B * S

    rows = -(-(B + 1) // 8) * 8
    cc = jnp.concatenate([c, c_ctx[None, :], jnp.zeros((rows - B - 1, D), F32)], axis=0)
    mod = _modulation(cc, w_mod, b_mod).reshape(L, rows, 6, D)

    w_in_p = jnp.concatenate([w_in[:, :, :DT_SRC + 16], jnp.zeros((L, D, LANE - 16), F32), w_in[:, :, DT_SRC + 16:]],
                             axis=-1).astype(BF16)
    w_out_b = w_out.astype(BF16)
    w_router = jnp.concatenate([moe_group_router, moe_expert_router,
                                jnp.zeros((L, D, LANE - MOE_GROUPS - MOE_EXPERTS), F32)], axis=-1)
    wr_hi = w_router.astype(BF16)
    wr = jnp.concatenate([wr_hi, (w_router - wr_hi.astype(F32)).astype(BF16)], axis=-1)

    cos, sin = _rope_tables(N)
    bd = (jnp.arange(LANE)[:, None] // HEAD_DIM == jnp.arange(LANE)[None, :] // HEAD_DIM).astype(BF16)
    pad_lane = lambda v: jnp.concatenate([v.reshape(-1), jnp.zeros((LANE - v.size,), F32)]).reshape(1, LANE)

    X = jnp.concatenate([ctx, x], axis=1).reshape(T, D)
    n_tiles = (2 * T) // TE + MOE_EXPERTS
    xs = None
    for l in range(L):
        modl = jnp.stack([jnp.broadcast_to(mod[l, B], (B, 6, D)), mod[l, :B]], axis=1).reshape(2 * B, 6, D)
        q, k, v, glu, z, xbc, dt, pin = _in_projection(X, modl, norm1_g[l][None], w_in_p, l, tps, tiles_ctx)
        r3 = lambda a: a.reshape(B, S, a.shape[-1])
        y_att = _attention(r3(q), r3(k), r3(v), cos, sin, jnp.tile(q_norm_g[l], 2)[None], jnp.tile(k_norm_g[l], 2)[None],
                           bd, attn_sink[l], LC, N)
        dww = jnp.concatenate([conv_dw_w[l], jnp.zeros((32 - CONV_K, GROUP_W), F32)], axis=0)
        y_conv = _conformer(r3(glu), dww, conv_dw_b[l][None], conv_ln_g[l][None], conv_ln_b[l][None],
                            conv_pw_w[l].astype(BF16), conv_pw_b[l][None], LC, N)
        y_ssd = _ssd(r3(xbc), r3(dt), r3(z), ssd_conv_w[l], ssd_conv_b[l][None], pad_lane(-jnp.exp(ssd_a_log[l])),
                     pad_lane(ssd_dt_bias[l]), jnp.repeat(ssd_d[l], HEAD_DIM)[None], ssd_norm_g[l][None], LC, N)
        y_pool = _pool(r3(pin), pool_w[l].astype(BF16), pool_scale[l][None], LC, N)
        ys = [a.reshape(T, GROUP_W) for a in (y_att, y_conv, y_ssd, y_pool)]
        last = l == L - 1
        X, h2p, logits = _out_projection(ys, X, modl, norm2_g[l][None], w_out_b, wr, l, tps, tiles_ctx,
                                         latent_only=last)
        meta, dest, tables = _route(logits, n_tiles)
        if xs is None:
            xs = jnp.zeros((n_tiles * TE * SUB, LANE), jnp.uint32)
        xs = _moe_dispatch(h2p, dest, xs)
        ye = _moe_experts(xs, tables, moe_w_gate, moe_w_up, moe_w_down, l)
        if last:
            X = _moe_combine(X, ye, meta, dest, modl, tps - tiles_ctx, 0)
        else:
            X = _moe_combine(X, ye, meta, dest, modl, tps, tiles_ctx)
    return X.reshape(B, N, D)
```
